```python
import jax, jax.numpy as jnp
from jax import lax
import numpy as np

D_MODEL = 1024
BATCH = 2
SEQ = 16384
DEPTH = 1
DEC_BATCH = 32
DEC_SEQ = 2048
PAST_LEN = 128

MLA_HEADS = 8
QK_NOPE_DIM = 64
QK_ROPE_DIM = 32
V_HEAD_DIM = 64
Q_LORA_RANK = 256
KV_LORA_RANK = 128
MLA_WIDTH = MLA_HEADS * V_HEAD_DIM
ROPE_THETA = 10000.0
Q_BLOCK = 128
GMLP_HEADS = 8
GMLP_HEAD_DIM = 64
GMLP_WIDTH = GMLP_HEADS * GMLP_HEAD_DIM
CHUNK = 128
MIX_WIDTH = MLA_WIDTH + GMLP_WIDTH
IN_COLS = Q_LORA_RANK + KV_LORA_RANK + QK_ROPE_DIM + 2 * GMLP_WIDTH
SPLITS = [Q_LORA_RANK, Q_LORA_RANK + KV_LORA_RANK, Q_LORA_RANK + KV_LORA_RANK + QK_ROPE_DIM, Q_LORA_RANK + KV_LORA_RANK + QK_ROPE_DIM + GMLP_WIDTH]
N_EXPERTS = 64
TOP_K = 8
N_GROUPS = 8
TOPK_GROUPS = 4
EXPERT_DIM = 256
SHARED_DIM = 256
ROUTED_SCALE = 2.5
MOE_BLOCK = 128
N_MOD = 6
EPS = 1e-6

kernel_name = 'hybrid_mla_gmlp_moe_encoder'


def rmsnorm(x, g):
    xf = x.astype(jnp.float32)
    y = xf * lax.rsqrt(jnp.mean(xf * xf, axis=-1, keepdims=True) + EPS)
    return (y * g.astype(jnp.float32)).astype(x.dtype)


def rope_tables(seq):
    inv = 1.0 / (ROPE_THETA ** (jnp.arange(0, QK_ROPE_DIM, 2, dtype=jnp.float32) / QK_ROPE_DIM))
    ang = jnp.arange(seq, dtype=jnp.float32)[:, None] * inv[None, :]
    return jnp.cos(ang), jnp.sin(ang)


def apply_rope(x, cos, sin):
    shp = (1, cos.shape[0]) + (1,) * (x.ndim - 3) + (cos.shape[1],)
    c = cos.reshape(shp).astype(x.dtype)
    s = sin.reshape(shp).astype(x.dtype)
    x1, x2 = jnp.split(x, 2, axis=-1)
    return jnp.concatenate([x1 * c - x2 * s, x1 * s + x2 * c], axis=-1)


def mla_attention(q_nope, q_rope, k_nope, k_rope, v):
    b, s, h, _ = q_nope.shape
    nb = s // Q_BLOCK
    scale = (QK_NOPE_DIM + QK_ROPE_DIM) ** -0.5

    def to_blocks(t):
        return jnp.moveaxis(t.reshape((b, nb, Q_BLOCK) + t.shape[2:]), 1, 0)

    def block(args):
        qn, qr = args
        sc = (jnp.einsum('bqhd,bkhd->bhqk', qn, k_nope, preferred_element_type=jnp.float32)
              + jnp.einsum('bqhr,bkr->bhqk', qr, k_rope, preferred_element_type=jnp.float32))
        p = jax.nn.softmax(sc * scale, axis=-1).astype(v.dtype)
        return jnp.einsum('bhqk,bkhd->bqhd', p, v)

    out = lax.map(block, (to_blocks(q_nope), to_blocks(q_rope)))
    return jnp.moveaxis(out, 0, 1).reshape(b, s, h * V_HEAD_DIM)


def spatial_gating(u, v, g_v, w_s, b_s):
    b, s, _ = u.shape
    v = rmsnorm(v, g_v)
    vc = v.reshape(b, s // CHUNK, CHUNK, GMLP_HEADS, GMLP_HEAD_DIM)
    mixed = jnp.einsum('hij,bnjhc->bnihc', w_s, vc) + jnp.transpose(b_s)[None, None, :, :, None]
    return u * mixed.reshape(b, s, GMLP_WIDTH)


def route(h, w_router, b_router):
    t = h.shape[0]
    logits = jnp.einsum('td,de->te', h, w_router, preferred_element_type=jnp.float32)
    scores = jax.nn.sigmoid(logits)
    biased = scores + b_router.astype(jnp.float32)
    grp = biased.reshape(t, N_GROUPS, N_EXPERTS // N_GROUPS)
    grp_score = jnp.sum(lax.top_k(grp, 2)[0], axis=-1)
    _, grp_idx = lax.top_k(grp_score, TOPK_GROUPS)
    grp_mask = jnp.any(grp_idx[..., None] == jnp.arange(N_GROUPS), axis=-2)
    exp_mask = jnp.repeat(grp_mask, N_EXPERTS // N_GROUPS, axis=-1)
    masked = jnp.where(exp_mask, biased, -jnp.inf)
    _, idx = lax.top_k(masked, TOP_K)
    w = jnp.take_along_axis(scores, idx, axis=-1)
    w = w / jnp.sum(w, axis=-1, keepdims=True) * ROUTED_SCALE
    return idx, w


def routed_experts(h, idx, w, w_gate, w_up, w_down):
    t, d = h.shape
    n_assign = t * TOP_K
    n_blocks = -(-n_assign // MOE_BLOCK) + N_EXPERTS
    n_rows = n_blocks * MOE_BLOCK
    e_flat = idx.reshape(-1)
    tok_flat = jnp.arange(n_assign, dtype=jnp.int32) // TOP_K
    w_flat = w.reshape(-1)
    order = jnp.argsort(e_flat)
    e_sorted = e_flat[order]
    counts = jnp.bincount(e_flat, length=N_EXPERTS)
    padded = (counts + MOE_BLOCK - 1) // MOE_BLOCK * MOE_BLOCK
    starts = jnp.cumsum(counts) - counts
    pends = jnp.cumsum(padded)
    pstarts = pends - padded
    dest = pstarts[e_sorted] + (jnp.arange(n_assign) - starts[e_sorted])
    row_tok = jnp.zeros((n_rows,), jnp.int32).at[dest].set(tok_flat[order])
    row_w = jnp.zeros((n_rows,), w.dtype).at[dest].set(w_flat[order])
    block_exp = jnp.minimum(jnp.searchsorted(pends, jnp.arange(n_blocks) * MOE_BLOCK, side='right'), N_EXPERTS - 1)

    def block(args):
        e, toks, ws = args
        xb = h[toks]
        a = jax.nn.silu(xb @ w_gate[e]) * (xb @ w_up[e])
        return (a @ w_down[e]) * ws[:, None].astype(h.dtype)

    y = lax.map(block, (block_exp, row_tok.reshape(n_blocks, MOE_BLOCK), row_w.reshape(n_blocks, MOE_BLOCK)))
    return jnp.zeros_like(h).at[row_tok].add(y.reshape(n_rows, d))


def encoder_layer(x, c, w_ada, b_ada, g_norm1, w_in, g_q_lat, w_uq, g_kv_lat, w_ukv,
                  g_gmlp_v, w_spatial, b_spatial, g_out_attn, g_out_gmlp, w_out,
                  g_norm2, w_router, b_router, w_gate_e, w_up_e, w_down_e,
                  w_gate_s, w_up_s, w_down_s):
    b, s, d = x.shape
    mod = jnp.einsum('bd,dm->bm', jax.nn.silu(c), w_ada) + b_ada
    sh1, sc1, ga1, sh2, sc2, ga2 = jnp.split(mod[:, None, :], N_MOD, axis=-1)

    h = rmsnorm(x, g_norm1) * (1 + sc1) + sh1
    z = jnp.einsum('bsd,dn->bsn', h, w_in)
    q_lat, kv_lat, k_rope, g_u, g_v = jnp.split(z, SPLITS, axis=-1)

    cos, sin = rope_tables(s)
    q = jnp.einsum('bsr,rn->bsn', rmsnorm(q_lat, g_q_lat), w_uq).reshape(b, s, MLA_HEADS, QK_NOPE_DIM + QK_ROPE_DIM)
    q_nope, q_rope = jnp.split(q, [QK_NOPE_DIM], axis=-1)
    kv = jnp.einsum('bsr,rn->bsn', rmsnorm(kv_lat, g_kv_lat), w_ukv).reshape(b, s, MLA_HEADS, QK_NOPE_DIM + V_HEAD_DIM)
    k_nope, v = jnp.split(kv, [QK_NOPE_DIM], axis=-1)
    attn = mla_attention(q_nope, apply_rope(q_rope, cos, sin), k_nope, apply_rope(k_rope, cos, sin), v)

    gm = spatial_gating(jax.nn.gelu(g_u), jax.nn.gelu(g_v), g_gmlp_v, w_spatial, b_spatial)

    merged = jnp.concatenate([rmsnorm(attn, g_out_attn), rmsnorm(gm, g_out_gmlp)], axis=-1)
    x = x + ga1 * jnp.einsum('bsm,md->bsd', merged, w_out)

    h2 = (rmsnorm(x, g_norm2) * (1 + sc2) + sh2).reshape(b * s, d)
    idx, w = route(h2, w_router, b_router)
    routed = routed_experts(h2, idx, w, w_gate_e, w_up_e, w_down_e)
    shared = (jax.nn.silu(h2 @ w_gate_s) * (h2 @ w_up_s)) @ w_down_s
    return x + ga2 * (routed + shared).reshape(b, s, d)


def trunk(x, c, w_ada, b_ada, g_norm1, w_in, g_q_lat, w_uq, g_kv_lat, w_ukv,
          g_gmlp_v, w_spatial, b_spatial, g_out_attn, g_out_gmlp, w_out,
          g_norm2, w_router, b_router, w_gate_e, w_up_e, w_down_e,
          w_gate_s, w_up_s, w_down_s, g_final):
    for l in range(DEPTH):
        x = encoder_layer(x, c, w_ada[l], b_ada[l], g_norm1[l], w_in[l], g_q_lat[l], w_uq[l],
                          g_kv_lat[l], w_ukv[l], g_gmlp_v[l], w_spatial[l], b_spatial[l],
                          g_out_attn[l], g_out_gmlp[l], w_out[l], g_norm2[l], w_router[l],
                          b_router[l], w_gate_e[l], w_up_e[l], w_down_e[l],
                          w_gate_s[l], w_up_s[l], w_down_s[l])
    return rmsnorm(x, g_final)


def setup_inputs(seed: int = 0) -> dict:
    key = jax.random.key(seed)
    ks = jax.random.split(key, 32)
    L, D = DEPTH, D_MODEL

    def nrm(k, shape, scale):
        return jax.random.normal(k, shape, jnp.float32) * scale

    def gain(k, shape):
        return 1.0 + 0.02 * jax.random.normal(k, shape, jnp.float32)

    return {
        'x_prompt': nrm(ks[0], (BATCH, SEQ, D), 1.0),
        'x_sample': nrm(ks[1], (DEC_BATCH, DEC_SEQ, D), 1.0),
        'c_prompt': nrm(ks[2], (BATCH, D), 1.0),
        'c_sample': nrm(ks[3], (DEC_BATCH, D), 1.0),
        'w_ada': nrm(ks[4], (L, D, N_MOD * D), 0.5 * D ** -0.5),
        'b_ada': nrm(ks[5], (L, N_MOD * D), 0.02),
        'g_norm1': gain(ks[6], (L, D)),
        'w_in': nrm(ks[7], (L, D, IN_COLS), D ** -0.5),
        'g_q_lat': gain(ks[8], (L, Q_LORA_RANK)),
        'w_uq': nrm(ks[9], (L, Q_LORA_RANK, MLA_HEADS * (QK_NOPE_DIM + QK_ROPE_DIM)), Q_LORA_RANK ** -0.5),
        'g_kv_lat': gain(ks[10], (L, KV_LORA_RANK)),
        'w_ukv': nrm(ks[11], (L, KV_LORA_RANK, MLA_HEADS * (QK_NOPE_DIM + V_HEAD_DIM)), KV_LORA_RANK ** -0.5),
        'g_gmlp_v': gain(ks[12], (L, GMLP_WIDTH)),
        'w_spatial': nrm(ks[13], (L, GMLP_HEADS, CHUNK, CHUNK), CHUNK ** -0.5),
        'b_spatial': gain(ks[14], (L, GMLP_HEADS, CHUNK)),
        'g_out_attn': gain(ks[15], (L, MLA_WIDTH)),
        'g_out_gmlp': gain(ks[16], (L, GMLP_WIDTH)),
        'w_out': nrm(ks[17], (L, MIX_WIDTH, D), MIX_WIDTH ** -0.5),
        'g_norm2': gain(ks[18], (L, D)),
        'w_router': nrm(ks[19], (L, D, N_EXPERTS), D ** -0.5),
        'b_router': nrm(ks[20], (L, N_EXPERTS), 0.01),
        'w_gate_e': nrm(ks[21], (L, N_EXPERTS, D, EXPERT_DIM), D ** -0.5),
        'w_up_e': nrm(ks[22], (L, N_EXPERTS, D, EXPERT_DIM), D ** -0.5),
        'w_down_e': nrm(ks[23], (L, N_EXPERTS, EXPERT_DIM, D), EXPERT_DIM ** -0.5),
        'w_gate_s': nrm(ks[24], (L, D, SHARED_DIM), D ** -0.5),
        'w_up_s': nrm(ks[25], (L, D, SHARED_DIM), D ** -0.5),
        'w_down_s': nrm(ks[26], (L, SHARED_DIM, D), SHARED_DIM ** -0.5),
        'g_final': gain(ks[27], (D,)),
    }


def reference(x_prompt, x_sample, c_prompt, c_sample, w_ada, b_ada, g_norm1, w_in,
              g_q_lat, w_uq, g_kv_lat, w_ukv, g_gmlp_v, w_spatial, b_spatial,
              g_out_attn, g_out_gmlp, w_out, g_norm2, w_router, b_router,
              w_gate_e, w_up_e, w_down_e, w_gate_s, w_up_s, w_down_s, g_final):
    y_prompt = trunk(x_prompt, c_prompt, w_ada, b_ada, g_norm1, w_in, g_q_lat, w_uq, g_kv_lat, w_ukv,
                     g_gmlp_v, w_spatial, b_spatial, g_out_attn, g_out_gmlp, w_out, g_norm2,
                     w_router, b_router, w_gate_e, w_up_e, w_down_e, w_gate_s, w_up_s, w_down_s, g_final)
    y_sample = trunk(x_sample, c_sample, w_ada, b_ada, g_norm1, w_in, g_q_lat, w_uq, g_kv_lat, w_ukv,
                     g_gmlp_v, w_spatial, b_spatial, g_out_attn, g_out_gmlp, w_out, g_norm2,
                     w_router, b_router, w_gate_e, w_up_e, w_down_e, w_gate_s, w_up_s, w_down_s, g_final)
    return (y_prompt, y_sample)
```

```python
import functools
import math

import jax
import jax.numpy as jnp
from jax import lax
from jax.experimental import pallas as pl
from jax.experimental.pallas import tpu as pltpu

F32 = jnp.float32
BF16 = jnp.bfloat16

D_MODEL = 1024
N_HEADS = 8
QK_NOPE = 64
QK_ROPE = 32
V_DIM = 64
Q_LORA = 256
KV_LORA = 128
GMLP_W = 512
CHUNK = 128
N_EXPERTS = 64
TOP_K = 8
N_GROUPS = 8
TOPK_GROUPS = 4
GROUP_SIZE = N_EXPERTS // N_GROUPS
EXPERT_DIM = 256
SHARED_DIM = 256
ROUTED_SCALE = 2.5
ROPE_THETA = 10000.0
N_MOD = 6
EPS = 1e-6

LANES = 128
HEAD_PAD = 128
HALF_ROPE = QK_ROPE // 2
VMEM_LIMIT = 52 * 1024 * 1024

SOFTMAX_SCALE = (QK_NOPE + QK_ROPE) ** -0.5
EXP2_SCALE = SOFTMAX_SCALE * math.log2(math.e)


def _rms(x, g):
    return x * lax.rsqrt(jnp.mean(x * x, axis=-1, keepdims=True) + EPS) * g


def _sigmoid(x):
    return 1.0 / (1.0 + jnp.exp(-x))


def _gelu_tanh(x):
    c = math.sqrt(2.0 / math.pi)
    return 0.5 * x * (1.0 + jnp.tanh(c * (x + 0.044715 * (x * x * x))))


def _dot(a, b):
    return jnp.dot(a, b, preferred_element_type=F32)


def _dot_nt(a, b):
    return lax.dot_general(a, b, (((1,), (1,)), ((), ())), preferred_element_type=F32)


def _pack_bf16_pairs(x):
    c = x.shape[1] // 2
    lo = pltpu.bitcast(x[:, :c].astype(BF16).astype(F32), jnp.uint32)
    hi = pltpu.bitcast(x[:, c:].astype(BF16).astype(F32), jnp.uint32)
    return (hi & jnp.uint32(0xFFFF0000)) | (lo >> 16)


def _unpack_bf16_pairs(w):
    lo = pltpu.bitcast(w << 16, F32)
    hi = pltpu.bitcast(w & jnp.uint32(0xFFFF0000), F32)
    return lo, hi


def _mod_kernel(c_ref, w_ref, b_ref, o_ref):
    c = c_ref[...]
    a = (c * _sigmoid(c)).astype(BF16)
    o_ref[...] = _dot(a, w_ref[...]) + b_ref[...]


def _modulation(c, w_ada_bf, b_ada):
    b = c.shape[0]
    bp = max(16, -(-b // 16) * 16)
    cp = jnp.pad(c, ((0, bp - b), (0, 0)))
    n = w_ada_bf.shape[1]
    tn = D_MODEL
    out = pl.pallas_call(
        _mod_kernel,
        name="mod",
        grid=(n // tn,),
        in_specs=[
            pl.BlockSpec((bp, D_MODEL), lambda j: (0, 0)),
            pl.BlockSpec((D_MODEL, tn), lambda j: (0, j)),
            pl.BlockSpec((1, tn), lambda j: (0, j)),
        ],
        out_specs=pl.BlockSpec((bp, tn), lambda j: (0, j)),
        out_shape=jax.ShapeDtypeStruct((bp, n), F32),
        compiler_params=pltpu.CompilerParams(dimension_semantics=("arbitrary",)),
    )(cp, w_ada_bf, b_ada.reshape(1, n))
    return out[:b].reshape(b, N_MOD, D_MODEL)


def _rope(xh, c, s1, s2):
    return (xh * c + pltpu.roll(xh, LANES - HALF_ROPE, axis=1) * s1
            + pltpu.roll(xh, HALF_ROPE, axis=1) * s2)


def _pre_kernel(x_ref, mod_ref, cos_ref, s1_ref, s2_ref, g1_ref, win_ref, gq_ref, wuq_ref,
                gkv_ref, wuk_ref, wuv_ref, ggv_ref, ws_ref, bs_ref, ggo_ref,
                q_ref, k_ref, v_ref, gm_ref, mix_ref):
    ts = x_ref.shape[1]
    x = x_ref[0]
    mod = mod_ref[0]
    h = _rms(x, g1_ref[...]) * (1.0 + mod[1:2]) + mod[0:1]
    z = _dot(h.astype(BF16), win_ref[...])
    o_kv = Q_LORA
    o_gu = o_kv + KV_LORA
    o_gv = o_gu + GMLP_W
    o_kr = o_gv + GMLP_W
    q_lat = z[:, :o_kv]
    kv_lat = z[:, o_kv:o_gu]
    g_u = z[:, o_gu:o_gv]
    g_v = z[:, o_gv:o_kr]
    kr = z[:, o_kr:o_kr + LANES]

    cos = cos_ref[...]
    s1 = s1_ref[...]
    s2 = s2_ref[...]

    qn = _rms(q_lat, gq_ref[...]).astype(BF16)
    q = _dot(qn, wuq_ref[...])
    kn = _rms(kv_lat, gkv_ref[...]).astype(BF16)
    kf = _dot(kn, wuk_ref[...])
    v_ref[0] = _dot(kn, wuv_ref[...]).astype(BF16)
    krr = _rope(kr, cos, s1, s2)
    for hd in range(N_HEADS):
        sl = slice(hd * HEAD_PAD, (hd + 1) * HEAD_PAD)
        q_ref[0, :, sl] = _rope(q[:, sl], cos, s1, s2).astype(BF16)
        k_ref[0, :, sl] = (kf[:, sl] + krr).astype(BF16)

    u = _gelu_tanh(g_u)
    vn = _rms(_gelu_tanh(g_v), ggv_ref[...]).astype(BF16)
    lane = lax.broadcasted_iota(jnp.int32, (CHUNK, LANES), 1)
    left = lane < (LANES // 2)
    zero = jnp.zeros((CHUNK, LANES), BF16)
    for n in range(ts // CHUNK):
        rs = slice(n * CHUNK, (n + 1) * CHUNK)
        for p in range(GMLP_W // LANES):
            cs = slice(p * LANES, (p + 1) * LANES)
            vp = vn[rs, cs]
            rhs = jnp.concatenate([jnp.where(left, vp, zero), jnp.where(left, zero, vp)], axis=0)
            mix_ref[rs, cs] = _dot(ws_ref[p], rhs) + bs_ref[:, cs]
    gm = u * mix_ref[...]
    gm_ref[0] = _rms(gm, ggo_ref[...]).astype(BF16)


def _pre_call(x, mod, tabs, wts, ts):
    b, s, d = x.shape
    cos, s1, s2 = tabs
    (g1, win, gq, wuq, gkv, wuk, wuv, ggv, wsp, bsp, ggo) = wts
    nz = win.shape[1]
    full = lambda a: pl.BlockSpec(a.shape, lambda bi, i: (0,) * a.ndim)
    tab = pl.BlockSpec((ts, LANES), lambda bi, i: (i, 0))
    out_shapes = (
        jax.ShapeDtypeStruct((b, s, N_HEADS * HEAD_PAD), BF16),
        jax.ShapeDtypeStruct((b, s, N_HEADS * HEAD_PAD), BF16),
        jax.ShapeDtypeStruct((b, s, N_HEADS * V_DIM), BF16),
        jax.ShapeDtypeStruct((b, s, GMLP_W), BF16),
    )
    tok = lambda w: pl.BlockSpec((1, ts, w), lambda bi, i: (bi, i, 0))
    return pl.pallas_call(
        _pre_kernel,
        name="pre",
        grid=(b, s // ts),
        in_specs=[tok(d), pl.BlockSpec((1, N_MOD, d), lambda bi, i: (bi, 0, 0)), tab, tab, tab,
                  full(g1), full(win), full(gq), full(wuq), full(gkv), full(wuk), full(wuv),
                  full(ggv), full(wsp), full(bsp), full(ggo)],
        out_specs=(tok(N_HEADS * HEAD_PAD), tok(N_HEADS * HEAD_PAD), tok(N_HEADS * V_DIM), tok(GMLP_W)),
        out_shape=out_shapes,
        scratch_shapes=[pltpu.VMEM((ts, GMLP_W), F32)],
        compiler_params=pltpu.CompilerParams(dimension_semantics=("arbitrary", "arbitrary"),
                                             vmem_limit_bytes=VMEM_LIMIT),
    )(x, mod, cos, s1, s2, g1, win, gq, wuq, gkv, wuk, wuv, ggv, wsp, bsp, ggo)


def _attn_kernel(q_ref, k_ref, v_ref, o_ref, *, tkc):
    tq = q_ref.shape[1]
    s_len = k_ref.shape[1]
    n_chunks = s_len // tkc
    outs = []
    for hh in range(2):
        hs = slice(hh * HEAD_PAD, (hh + 1) * HEAD_PAD)
        q = q_ref[0, :, hs]

        def step(c, carry):
            m, l, acc = carry
            off = pl.multiple_of(c * tkc, tkc)
            kc = k_ref[0, pl.ds(off, tkc), hs]
            vc = v_ref[0, pl.ds(off, tkc), :]
            sc = _dot_nt(q, kc)
            m_new = jnp.maximum(m, jnp.max(sc, axis=-1, keepdims=True))
            p = jnp.exp2((sc - m_new) * EXP2_SCALE)
            alpha = jnp.exp2((m - m_new) * EXP2_SCALE)
            l_new = alpha * l + jnp.sum(p, axis=-1, keepdims=True)
            acc_new = alpha * acc + _dot(p.astype(BF16), vc)
            return m_new, l_new, acc_new

        init = (jnp.full((tq, 1), -jnp.inf, F32), jnp.zeros((tq, 1), F32),
                jnp.zeros((tq, 2 * V_DIM), F32))
        if n_chunks == 1:
            m, l, acc = step(0, init)
        else:
            m, l, acc = lax.fori_loop(0, n_chunks, step, init)
        outs.append(acc / l)
    lane = lax.broadcasted_iota(jnp.int32, (tq, 2 * V_DIM), 1)
    o_ref[0] = jnp.where(lane < V_DIM, outs[0], outs[1]).astype(BF16)


def _attn_call(q, k, v, tq, tkc):
    b, s, _ = q.shape
    hp = N_HEADS // 2
    return pl.pallas_call(
        functools.partial(_attn_kernel, tkc=tkc),
        name="attn",
        grid=(b, hp, s // tq),
        in_specs=[
            pl.BlockSpec((1, tq, 2 * HEAD_PAD), lambda bi, h, i: (bi, i, h)),
            pl.BlockSpec((1, s, 2 * HEAD_PAD), lambda bi, h, i: (bi, 0, h)),
            pl.BlockSpec((1, s, 2 * V_DIM), lambda bi, h, i: (bi, 0, h)),
        ],
        out_specs=pl.BlockSpec((1, tq, 2 * V_DIM), lambda bi, h, i: (bi, i, h)),
        out_shape=jax.ShapeDtypeStruct((b, s, N_HEADS * V_DIM), BF16),
        compiler_params=pltpu.CompilerParams(
            dimension_semantics=("arbitrary", "arbitrary", "arbitrary"),
            vmem_limit_bytes=VMEM_LIMIT),
    )(q, k, v)


def _post_kernel(attn_ref, gm_ref, x_ref, mod_ref, goa_ref, woa_ref, wog_ref, g2_ref, wr_ref,
                 br_ref, wgus_ref, wds_ref,
                 ybase_ref, h2p_ref, e_ref, rank_ref, w_ref, cnt_ref, run_ref):
    ts = x_ref.shape[1]
    first = jnp.logical_and(pl.program_id(0) == 0, pl.program_id(1) == 0)

    @pl.when(first)
    def _():
        run_ref[...] = jnp.zeros_like(run_ref)

    mod = mod_ref[0]
    ga1, sh2, sc2, ga2 = mod[2:3], mod[3:4], mod[4:5], mod[5:6]
    an = _rms(attn_ref[0].astype(F32), goa_ref[...]).astype(BF16)
    y = _dot(an, woa_ref[...]) + _dot(gm_ref[0], wog_ref[...])
    x1 = x_ref[0] + ga1 * y
    h2 = _rms(x1, g2_ref[...]) * (1.0 + sc2) + sh2
    h2b = h2.astype(BF16)
    h2p_ref[...] = _pack_bf16_pairs(h2)

    gu = _dot(h2b, wgus_ref[...])
    g, u = gu[:, :SHARED_DIM], gu[:, SHARED_DIM:]
    a = (g * _sigmoid(g) * u).astype(BF16)
    ybase_ref[0] = x1 + ga2 * _dot(a, wds_ref[...])

    logits = _dot_nt(wr_ref[...], h2b)
    scores = _sigmoid(logits)
    biased = scores + br_ref[...]
    ninf = jnp.float32(-jnp.inf)
    bj = [biased[j * N_GROUPS:(j + 1) * N_GROUPS] for j in range(GROUP_SIZE)]
    sj = [scores[j * N_GROUPS:(j + 1) * N_GROUPS] for j in range(GROUP_SIZE)]
    m1 = bj[0]
    for j in range(1, GROUP_SIZE):
        m1 = jnp.maximum(m1, bj[j])
    found = jnp.zeros_like(m1)
    m2 = jnp.full_like(m1, ninf)
    for j in range(GROUP_SIZE):
        eq = jnp.where(bj[j] == m1, 1.0, 0.0)
        is_first = eq * (1.0 - found)
        found = jnp.maximum(found, eq)
        m2 = jnp.maximum(m2, jnp.where(is_first > 0.0, ninf, bj[j]))
    gs = m1 + m2
    gidx = lax.broadcasted_iota(jnp.int32, gs.shape, 0)
    grank = jnp.zeros_like(gs)
    for kk in range(1, N_GROUPS):
        r = pltpu.roll(gs, kk, axis=0)
        grank = grank + jnp.where(gidx >= kk, jnp.where(r >= gs, 1.0, 0.0), jnp.where(r > gs, 1.0, 0.0))
    gsel = grank < float(TOPK_GROUPS)
    masked = [jnp.where(gsel, bj[j], ninf) for j in range(GROUP_SIZE)]
    eidx = [gidx * GROUP_SIZE + j for j in range(GROUP_SIZE)]

    selm = [jnp.zeros_like(gs) for _ in range(GROUP_SIZE)]
    e_sel = []
    for _k in range(TOP_K):
        m = masked[0]
        for j in range(1, GROUP_SIZE):
            m = jnp.maximum(m, masked[j])
        m = jnp.max(m, axis=0, keepdims=True)
        cand = jnp.where(masked[0] == m, eidx[0], N_EXPERTS)
        for j in range(1, GROUP_SIZE):
            cand = jnp.minimum(cand, jnp.where(masked[j] == m, eidx[j], N_EXPERTS))
        emin = jnp.min(cand, axis=0, keepdims=True)
        e_sel.append(emin)
        for j in range(GROUP_SIZE):
            hit = eidx[j] == emin
            selm[j] = jnp.where(hit, 1.0, selm[j])
            masked[j] = jnp.where(hit, ninf, masked[j])

    wsel = [selm[j] * sj[j] for j in range(GROUP_SIZE)]
    tot = wsel[0]
    for j in range(1, GROUP_SIZE):
        tot = tot + wsel[j]
    tot = jnp.sum(tot, axis=0, keepdims=True)
    wn = [wsel[j] / tot * ROUTED_SCALE for j in range(GROUP_SIZE)]

    sel = jnp.concatenate(selm, axis=0)
    tr = lax.broadcasted_iota(jnp.int32, (ts, ts), 0)
    tc = lax.broadcasted_iota(jnp.int32, (ts, ts), 1)
    upper = jnp.where(tr < tc, 1.0, 0.0).astype(BF16)
    run = run_ref[...]
    rank_full = _dot(sel.astype(BF16), upper) + run[:, 0:1]
    run_new = run + jnp.sum(sel, axis=1, keepdims=True)
    run_ref[...] = run_new
    cnt_ref[...] = run_new
    rj = [rank_full[j * N_GROUPS:(j + 1) * N_GROUPS] for j in range(GROUP_SIZE)]

    for k in range(TOP_K):
        rk = jnp.zeros_like(gs)
        wk = jnp.zeros_like(gs)
        for j in range(GROUP_SIZE):
            hit = eidx[j] == e_sel[k]
            rk = rk + jnp.where(hit, rj[j], 0.0)
            wk = wk + jnp.where(hit, wn[j], 0.0)
        e_ref[0, k:k + 1, :] = e_sel[k]
        rank_ref[0, k:k + 1, :] = jnp.sum(rk, axis=0, keepdims=True).astype(jnp.int32)
        w_ref[0, k:k + 1, :] = jnp.sum(wk, axis=0, keepdims=True)


def _post_call(attn, gm, x, mod, wts, ts):
    b, s, d = x.shape
    nt = s // ts
    (goa, woa, wog, g2, wr, br, wgus, wds) = wts
    full = lambda a: pl.BlockSpec(a.shape, lambda bi, i: (0,) * a.ndim)
    tok = lambda w: pl.BlockSpec((1, ts, w), lambda bi, i: (bi, i, 0))
    rout = pl.BlockSpec((1, TOP_K, ts), lambda bi, i: (bi * nt + i, 0, 0))
    out_shapes = (
        jax.ShapeDtypeStruct((b, s, d), F32),
        jax.ShapeDtypeStruct((b * s, d // 2), jnp.uint32),
        jax.ShapeDtypeStruct((b * nt, TOP_K, ts), jnp.int32),
        jax.ShapeDtypeStruct((b * nt, TOP_K, ts), jnp.int32),
        jax.ShapeDtypeStruct((b * nt, TOP_K, ts), F32),
        jax.ShapeDtypeStruct((N_EXPERTS, LANES), F32),
    )
    return pl.pallas_call(
        _post_kernel,
        name="post",
        grid=(b, nt),
        in_specs=[tok(N_HEADS * V_DIM), tok(GMLP_W), tok(d),
                  pl.BlockSpec((1, N_MOD, d), lambda bi, i: (bi, 0, 0)),
                  full(goa), full(woa), full(wog), full(g2), full(wr), full(br), full(wgus), full(wds)],
        out_specs=(tok(d), pl.BlockSpec((ts, d // 2), lambda bi, i: (bi * nt + i, 0)),
                   rout, rout, rout, pl.BlockSpec((N_EXPERTS, LANES), lambda bi, i: (0, 0))),
        out_shape=out_shapes,
        scratch_shapes=[pltpu.VMEM((N_EXPERTS, LANES), F32)],
        compiler_params=pltpu.CompilerParams(dimension_semantics=("arbitrary", "arbitrary"),
                                             vmem_limit_bytes=VMEM_LIMIT),
    )(attn, gm, x, mod, goa, woa, wog, g2, wr, br, wgus, wds)


def _disp_kernel(dest_ref, h_ref, xs_in_ref, xs_ref, sem):
    del xs_in_ref
    td = h_ref.shape[0]

    def row_copy(t, d):
        return pltpu.make_async_copy(h_ref.at[pl.ds(t, 1), :], xs_ref.at[pl.ds(d, 1), :], sem)

    def issue(t, c):
        for k in range(TOP_K):
            row_copy(t, dest_ref[0, k, t]).start()
        return c

    lax.fori_loop(0, td, issue, 0)

    def drain(t, c):
        for k in range(TOP_K):
            row_copy(t, dest_ref[0, k, t]).wait()
        return c

    lax.fori_loop(0, td, drain, 0)


def _disp_call(dest, h2p, n_rows, td):
    t, w = h2p.shape
    nt = t // td
    xs0 = jnp.zeros((n_rows, w), jnp.uint32)
    return pl.pallas_call(
        _disp_kernel,
        name="disp",
        grid=(nt,),
        in_specs=[pl.BlockSpec((1, TOP_K, td), lambda i: (i, 0, 0), memory_space=pltpu.SMEM),
                  pl.BlockSpec((td, w), lambda i: (i, 0)),
                  pl.BlockSpec(memory_space=pl.ANY)],
        out_specs=pl.BlockSpec(memory_space=pl.ANY),
        out_shape=jax.ShapeDtypeStruct((n_rows, w), jnp.uint32),
        scratch_shapes=[pltpu.SemaphoreType.DMA(())],
        input_output_aliases={2: 0},
        compiler_params=pltpu.CompilerParams(dimension_semantics=("arbitrary",)),
    )(dest, h2p, xs0)


def _exp_kernel(bexp_ref, nused_ref, xs_ref, wgu_ref, wd_ref, ys_ref):
    i = pl.program_id(0)

    @pl.when(i < nused_ref[0])
    def _():
        half = D_MODEL // 2
        lo, hi = _unpack_bf16_pairs(xs_ref[...])
        gu = (_dot(lo.astype(BF16), wgu_ref[0, :half, :])
              + _dot(hi.astype(BF16), wgu_ref[0, half:, :]))
        g, u = gu[:, :EXPERT_DIM], gu[:, EXPERT_DIM:]
        a = (g * _sigmoid(g) * u).astype(BF16)
        ys_ref[...] = _pack_bf16_pairs(_dot(a, wd_ref[0]))


def _exp_call(block_exp, n_used, xs, wgu, wd, blk):
    n_rows, w = xs.shape
    n_blocks = n_rows // blk

    def row_map(i, bexp, nused):
        return (jnp.minimum(i, nused[0] - 1), 0)

    def w_map(i, bexp, nused):
        return (bexp[jnp.minimum(i, nused[0] - 1)], 0, 0)

    return pl.pallas_call(
        _exp_kernel,
        name="exp",
        grid_spec=pltpu.PrefetchScalarGridSpec(
            num_scalar_prefetch=2,
            grid=(n_blocks,),
            in_specs=[pl.BlockSpec((blk, w), row_map),
                      pl.BlockSpec((1, D_MODEL, 2 * EXPERT_DIM), w_map),
                      pl.BlockSpec((1, EXPERT_DIM, D_MODEL), w_map)],
            out_specs=pl.BlockSpec((blk, w), row_map),
        ),
        out_shape=jax.ShapeDtypeStruct((n_rows, w), jnp.uint32),
        compiler_params=pltpu.CompilerParams(dimension_semantics=("arbitrary",),
                                             vmem_limit_bytes=VMEM_LIMIT),
    )(block_exp, n_used, xs, wgu, wd)


def _comb_kernel(dest_ref, ybase_ref, mod_ref, w_ref, gf_ref, ys_ref, o_ref, buf, sem):
    tc = ybase_ref.shape[1]

    def row_copy(t, k, d):
        return pltpu.make_async_copy(ys_ref.at[pl.ds(d, 1), :], buf.at[k, pl.ds(t, 1), :], sem)

    def issue(t, c):
        for k in range(TOP_K):
            row_copy(t, k, dest_ref[0, k, t]).start()
        return c

    lax.fori_loop(0, tc, issue, 0)

    def drain(t, c):
        for k in range(TOP_K):
            row_copy(t, k, dest_ref[0, k, t]).wait()
        return c

    lax.fori_loop(0, tc, drain, 0)

    w = w_ref[...]
    half = D_MODEL // 2
    acc_lo = jnp.zeros((tc, half), F32)
    acc_hi = jnp.zeros((tc, half), F32)
    for k in range(TOP_K):
        lo, hi = _unpack_bf16_pairs(buf[k])
        wk = w[:, k:k + 1]
        acc_lo = acc_lo + wk * lo
        acc_hi = acc_hi + wk * hi
    routed = jnp.concatenate([acc_lo, acc_hi], axis=1)
    ga2 = mod_ref[0][5:6]
    o_ref[0] = _rms(ybase_ref[0] + ga2 * routed, gf_ref[...])


def _comb_call(dest, ybase, mod, w_tok, g_final, ys, tc):
    b, s, d = ybase.shape
    nt = s // tc
    return pl.pallas_call(
        _comb_kernel,
        name="comb",
        grid=(b, nt),
        in_specs=[pl.BlockSpec((1, TOP_K, tc), lambda bi, i: (bi * nt + i, 0, 0), memory_space=pltpu.SMEM),
                  pl.BlockSpec((1, tc, d), lambda bi, i: (bi, i, 0)),
                  pl.BlockSpec((1, N_MOD, d), lambda bi, i: (bi, 0, 0)),
                  pl.BlockSpec((tc, TOP_K), lambda bi, i: (bi * nt + i, 0)),
                  pl.BlockSpec((1, d), lambda bi, i: (0, 0)),
                  pl.BlockSpec(memory_space=pl.ANY)],
        out_specs=pl.BlockSpec((1, tc, d), lambda bi, i: (bi, i, 0)),
        out_shape=jax.ShapeDtypeStruct((b, s, d), F32),
        scratch_shapes=[pltpu.VMEM((TOP_K, tc, d // 2), jnp.uint32), pltpu.SemaphoreType.DMA(())],
        compiler_params=pltpu.CompilerParams(dimension_semantics=("arbitrary", "arbitrary"),
                                             vmem_limit_bytes=VMEM_LIMIT),
    )(dest, ybase, mod, w_tok, g_final, ys)


def _prep_weights(w_ada, b_ada, g_norm1, w_in, g_q_lat, w_uq, g_kv_lat, w_ukv, g_gmlp_v, w_spatial,
                  b_spatial, g_out_attn, g_out_gmlp, w_out, g_norm2, w_router, b_router, w_gate_e,
                  w_up_e, w_down_e, w_gate_s, w_up_s, w_down_s, g_final):
    row = lambda g: g.reshape(1, -1).astype(F32)
    o1 = Q_LORA
    o2 = o1 + KV_LORA
    o3 = o2 + QK_ROPE
    o4 = o3 + GMLP_W
    kr_cols = jnp.pad(w_in[:, o2:o3], ((0, 0), (QK_NOPE, LANES - QK_NOPE - QK_ROPE)))
    win = jnp.concatenate([w_in[:, :o2], w_in[:, o3:o4], w_in[:, o4:], kr_cols], axis=1).astype(BF16)
    qd = QK_NOPE + QK_ROPE
    wuq = jnp.pad(w_uq.reshape(Q_LORA, N_HEADS, qd), ((0, 0), (0, 0), (0, HEAD_PAD - qd)))
    wuq = wuq.reshape(Q_LORA, N_HEADS * HEAD_PAD).astype(BF16)
    wkv = w_ukv.reshape(KV_LORA, N_HEADS, QK_NOPE + V_DIM)
    wuk = jnp.pad(wkv[:, :, :QK_NOPE], ((0, 0), (0, 0), (0, HEAD_PAD - QK_NOPE)))
    wuk = wuk.reshape(KV_LORA, N_HEADS * HEAD_PAD).astype(BF16)
    wuv = wkv[:, :, QK_NOPE:].reshape(KV_LORA, N_HEADS * V_DIM).astype(BF16)
    wsp = w_spatial.reshape(N_HEADS // 2, 2, CHUNK, CHUNK).transpose(0, 2, 1, 3)
    wsp = wsp.reshape(N_HEADS // 2, CHUNK, 2 * CHUNK).astype(BF16)
    bsp = jnp.repeat(jnp.transpose(b_spatial), GMLP_W // N_HEADS, axis=1).astype(F32)
    pre = (row(g_norm1), win, row(g_q_lat), wuq, row(g_kv_lat), wuk, wuv, row(g_gmlp_v), wsp, bsp,
           row(g_out_gmlp))
    perm = (jnp.arange(N_GROUPS)[None, :] * GROUP_SIZE + jnp.arange(GROUP_SIZE)[:, None]).reshape(-1)
    wr = jnp.transpose(w_router)[perm].astype(BF16)
    br = b_router.astype(F32)[perm].reshape(N_EXPERTS, 1)
    mla_w = N_HEADS * V_DIM
    wgus = jnp.concatenate([w_gate_s, w_up_s], axis=1).astype(BF16)
    post = (row(g_out_attn), w_out[:mla_w].astype(BF16), w_out[mla_w:].astype(BF16), row(g_norm2),
            wr, br, wgus, w_down_s.astype(BF16))
    wgu_e = jnp.concatenate([w_gate_e, w_up_e], axis=2).astype(BF16)
    wd_e = w_down_e.astype(BF16)
    return w_ada.astype(BF16), b_ada, pre, post, (wgu_e, wd_e), row(g_final)


def _rope_tables(s):
    inv = 1.0 / (ROPE_THETA ** (jnp.arange(0, QK_ROPE, 2, dtype=F32) / QK_ROPE))
    ang = jnp.arange(s, dtype=F32)[:, None] * inv[None, :]
    cos, sin = jnp.cos(ang), jnp.sin(ang)
    z = lambda n: jnp.zeros((s, n), F32)
    tail = LANES - QK_NOPE - QK_ROPE
    c = jnp.concatenate([jnp.ones((s, QK_NOPE), F32), cos, cos, z(tail)], axis=1)
    s1 = jnp.concatenate([z(QK_NOPE), -sin, z(HALF_ROPE), z(tail)], axis=1)
    s2 = jnp.concatenate([z(QK_NOPE), z(HALF_ROPE), sin, z(tail)], axis=1)
    return c, s1, s2


def _tiles(s):
    ts = min(512, s)
    tq = min(512, s)
    tkc = min(1024, s)
    blk = 512
    return ts, tq, tkc, blk


def _trunk(x, c, prep, tiles=None):
    w_ada, b_ada, pre_w, post_w, exp_w, g_final = prep
    b, s, d = x.shape
    ts, tq, tkc, blk = tiles or _tiles(s)
    t = b * s
    nt = s // ts
    mod = _modulation(c, w_ada, b_ada)
    q, k, v, gm = _pre_call(x, mod, _rope_tables(s), pre_w, ts)
    attn = _attn_call(q, k, v, tq, tkc)
    ybase, h2p, e_arr, rank_arr, w_arr, cnt = _post_call(attn, gm, x, mod, post_w, ts)

    counts = cnt[:, 0].astype(jnp.int32).reshape(GROUP_SIZE, N_GROUPS).T.reshape(N_EXPERTS)
    padded = (counts + blk - 1) // blk * blk
    pends = jnp.cumsum(padded)
    pstart = pends - padded
    n_blocks = -(-(t * TOP_K) // blk) + N_EXPERTS
    n_rows = n_blocks * blk
    dest = pstart[e_arr] + rank_arr
    n_used = (pends[-1] // blk).astype(jnp.int32).reshape(1)
    block_exp = jnp.minimum(
        jnp.searchsorted(pends, jnp.arange(n_blocks, dtype=jnp.int32) * blk, side='right'),
        N_EXPERTS - 1).astype(jnp.int32)
    w_tok = jnp.transpose(w_arr, (0, 2, 1)).reshape(t, TOP_K)

    xs = _disp_call(dest, h2p, n_rows, ts)
    ys = _exp_call(block_exp, n_used, xs, exp_w[0], exp_w[1], blk)
    return _comb_call(dest, ybase, mod, w_tok, g_final, ys, ts)


def kernel(x_prompt, x_sample, c_prompt, c_sample, w_ada, b_ada, g_norm1, w_in, g_q_lat, w_uq, g_kv_lat, w_ukv, g_gmlp_v, w_spatial, b_spatial, g_out_attn, g_out_gmlp, w_out, g_norm2, w_router, b_router, w_gate_e, w_up_e, w_down_e, w_gate_s, w_up_s, w_down_s, g_final):
    prep = _prep_weights(w_ada[0], b_ada[0], g_norm1[0], w_in[0], g_q_lat[0], w_uq[0], g_kv_lat[0],
                         w_ukv[0], g_gmlp_v[0], w_spatial[0], b_spatial[0], g_out_attn[0],
                         g_out_gmlp[0], w_out[0], g_norm2[0], w_router[0], b_router[0], w_gate_e[0],
                         w_up_e[0], w_down_e[0], w_gate_s[0], w_up_s[0], w_down_s[0], g_final)
    return (_trunk(x_prompt, c_prompt, prep), _trunk(x_sample, c_sample, prep))
```

```python
import functools
import math

import jax
import jax.numpy as jnp
from jax import lax
from jax.experimental import pallas as pl
from jax.experimental.pallas import tpu as pltpu

F32 = jnp.float32
BF16 = jnp.bfloat16

D_MODEL = 1024
N_HEADS = 8
QK_NOPE = 64
QK_ROPE = 32
V_DIM = 64
Q_LORA = 256
KV_LORA = 128
GMLP_W = 512
CHUNK = 128
N_EXPERTS = 64
TOP_K = 8
N_GROUPS = 8
TOPK_GROUPS = 4
GROUP_SIZE = N_EXPERTS // N_GROUPS
EXPERT_DIM = 256
SHARED_DIM = 256
ROUTED_SCALE = 2.5
ROPE_THETA = 10000.0
N_MOD = 6
EPS = 1e-6

LANES = 128
HEAD_PAD = 128
HALF_ROPE = QK_ROPE // 2
VMEM_LIMIT = 52 * 1024 * 1024

SOFTMAX_SCALE = (QK_NOPE + QK_ROPE) ** -0.5
EXP2_SCALE = SOFTMAX_SCALE * math.log2(math.e)


def _rms(x, g):
    return x * lax.rsqrt(jnp.mean(x * x, axis=-1, keepdims=True) + EPS) * g


def _sigmoid(x):
    return 1.0 / (1.0 + jnp.exp(-x))


def _gelu_tanh(x):
    c = math.sqrt(2.0 / math.pi)
    return 0.5 * x * (1.0 + jnp.tanh(c * (x + 0.044715 * (x * x * x))))


def _dot(a, b):
    return jnp.dot(a, b, preferred_element_type=F32)


def _dot_nt(a, b):
    return lax.dot_general(a, b, (((1,), (1,)), ((), ())), preferred_element_type=F32)


def _pack_bf16_pairs(x):
    c = x.shape[1] // 2
    lo = pltpu.bitcast(x[:, :c].astype(BF16).astype(F32), jnp.uint32)
    hi = pltpu.bitcast(x[:, c:].astype(BF16).astype(F32), jnp.uint32)
    return (hi & jnp.uint32(0xFFFF0000)) | (lo >> 16)


def _unpack_bf16_pairs(w):
    lo = pltpu.bitcast(w << 16, F32)
    hi = pltpu.bitcast(w & jnp.uint32(0xFFFF0000), F32)
    return lo, hi


def _mod_kernel(c_ref, w_ref, b_ref, o_ref):
    c = c_ref[...]
    a = (c * _sigmoid(c)).astype(BF16)
    o_ref[...] = _dot(a, w_ref[...]) + b_ref[...]


def _modulation(c, w_ada_bf, b_ada):
    b = c.shape[0]
    bp = max(16, -(-b // 16) * 16)
    cp = jnp.pad(c, ((0, bp - b), (0, 0)))
    n = w_ada_bf.shape[1]
    tn = D_MODEL
    out = pl.pallas_call(
        _mod_kernel,
        name="mod",
        grid=(n // tn,),
        in_specs=[
            pl.BlockSpec((bp, D_MODEL), lambda j: (0, 0)),
            pl.BlockSpec((D_MODEL, tn), lambda j: (0, j)),
            pl.BlockSpec((1, tn), lambda j: (0, j)),
        ],
        out_specs=pl.BlockSpec((bp, tn), lambda j: (0, j)),
        out_shape=jax.ShapeDtypeStruct((bp, n), F32),
        compiler_params=pltpu.CompilerParams(dimension_semantics=("arbitrary",)),
    )(cp, w_ada_bf, b_ada.reshape(1, n))
    return out[:b].reshape(b, N_MOD, D_MODEL)


def _rope(xh, c, s1, s2):
    return (xh * c + pltpu.roll(xh, LANES - HALF_ROPE, axis=1) * s1
            + pltpu.roll(xh, HALF_ROPE, axis=1) * s2)


def _pre_kernel(x_ref, mod_ref, cos_ref, s1_ref, s2_ref, g1_ref, win_ref, gq_ref, wuq_ref,
                gkv_ref, wuk_ref, wuv_ref, vone_ref, ggv_ref, ws_ref, bs_ref, ggo_ref,
                q_ref, k_ref, v_ref, gm_ref, mix_ref):
    ts = x_ref.shape[1]
    x = x_ref[0]
    mod = mod_ref[0]
    h = _rms(x, g1_ref[...]) * (1.0 + mod[1:2]) + mod[0:1]
    z = _dot(h.astype(BF16), win_ref[...])
    o_kv = Q_LORA
    o_gu = o_kv + KV_LORA
    o_gv = o_gu + GMLP_W
    o_kr = o_gv + GMLP_W
    q_lat = z[:, :o_kv]
    kv_lat = z[:, o_kv:o_gu]
    g_u = z[:, o_gu:o_gv]
    g_v = z[:, o_gv:o_kr]
    kr = z[:, o_kr:o_kr + LANES]

    cos = cos_ref[...]
    s1 = s1_ref[...]
    s2 = s2_ref[...]

    qn = _rms(q_lat, gq_ref[...]).astype(BF16)
    q = _dot(qn, wuq_ref[...])
    kn = _rms(kv_lat, gkv_ref[...]).astype(BF16)
    kf = _dot(kn, wuk_ref[...])
    v_ref[0] = (_dot(kn, wuv_ref[...]) + vone_ref[...]).astype(BF16)
    krr = _rope(kr, cos, s1, s2)
    for hd in range(N_HEADS):
        sl = slice(hd * HEAD_PAD, (hd + 1) * HEAD_PAD)
        q_ref[0, :, sl] = (_rope(q[:, sl], cos, s1, s2) * EXP2_SCALE).astype(BF16)
        k_ref[0, :, sl] = (kf[:, sl] + krr).astype(BF16)

    u = _gelu_tanh(g_u)
    vn = _rms(_gelu_tanh(g_v), ggv_ref[...]).astype(BF16)
    lane = lax.broadcasted_iota(jnp.int32, (CHUNK, LANES), 1)
    left = lane < (LANES // 2)
    zero = jnp.zeros((CHUNK, LANES), BF16)
    for n in range(ts // CHUNK):
        rs = slice(n * CHUNK, (n + 1) * CHUNK)
        for p in range(GMLP_W // LANES):
            cs = slice(p * LANES, (p + 1) * LANES)
            vp = vn[rs, cs]
            rhs = jnp.concatenate([jnp.where(left, vp, zero), jnp.where(left, zero, vp)], axis=0)
            mix_ref[rs, cs] = _dot(ws_ref[p], rhs) + bs_ref[:, cs]
    gm = u * mix_ref[...]
    gm_ref[0] = _rms(gm, ggo_ref[...]).astype(BF16)


def _pre_call(x, mod, tabs, wts, ts):
    b, s, d = x.shape
    cos, s1, s2 = tabs
    (g1, win, gq, wuq, gkv, wuk, wuv, vone, ggv, wsp, bsp, ggo) = wts
    full = lambda a: pl.BlockSpec(a.shape, lambda bi, i: (0,) * a.ndim)
    tab = pl.BlockSpec((ts, LANES), lambda bi, i: (i, 0))
    out_shapes = (
        jax.ShapeDtypeStruct((b, s, N_HEADS * HEAD_PAD), BF16),
        jax.ShapeDtypeStruct((b, s, N_HEADS * HEAD_PAD), BF16),
        jax.ShapeDtypeStruct((b, s, N_HEADS * HEAD_PAD), BF16),
        jax.ShapeDtypeStruct((b, s, GMLP_W), BF16),
    )
    tok = lambda w: pl.BlockSpec((1, ts, w), lambda bi, i: (bi, i, 0))
    return pl.pallas_call(
        _pre_kernel,
        name="pre",
        grid=(b, s // ts),
        in_specs=[tok(d), pl.BlockSpec((1, N_MOD, d), lambda bi, i: (bi, 0, 0)), tab, tab, tab,
                  full(g1), full(win), full(gq), full(wuq), full(gkv), full(wuk), full(wuv), full(vone),
                  full(ggv), full(wsp), full(bsp), full(ggo)],
        out_specs=(tok(N_HEADS * HEAD_PAD), tok(N_HEADS * HEAD_PAD), tok(N_HEADS * HEAD_PAD), tok(GMLP_W)),
        out_shape=out_shapes,
        scratch_shapes=[pltpu.VMEM((ts, GMLP_W), F32)],
        compiler_params=pltpu.CompilerParams(dimension_semantics=("arbitrary", "arbitrary"),
                                             vmem_limit_bytes=VMEM_LIMIT),
    )(x, mod, cos, s1, s2, g1, win, gq, wuq, gkv, wuk, wuv, vone, ggv, wsp, bsp, ggo)


def _attn_kernel(q_ref, k_ref, v_ref, o_ref, *, tkc):
    tq = q_ref.shape[1]
    n_chunks = k_ref.shape[1] // tkc
    heads = [slice(hh * HEAD_PAD, (hh + 1) * HEAD_PAD) for hh in range(2)]
    qs = [q_ref[0, :, hs] for hs in heads]

    def step(c, carry):
        off = pl.multiple_of(c * tkc, tkc)
        out = []
        for hh, hs in enumerate(heads):
            m, acc = carry[2 * hh], carry[2 * hh + 1]
            kc = k_ref[0, pl.ds(off, tkc), hs]
            vc = v_ref[0, pl.ds(off, tkc), hs]
            sc = _dot_nt(qs[hh], kc)
            m_new = jnp.maximum(m, jnp.max(sc, axis=-1, keepdims=True))
            p = jnp.exp2(sc - m_new).astype(BF16)
            acc_new = jnp.exp2(m - m_new) * acc + _dot(p, vc)
            out += [m_new, acc_new]
        return tuple(out)

    init = (jnp.full((tq, 1), -jnp.inf, F32), jnp.zeros((tq, HEAD_PAD), F32)) * 2
    res = step(0, init) if n_chunks == 1 else lax.fori_loop(0, n_chunks, step, init)
    o0 = res[1] / res[1][:, V_DIM:V_DIM + 1]
    o1 = res[3] / res[3][:, V_DIM:V_DIM + 1]
    lane = lax.broadcasted_iota(jnp.int32, (tq, HEAD_PAD), 1)
    o_ref[0] = jnp.where(lane < V_DIM, o0, pltpu.roll(o1, V_DIM, axis=1)).astype(BF16)


def _attn_call(q, k, v, tq, tkc):
    b, s, _ = q.shape
    hp = N_HEADS // 2
    resident = lambda: pl.BlockSpec((1, s, 2 * HEAD_PAD), lambda bi, h, i: (bi, 0, h),
                                    pipeline_mode=pl.Buffered(1))
    return pl.pallas_call(
        functools.partial(_attn_kernel, tkc=tkc),
        name="attn",
        grid=(b, hp, s // tq),
        in_specs=[pl.BlockSpec((1, tq, 2 * HEAD_PAD), lambda bi, h, i: (bi, i, h)),
                  resident(), resident()],
        out_specs=pl.BlockSpec((1, tq, 2 * V_DIM), lambda bi, h, i: (bi, i, h)),
        out_shape=jax.ShapeDtypeStruct((b, s, N_HEADS * V_DIM), BF16),
        compiler_params=pltpu.CompilerParams(
            dimension_semantics=("arbitrary", "arbitrary", "arbitrary"),
            vmem_limit_bytes=VMEM_LIMIT),
    )(q, k, v)


def _post_kernel(attn_ref, gm_ref, x_ref, mod_ref, goa_ref, woa_ref, wog_ref, g2_ref, wr_ref,
                 br_ref, wgus_ref, wds_ref,
                 ybase_ref, h2p_ref, e_ref, rank_ref, w_ref, cnt_ref, run_ref):
    ts = x_ref.shape[1]
    first = jnp.logical_and(pl.program_id(0) == 0, pl.program_id(1) == 0)

    @pl.when(first)
    def _():
        run_ref[...] = jnp.zeros_like(run_ref)

    mod = mod_ref[0]
    ga1, sh2, sc2, ga2 = mod[2:3], mod[3:4], mod[4:5], mod[5:6]
    an = _rms(attn_ref[0].astype(F32), goa_ref[...]).astype(BF16)
    y = _dot(an, woa_ref[...]) + _dot(gm_ref[0], wog_ref[...])
    x1 = x_ref[0] + ga1 * y
    h2 = _rms(x1, g2_ref[...]) * (1.0 + sc2) + sh2
    h2b = h2.astype(BF16)
    h2p_ref[...] = _pack_bf16_pairs(h2)

    gu = _dot(h2b, wgus_ref[...])
    g, u = gu[:, :SHARED_DIM], gu[:, SHARED_DIM:]
    a = (g * _sigmoid(g) * u).astype(BF16)
    ybase_ref[0] = x1 + ga2 * _dot(a, wds_ref[...])

    logits = _dot_nt(wr_ref[...], h2b)
    scores = _sigmoid(logits)
    biased = scores + br_ref[...]
    ninf = jnp.float32(-jnp.inf)
    bj = [biased[j * N_GROUPS:(j + 1) * N_GROUPS] for j in range(GROUP_SIZE)]
    sj = [scores[j * N_GROUPS:(j + 1) * N_GROUPS] for j in range(GROUP_SIZE)]
    m1 = bj[0]
    for j in range(1, GROUP_SIZE):
        m1 = jnp.maximum(m1, bj[j])
    found = jnp.zeros_like(m1)
    m2 = jnp.full_like(m1, ninf)
    for j in range(GROUP_SIZE):
        eq = jnp.where(bj[j] == m1, 1.0, 0.0)
        is_first = eq * (1.0 - found)
        found = jnp.maximum(found, eq)
        m2 = jnp.maximum(m2, jnp.where(is_first > 0.0, ninf, bj[j]))
    gs = m1 + m2
    gidx = lax.broadcasted_iota(jnp.int32, gs.shape, 0)
    grank = jnp.zeros_like(gs)
    for kk in range(1, N_GROUPS):
        r = pltpu.roll(gs, kk, axis=0)
        grank = grank + jnp.where(gidx >= kk, jnp.where(r >= gs, 1.0, 0.0), jnp.where(r > gs, 1.0, 0.0))
    gsel = grank < float(TOPK_GROUPS)
    masked = [jnp.where(gsel, bj[j], ninf) for j in range(GROUP_SIZE)]
    eidx = [gidx * GROUP_SIZE + j for j in range(GROUP_SIZE)]

    selm = [jnp.zeros_like(gs) for _ in range(GROUP_SIZE)]
    e_sel = []
    for _k in range(TOP_K):
        m = masked[0]
        for j in range(1, GROUP_SIZE):
            m = jnp.maximum(m, masked[j])
        m = jnp.max(m, axis=0, keepdims=True)
        cand = jnp.where(masked[0] == m, eidx[0], N_EXPERTS)
        for j in range(1, GROUP_SIZE):
            cand = jnp.minimum(cand, jnp.where(masked[j] == m, eidx[j], N_EXPERTS))
        emin = jnp.min(cand, axis=0, keepdims=True)
        e_sel.append(emin)
        for j in range(GROUP_SIZE):
            hit = eidx[j] == emin
            selm[j] = jnp.where(hit, 1.0, selm[j])
            masked[j] = jnp.where(hit, ninf, masked[j])

    wsel = [selm[j] * sj[j] for j in range(GROUP_SIZE)]
    tot = wsel[0]
    for j in range(1, GROUP_SIZE):
        tot = tot + wsel[j]
    tot = jnp.sum(tot, axis=0, keepdims=True)
    wn = [wsel[j] / tot * ROUTED_SCALE for j in range(GROUP_SIZE)]

    sel = jnp.concatenate(selm, axis=0)
    tr = lax.broadcasted_iota(jnp.int32, (ts, ts), 0)
    tc = lax.broadcasted_iota(jnp.int32, (ts, ts), 1)
    upper = jnp.where(tr < tc, 1.0, 0.0).astype(BF16)
    run = run_ref[...]
    rank_full = _dot(sel.astype(BF16), upper) + run[:, 0:1]
    run_new = run + jnp.sum(sel, axis=1, keepdims=True)
    run_ref[...] = run_new
    cnt_ref[...] = run_new
    rj = [rank_full[j * N_GROUPS:(j + 1) * N_GROUPS] for j in range(GROUP_SIZE)]

    for k in range(TOP_K):
        rk = jnp.zeros_like(gs)
        wk = jnp.zeros_like(gs)
        for j in range(GROUP_SIZE):
            hit = eidx[j] == e_sel[k]
            rk = rk + jnp.where(hit, rj[j], 0.0)
            wk = wk + jnp.where(hit, wn[j], 0.0)
        e_ref[0, k:k + 1, :] = e_sel[k]
        rank_ref[0, k:k + 1, :] = jnp.sum(rk, axis=0, keepdims=True).astype(jnp.int32)
        w_ref[0, k:k + 1, :] = jnp.sum(wk, axis=0, keepdims=True)


def _post_call(attn, gm, x, mod, wts, ts):
    b, s, d = x.shape
    nt = s // ts
    (goa, woa, wog, g2, wr, br, wgus, wds) = wts
    full = lambda a: pl.BlockSpec(a.shape, lambda bi, i: (0,) * a.ndim)
    tok = lambda w: pl.BlockSpec((1, ts, w), lambda bi, i: (bi, i, 0))
    rout = pl.BlockSpec((1, TOP_K, ts), lambda bi, i: (bi * nt + i, 0, 0))
    out_shapes = (
        jax.ShapeDtypeStruct((b, s, d), F32),
        jax.ShapeDtypeStruct((b * s, d // 2), jnp.uint32),
        jax.ShapeDtypeStruct((b * nt, TOP_K, ts), jnp.int32),
        jax.ShapeDtypeStruct((b * nt, TOP_K, ts), jnp.int32),
        jax.ShapeDtypeStruct((b * nt, TOP_K, ts), F32),
        jax.ShapeDtypeStruct((N_EXPERTS, LANES), F32),
    )
    return pl.pallas_call(
        _post_kernel,
        name="post",
        grid=(b, nt),
        in_specs=[tok(N_HEADS * V_DIM), tok(GMLP_W), tok(d),
                  pl.BlockSpec((1, N_MOD, d), lambda bi, i: (bi, 0, 0)),
                  full(goa), full(woa), full(wog), full(g2), full(wr), full(br), full(wgus), full(wds)],
        out_specs=(tok(d), pl.BlockSpec((ts, d // 2), lambda bi, i: (bi * nt + i, 0)),
                   rout, rout, rout, pl.BlockSpec((N_EXPERTS, LANES), lambda bi, i: (0, 0))),
        out_shape=out_shapes,
        scratch_shapes=[pltpu.VMEM((N_EXPERTS, LANES), F32)],
        compiler_params=pltpu.CompilerParams(dimension_semantics=("arbitrary", "arbitrary"),
                                             vmem_limit_bytes=VMEM_LIMIT),
    )(attn, gm, x, mod, goa, woa, wog, g2, wr, br, wgus, wds)


def _disp_kernel(pends_ref, padded_ref, dest_ref, h_ref, xs_ref, zbuf, sem, zsem):
    td = h_ref.shape[0]
    blk = zbuf.shape[0]

    @pl.when(pl.program_id(0) == 0)
    def _():
        zbuf[...] = jnp.zeros_like(zbuf)

        def tail_copy(e):
            start = pl.multiple_of(pends_ref[e] - blk, blk)
            return pltpu.make_async_copy(zbuf, xs_ref.at[pl.ds(start, blk), :], zsem)

        def zstart(e, c):
            @pl.when(padded_ref[e] > 0)
            def _():
                tail_copy(e).start()
            return c

        def zwait(e, c):
            @pl.when(padded_ref[e] > 0)
            def _():
                tail_copy(e).wait()
            return c

        lax.fori_loop(0, N_EXPERTS, zstart, 0)
        lax.fori_loop(0, N_EXPERTS, zwait, 0)

    def row_copy(t, d):
        return pltpu.make_async_copy(h_ref.at[pl.ds(t, 1), :], xs_ref.at[pl.ds(d, 1), :], sem)

    def issue(t, c):
        for k in range(TOP_K):
            row_copy(t, dest_ref[0, k, t]).start(priority=k % 2)
        return c

    lax.fori_loop(0, td, issue, 0)

    def drain(t, c):
        for k in range(TOP_K):
            row_copy(t, dest_ref[0, k, t]).wait()
        return c

    lax.fori_loop(0, td, drain, 0)


def _disp_call(pends, padded, dest, h2p, n_rows, td, blk):
    t, w = h2p.shape
    nt = t // td
    return pl.pallas_call(
        _disp_kernel,
        name="disp",
        grid_spec=pltpu.PrefetchScalarGridSpec(
            num_scalar_prefetch=2,
            grid=(nt,),
            in_specs=[pl.BlockSpec((1, TOP_K, td), lambda i, pe, pa: (i, 0, 0), memory_space=pltpu.SMEM),
                      pl.BlockSpec((td, w), lambda i, pe, pa: (i, 0))],
            out_specs=pl.BlockSpec(memory_space=pl.ANY),
            scratch_shapes=[pltpu.VMEM((blk, w), jnp.uint32), pltpu.SemaphoreType.DMA(()),
                            pltpu.SemaphoreType.DMA(())],
        ),
        out_shape=jax.ShapeDtypeStruct((n_rows, w), jnp.uint32),
        compiler_params=pltpu.CompilerParams(dimension_semantics=("arbitrary",)),
    )(pends, padded, dest, h2p)


def _exp_kernel(bexp_ref, nused_ref, xs_ref, wgu_ref, wd_ref, ys_ref):
    i = pl.program_id(0)

    @pl.when(i < nused_ref[0])
    def _():
        half = D_MODEL // 2
        lo, hi = _unpack_bf16_pairs(xs_ref[...])
        gu = (_dot(lo.astype(BF16), wgu_ref[0, :half, :])
              + _dot(hi.astype(BF16), wgu_ref[0, half:, :]))
        g, u = gu[:, :EXPERT_DIM], gu[:, EXPERT_DIM:]
        a = (g * _sigmoid(g) * u).astype(BF16)
        ys_ref[...] = _pack_bf16_pairs(_dot(a, wd_ref[0]))


def _exp_call(block_exp, n_used, xs, wgu, wd, blk):
    n_rows, w = xs.shape
    n_blocks = n_rows // blk

    def row_map(i, bexp, nused):
        return (jnp.minimum(i, nused[0] - 1), 0)

    def w_map(i, bexp, nused):
        return (bexp[jnp.minimum(i, nused[0] - 1)], 0, 0)

    return pl.pallas_call(
        _exp_kernel,
        name="exp",
        grid_spec=pltpu.PrefetchScalarGridSpec(
            num_scalar_prefetch=2,
            grid=(n_blocks,),
            in_specs=[pl.BlockSpec((blk, w), row_map),
                      pl.BlockSpec((1, D_MODEL, 2 * EXPERT_DIM), w_map),
                      pl.BlockSpec((1, EXPERT_DIM, D_MODEL), w_map)],
            out_specs=pl.BlockSpec((blk, w), row_map),
        ),
        out_shape=jax.ShapeDtypeStruct((n_rows, w), jnp.uint32),
        compiler_params=pltpu.CompilerParams(dimension_semantics=("arbitrary",),
                                             vmem_limit_bytes=VMEM_LIMIT),
    )(block_exp, n_used, xs, wgu, wd)


def _comb_kernel(dest_ref, ybase_ref, mod_ref, w_ref, gf_ref, ys_ref, o_ref, buf, sem):
    tc = ybase_ref.shape[1]

    def row_copy(t, k, d):
        return pltpu.make_async_copy(ys_ref.at[pl.ds(d, 1), :], buf.at[k, pl.ds(t, 1), :], sem)

    def issue(t, c):
        for k in range(TOP_K):
            row_copy(t, k, dest_ref[0, k, t]).start(priority=k % 2)
        return c

    lax.fori_loop(0, tc, issue, 0)

    def drain(t, c):
        for k in range(TOP_K):
            row_copy(t, k, dest_ref[0, k, t]).wait()
        return c

    lax.fori_loop(0, tc, drain, 0)

    w = w_ref[...]
    half = D_MODEL // 2
    acc_lo = jnp.zeros((tc, half), F32)
    acc_hi = jnp.zeros((tc, half), F32)
    for k in range(TOP_K):
        lo, hi = _unpack_bf16_pairs(buf[k])
        wk = w[:, k:k + 1]
        acc_lo = acc_lo + wk * lo
        acc_hi = acc_hi + wk * hi
    routed = jnp.concatenate([acc_lo, acc_hi], axis=1)
    ga2 = mod_ref[0][5:6]
    o_ref[0] = _rms(ybase_ref[0] + ga2 * routed, gf_ref[...])


def _comb_call(dest, ybase, mod, w_tok, g_final, ys, tc):
    b, s, d = ybase.shape
    nt = s // tc
    return pl.pallas_call(
        _comb_kernel,
        name="comb",
        grid=(b, nt),
        in_specs=[pl.BlockSpec((1, TOP_K, tc), lambda bi, i: (bi * nt + i, 0, 0), memory_space=pltpu.SMEM),
                  pl.BlockSpec((1, tc, d), lambda bi, i: (bi, i, 0)),
                  pl.BlockSpec((1, N_MOD, d), lambda bi, i: (bi, 0, 0)),
                  pl.BlockSpec((tc, TOP_K), lambda bi, i: (bi * nt + i, 0)),
                  pl.BlockSpec((1, d), lambda bi, i: (0, 0)),
                  pl.BlockSpec(memory_space=pl.ANY)],
        out_specs=pl.BlockSpec((1, tc, d), lambda bi, i: (bi, i, 0)),
        out_shape=jax.ShapeDtypeStruct((b, s, d), F32),
        scratch_shapes=[pltpu.VMEM((TOP_K, tc, d // 2), jnp.uint32), pltpu.SemaphoreType.DMA(())],
        compiler_params=pltpu.CompilerParams(dimension_semantics=("arbitrary", "arbitrary"),
                                             vmem_limit_bytes=VMEM_LIMIT),
    )(dest, ybase, mod, w_tok, g_final, ys)


def _prep_weights(w_ada, b_ada, g_norm1, w_in, g_q_lat, w_uq, g_kv_lat, w_ukv, g_gmlp_v, w_spatial,
                  b_spatial, g_out_attn, g_out_gmlp, w_out, g_norm2, w_router, b_router, w_gate_e,
                  w_up_e, w_down_e, w_gate_s, w_up_s, w_down_s, g_final):
    row = lambda g: g.reshape(1, -1).astype(F32)
    o1 = Q_LORA
    o2 = o1 + KV_LORA
    o3 = o2 + QK_ROPE
    o4 = o3 + GMLP_W
    kr_cols = jnp.pad(w_in[:, o2:o3], ((0, 0), (QK_NOPE, LANES - QK_NOPE - QK_ROPE)))
    win = jnp.concatenate([w_in[:, :o2], w_in[:, o3:o4], w_in[:, o4:], kr_cols], axis=1).astype(BF16)
    qd = QK_NOPE + QK_ROPE
    wuq = jnp.pad(w_uq.reshape(Q_LORA, N_HEADS, qd), ((0, 0), (0, 0), (0, HEAD_PAD - qd)))
    wuq = wuq.reshape(Q_LORA, N_HEADS * HEAD_PAD).astype(BF16)
    wkv = w_ukv.reshape(KV_LORA, N_HEADS, QK_NOPE + V_DIM)
    wuk = jnp.pad(wkv[:, :, :QK_NOPE], ((0, 0), (0, 0), (0, HEAD_PAD - QK_NOPE)))
    wuk = wuk.reshape(KV_LORA, N_HEADS * HEAD_PAD).astype(BF16)
    wuv = jnp.pad(wkv[:, :, QK_NOPE:], ((0, 0), (0, 0), (0, HEAD_PAD - V_DIM)))
    wuv = wuv.reshape(KV_LORA, N_HEADS * HEAD_PAD).astype(BF16)
    vone = jnp.tile((jnp.arange(HEAD_PAD) == V_DIM).astype(F32), N_HEADS).reshape(1, -1)
    wsp = w_spatial.reshape(N_HEADS // 2, 2, CHUNK, CHUNK).transpose(0, 2, 1, 3)
    wsp = wsp.reshape(N_HEADS // 2, CHUNK, 2 * CHUNK).astype(BF16)
    bsp = jnp.repeat(jnp.transpose(b_spatial), GMLP_W // N_HEADS, axis=1).astype(F32)
    pre = (row(g_norm1), win, row(g_q_lat), wuq, row(g_kv_lat), wuk, wuv, vone, row(g_gmlp_v), wsp,
           bsp, row(g_out_gmlp))
    perm = (jnp.arange(N_GROUPS)[None, :] * GROUP_SIZE + jnp.arange(GROUP_SIZE)[:, None]).reshape(-1)
    wr = jnp.transpose(w_router)[perm].astype(BF16)
    br = b_router.astype(F32)[perm].reshape(N_EXPERTS, 1)
    mla_w = N_HEADS * V_DIM
    wgus = jnp.concatenate([w_gate_s, w_up_s], axis=1).astype(BF16)
    post = (row(g_out_attn), w_out[:mla_w].astype(BF16), w_out[mla_w:].astype(BF16), row(g_norm2),
            wr, br, wgus, w_down_s.astype(BF16))
    wgu_e = jnp.concatenate([w_gate_e, w_up_e], axis=2).astype(BF16)
    wd_e = w_down_e.astype(BF16)
    return w_ada.astype(BF16), b_ada, pre, post, (wgu_e, wd_e), row(g_final)


def _rope_tables(s):
    inv = 1.0 / (ROPE_THETA ** (jnp.arange(0, QK_ROPE, 2, dtype=F32) / QK_ROPE))
    ang = jnp.arange(s, dtype=F32)[:, None] * inv[None, :]
    cos, sin = jnp.cos(ang), jnp.sin(ang)
    z = lambda n: jnp.zeros((s, n), F32)
    tail = LANES - QK_NOPE - QK_ROPE
    c = jnp.concatenate([jnp.ones((s, QK_NOPE), F32), cos, cos, z(tail)], axis=1)
    s1 = jnp.concatenate([z(QK_NOPE), -sin, z(HALF_ROPE), z(tail)], axis=1)
    s2 = jnp.concatenate([z(QK_NOPE), z(HALF_ROPE), sin, z(tail)], axis=1)
    return c, s1, s2


def _tiles(s):
    ts = min(512, s)
    tq = min(512, s)
    tkc = min(1024, s)
    blk = 512
    return ts, tq, tkc, blk


def _trunk(x, c, prep, tiles=None):
    w_ada, b_ada, pre_w, post_w, exp_w, g_final = prep
    b, s, d = x.shape
    ts, tq, tkc, blk = tiles or _tiles(s)
    t = b * s
    nt = s // ts
    mod = _modulation(c, w_ada, b_ada)
    q, k, v, gm = _pre_call(x, mod, _rope_tables(s), pre_w, ts)
    attn = _attn_call(q, k, v, tq, tkc)
    ybase, h2p, e_arr, rank_arr, w_arr, cnt = _post_call(attn, gm, x, mod, post_w, ts)

    counts = cnt[:, 0].astype(jnp.int32).reshape(GROUP_SIZE, N_GROUPS).T.reshape(N_EXPERTS)
    padded = (counts + blk - 1) // blk * blk
    pends = jnp.cumsum(padded)
    pstart = pends - padded
    n_blocks = -(-(t * TOP_K) // blk) + N_EXPERTS
    n_rows = n_blocks * blk
    eids = jnp.arange(N_EXPERTS, dtype=jnp.int32)
    dest = rank_arr + jnp.sum(jnp.where(e_arr[..., None] == eids, pstart, 0), axis=-1)
    n_used = (pends[-1] // blk).astype(jnp.int32).reshape(1)
    block_start = jnp.arange(n_blocks, dtype=jnp.int32) * blk
    block_exp = jnp.minimum(jnp.sum((pends[None, :] <= block_start[:, None]).astype(jnp.int32), axis=1),
                            N_EXPERTS - 1)
    w_tok = jnp.transpose(w_arr, (0, 2, 1)).reshape(t, TOP_K)

    xs = _disp_call(pends.astype(jnp.int32), padded, dest, h2p, n_rows, ts, blk)
    ys = _exp_call(block_exp, n_used, xs, exp_w[0], exp_w[1], blk)
    return _comb_call(dest, ybase, mod, w_tok, g_final, ys, ts)


def kernel(x_prompt, x_sample, c_prompt, c_sample, w_ada, b_ada, g_norm1, w_in, g_q_lat, w_uq, g_kv_lat, w_ukv, g_gmlp_v, w_spatial, b_spatial, g_out_attn, g_out_gmlp, w_out, g_norm2, w_router, b_router, w_gate_e, w_up_e, w_down_e, w_gate_s, w_up_s, w_down_s, g_final):
    prep = _prep_weights(w_ada[0], b_ada[0], g_norm1[0], w_in[0], g_q_lat[0], w_uq[0], g_kv_lat[0],
                         w_ukv[0], g_gmlp_v[0], w_spatial[0], b_spatial[0], g_out_attn[0],
                         g_out_gmlp[0], w_out[0], g_norm2[0], w_router[0], b_router[0], w_gate_e[0],
                         w_up_e[0], w_down_e[0], w_gate_s[0], w_up_s[0], w_down_s[0], g_final)
    return (_trunk(x_prompt, c_prompt, prep), _trunk(x_sample, c_sample, prep))
```

```python
import functools
import math

import jax
import jax.numpy as jnp
from jax import lax
from jax.experimental import pallas as pl
from jax.experimental.pallas import tpu as pltpu

F32 = jnp.float32
BF16 = jnp.bfloat16

D_MODEL = 1024
N_HEADS = 8
QK_NOPE = 64
QK_ROPE = 32
V_DIM = 64
Q_LORA = 256
KV_LORA = 128
GMLP_W = 512
CHUNK = 128
N_EXPERTS = 64
TOP_K = 8
N_GROUPS = 8
TOPK_GROUPS = 4
GROUP_SIZE = N_EXPERTS // N_GROUPS
EXPERT_DIM = 256
SHARED_DIM = 256
ROUTED_SCALE = 2.5
ROPE_THETA = 10000.0
N_MOD = 6
EPS = 1e-6

LANES = 128
HEAD_PAD = 128
HALF_ROPE = QK_ROPE // 2
VMEM_LIMIT = 52 * 1024 * 1024
ROW_ALIGN = 8
DISP_CHUNK = 32
COMB_CHUNK = 16
PERM_ROWS = 256
COMB_KTILE = 256
EXPERT_SLACK = max(DISP_CHUNK, 2 * (COMB_CHUNK - 1))

SOFTMAX_SCALE = (QK_NOPE + QK_ROPE) ** -0.5
EXP2_SCALE = SOFTMAX_SCALE * math.log2(math.e)


def _rms(x, g):
    return x * lax.rsqrt(jnp.mean(x * x, axis=-1, keepdims=True) + EPS) * g


def _sigmoid(x):
    return 1.0 / (1.0 + jnp.exp(-x))


def _gelu_tanh(x):
    c = math.sqrt(2.0 / math.pi)
    return 0.5 * x * (1.0 + jnp.tanh(c * (x + 0.044715 * (x * x * x))))


def _dot(a, b):
    return jnp.dot(a, b, preferred_element_type=F32)


def _dot_nt(a, b):
    return lax.dot_general(a, b, (((1,), (1,)), ((), ())), preferred_element_type=F32)


def _pack_bf16_pairs(x):
    c = x.shape[1] // 2
    lo = pltpu.bitcast(x[:, :c].astype(BF16).astype(F32), jnp.uint32)
    hi = pltpu.bitcast(x[:, c:].astype(BF16).astype(F32), jnp.uint32)
    return (hi & jnp.uint32(0xFFFF0000)) | (lo >> 16)


def _unpack_bf16_pairs(w):
    lo = pltpu.bitcast(w << 16, F32)
    hi = pltpu.bitcast(w & jnp.uint32(0xFFFF0000), F32)
    return lo, hi


def _mod_kernel(c_ref, w_ref, b_ref, o_ref):
    c = c_ref[...]
    a = (c * _sigmoid(c)).astype(BF16)
    o_ref[...] = _dot(a, w_ref[...]) + b_ref[...]


def _modulation(c, w_ada_bf, b_ada):
    b = c.shape[0]
    bp = max(16, -(-b // 16) * 16)
    cp = jnp.pad(c, ((0, bp - b), (0, 0)))
    n = w_ada_bf.shape[1]
    tn = D_MODEL
    out = pl.pallas_call(
        _mod_kernel,
        name="mod",
        grid=(n // tn,),
        in_specs=[
            pl.BlockSpec((bp, D_MODEL), lambda j: (0, 0)),
            pl.BlockSpec((D_MODEL, tn), lambda j: (0, j)),
            pl.BlockSpec((1, tn), lambda j: (0, j)),
        ],
        out_specs=pl.BlockSpec((bp, tn), lambda j: (0, j)),
        out_shape=jax.ShapeDtypeStruct((bp, n), F32),
        compiler_params=pltpu.CompilerParams(dimension_semantics=("arbitrary",)),
    )(cp, w_ada_bf, b_ada.reshape(1, n))
    return out[:b].reshape(b, N_MOD, D_MODEL)


def _rope(xh, c, s1, s2):
    return (xh * c + pltpu.roll(xh, LANES - HALF_ROPE, axis=1) * s1
            + pltpu.roll(xh, HALF_ROPE, axis=1) * s2)


def _pre_kernel(x_ref, mod_ref, cos_ref, s1_ref, s2_ref, g1_ref, win_ref, gq_ref, wuq_ref,
                gkv_ref, wuk_ref, wuv_ref, vone_ref, ggv_ref, ws_ref, bs_ref, ggo_ref,
                q_ref, k_ref, v_ref, gm_ref, mix_ref):
    ts = x_ref.shape[1]
    x = x_ref[0]
    mod = mod_ref[0]
    h = _rms(x, g1_ref[...]) * (1.0 + mod[1:2]) + mod[0:1]
    z = _dot(h.astype(BF16), win_ref[...])
    o_kv = Q_LORA
    o_gu = o_kv + KV_LORA
    o_gv = o_gu + GMLP_W
    o_kr = o_gv + GMLP_W
    q_lat = z[:, :o_kv]
    kv_lat = z[:, o_kv:o_gu]
    g_u = z[:, o_gu:o_gv]
    g_v = z[:, o_gv:o_kr]
    kr = z[:, o_kr:o_kr + LANES]

    cos = cos_ref[...]
    s1 = s1_ref[...]
    s2 = s2_ref[...]

    qn = _rms(q_lat, gq_ref[...]).astype(BF16)
    q = _dot(qn, wuq_ref[...])
    kn = _rms(kv_lat, gkv_ref[...]).astype(BF16)
    kf = _dot(kn, wuk_ref[...])
    v_ref[0] = (_dot(kn, wuv_ref[...]) + vone_ref[...]).astype(BF16)
    krr = _rope(kr, cos, s1, s2)
    for hd in range(N_HEADS):
        sl = slice(hd * HEAD_PAD, (hd + 1) * HEAD_PAD)
        q_ref[0, :, sl] = (_rope(q[:, sl], cos, s1, s2) * EXP2_SCALE).astype(BF16)
        k_ref[0, :, sl] = (kf[:, sl] + krr).astype(BF16)

    u = _gelu_tanh(g_u)
    vn = _rms(_gelu_tanh(g_v), ggv_ref[...]).astype(BF16)
    lane = lax.broadcasted_iota(jnp.int32, (CHUNK, LANES), 1)
    left = lane < (LANES // 2)
    zero = jnp.zeros((CHUNK, LANES), BF16)
    for n in range(ts // CHUNK):
        rs = slice(n * CHUNK, (n + 1) * CHUNK)
        for p in range(GMLP_W // LANES):
            cs = slice(p * LANES, (p + 1) * LANES)
            vp = vn[rs, cs]
            rhs = jnp.concatenate([jnp.where(left, vp, zero), jnp.where(left, zero, vp)], axis=0)
            mix_ref[rs, cs] = _dot(ws_ref[p], rhs) + bs_ref[:, cs]
    gm = u * mix_ref[...]
    gm_ref[0] = _rms(gm, ggo_ref[...]).astype(BF16)


def _pre_call(x, mod, tabs, wts, ts):
    b, s, d = x.shape
    cos, s1, s2 = tabs
    (g1, win, gq, wuq, gkv, wuk, wuv, vone, ggv, wsp, bsp, ggo) = wts
    full = lambda a: pl.BlockSpec(a.shape, lambda bi, i: (0,) * a.ndim)
    tab = pl.BlockSpec((ts, LANES), lambda bi, i: (i, 0))
    out_shapes = (
        jax.ShapeDtypeStruct((b, s, N_HEADS * HEAD_PAD), BF16),
        jax.ShapeDtypeStruct((b, s, N_HEADS * HEAD_PAD), BF16),
        jax.ShapeDtypeStruct((b, s, N_HEADS * HEAD_PAD), BF16),
        jax.ShapeDtypeStruct((b, s, GMLP_W), BF16),
    )
    tok = lambda w: pl.BlockSpec((1, ts, w), lambda bi, i: (bi, i, 0))
    return pl.pallas_call(
        _pre_kernel,
        name="pre",
        grid=(b, s // ts),
        in_specs=[tok(d), pl.BlockSpec((1, N_MOD, d), lambda bi, i: (bi, 0, 0)), tab, tab, tab,
                  full(g1), full(win), full(gq), full(wuq), full(gkv), full(wuk), full(wuv), full(vone),
                  full(ggv), full(wsp), full(bsp), full(ggo)],
        out_specs=(tok(N_HEADS * HEAD_PAD), tok(N_HEADS * HEAD_PAD), tok(N_HEADS * HEAD_PAD), tok(GMLP_W)),
        out_shape=out_shapes,
        scratch_shapes=[pltpu.VMEM((ts, GMLP_W), F32)],
        compiler_params=pltpu.CompilerParams(dimension_semantics=("arbitrary", "arbitrary"),
                                             vmem_limit_bytes=VMEM_LIMIT),
    )(x, mod, cos, s1, s2, g1, win, gq, wuq, gkv, wuk, wuv, vone, ggv, wsp, bsp, ggo)


def _attn_kernel(q_ref, k_ref, v_ref, o_ref, *, tkc):
    tq = q_ref.shape[1]
    n_chunks = k_ref.shape[1] // tkc
    heads = [slice(hh * HEAD_PAD, (hh + 1) * HEAD_PAD) for hh in range(2)]
    qs = [q_ref[0, :, hs] for hs in heads]

    def step(c, carry):
        off = pl.multiple_of(c * tkc, tkc)
        out = []
        for hh, hs in enumerate(heads):
            m, acc = carry[2 * hh], carry[2 * hh + 1]
            kc = k_ref[0, pl.ds(off, tkc), hs]
            vc = v_ref[0, pl.ds(off, tkc), hs]
            sc = _dot_nt(qs[hh], kc)
            m_new = jnp.maximum(m, jnp.max(sc, axis=-1, keepdims=True))
            p = jnp.exp2(sc - m_new).astype(BF16)
            acc_new = jnp.exp2(m - m_new) * acc + _dot(p, vc)
            out += [m_new, acc_new]
        return tuple(out)

    init = (jnp.full((tq, 1), -jnp.inf, F32), jnp.zeros((tq, HEAD_PAD), F32)) * 2
    res = step(0, init) if n_chunks == 1 else lax.fori_loop(0, n_chunks, step, init)
    o0 = res[1] / res[1][:, V_DIM:V_DIM + 1]
    o1 = res[3] / res[3][:, V_DIM:V_DIM + 1]
    lane = lax.broadcasted_iota(jnp.int32, (tq, HEAD_PAD), 1)
    o_ref[0] = jnp.where(lane < V_DIM, o0, pltpu.roll(o1, V_DIM, axis=1)).astype(BF16)


def _attn_call(q, k, v, tq, tkc):
    b, s, _ = q.shape
    hp = N_HEADS // 2
    resident = lambda: pl.BlockSpec((1, s, 2 * HEAD_PAD), lambda bi, h, i: (bi, 0, h),
                                    pipeline_mode=pl.Buffered(1))
    return pl.pallas_call(
        functools.partial(_attn_kernel, tkc=tkc),
        name="attn",
        grid=(b, hp, s // tq),
        in_specs=[pl.BlockSpec((1, tq, 2 * HEAD_PAD), lambda bi, h, i: (bi, i, h)),
                  resident(), resident()],
        out_specs=pl.BlockSpec((1, tq, 2 * V_DIM), lambda bi, h, i: (bi, i, h)),
        out_shape=jax.ShapeDtypeStruct((b, s, N_HEADS * V_DIM), BF16),
        compiler_params=pltpu.CompilerParams(
            dimension_semantics=("arbitrary", "arbitrary", "arbitrary"),
            vmem_limit_bytes=VMEM_LIMIT),
    )(q, k, v)


def _post_kernel(attn_ref, gm_ref, x_ref, mod_ref, goa_ref, woa_ref, wog_ref, g2_ref, wr_ref,
                 br_ref, wgus_ref, wds_ref,
                 ybase_ref, h2p_ref, e_ref, rank_ref, w_ref, cnt_ref):
    ts = x_ref.shape[1]
    mod = mod_ref[0]
    ga1, sh2, sc2, ga2 = mod[2:3], mod[3:4], mod[4:5], mod[5:6]
    an = _rms(attn_ref[0].astype(F32), goa_ref[...]).astype(BF16)
    y = _dot(an, woa_ref[...]) + _dot(gm_ref[0], wog_ref[...])
    x1 = x_ref[0] + ga1 * y
    h2 = _rms(x1, g2_ref[...]) * (1.0 + sc2) + sh2
    h2b = h2.astype(BF16)
    h2p_ref[...] = _pack_bf16_pairs(h2)

    gu = _dot(h2b, wgus_ref[...])
    g, u = gu[:, :SHARED_DIM], gu[:, SHARED_DIM:]
    a = (g * _sigmoid(g) * u).astype(BF16)
    ybase_ref[0] = x1 + ga2 * _dot(a, wds_ref[...])

    logits = _dot_nt(wr_ref[...], h2b)
    scores = _sigmoid(logits)
    biased = scores + br_ref[...]
    ninf = jnp.float32(-jnp.inf)
    bj = [biased[j * N_GROUPS:(j + 1) * N_GROUPS] for j in range(GROUP_SIZE)]
    sj = [scores[j * N_GROUPS:(j + 1) * N_GROUPS] for j in range(GROUP_SIZE)]
    m1 = bj[0]
    for j in range(1, GROUP_SIZE):
        m1 = jnp.maximum(m1, bj[j])
    found = jnp.zeros_like(m1)
    m2 = jnp.full_like(m1, ninf)
    for j in range(GROUP_SIZE):
        eq = jnp.where(bj[j] == m1, 1.0, 0.0)
        is_first = eq * (1.0 - found)
        found = jnp.maximum(found, eq)
        m2 = jnp.maximum(m2, jnp.where(is_first > 0.0, ninf, bj[j]))
    gs = m1 + m2
    gidx = lax.broadcasted_iota(jnp.int32, gs.shape, 0)
    grank = jnp.zeros_like(gs)
    for kk in range(1, N_GROUPS):
        r = pltpu.roll(gs, kk, axis=0)
        grank = grank + jnp.where(gidx >= kk, jnp.where(r >= gs, 1.0, 0.0), jnp.where(r > gs, 1.0, 0.0))
    gsel = grank < float(TOPK_GROUPS)
    masked = [jnp.where(gsel, bj[j], ninf) for j in range(GROUP_SIZE)]
    eidx = [gidx * GROUP_SIZE + j for j in range(GROUP_SIZE)]

    selm = [jnp.zeros_like(gs) for _ in range(GROUP_SIZE)]
    e_sel = []
    for _k in range(TOP_K):
        m = masked[0]
        for j in range(1, GROUP_SIZE):
            m = jnp.maximum(m, masked[j])
        m = jnp.max(m, axis=0, keepdims=True)
        cand = jnp.where(masked[0] == m, eidx[0], N_EXPERTS)
        for j in range(1, GROUP_SIZE):
            cand = jnp.minimum(cand, jnp.where(masked[j] == m, eidx[j], N_EXPERTS))
        emin = jnp.min(cand, axis=0, keepdims=True)
        e_sel.append(emin)
        for j in range(GROUP_SIZE):
            hit = eidx[j] == emin
            selm[j] = jnp.where(hit, 1.0, selm[j])
            masked[j] = jnp.where(hit, ninf, masked[j])

    wsel = [selm[j] * sj[j] for j in range(GROUP_SIZE)]
    tot = wsel[0]
    for j in range(1, GROUP_SIZE):
        tot = tot + wsel[j]
    tot = jnp.sum(tot, axis=0, keepdims=True)
    wn = [wsel[j] / tot * ROUTED_SCALE for j in range(GROUP_SIZE)]

    sel = jnp.concatenate(selm, axis=0)
    tr = lax.broadcasted_iota(jnp.int32, (ts, ts), 0)
    tc = lax.broadcasted_iota(jnp.int32, (ts, ts), 1)
    upper = jnp.where(tr < tc, 1.0, 0.0).astype(BF16)
    rank_full = _dot(sel.astype(BF16), upper)
    cnt_ref[0] = jnp.broadcast_to(jnp.sum(sel, axis=1, keepdims=True), (N_EXPERTS, LANES))
    rj = [rank_full[j * N_GROUPS:(j + 1) * N_GROUPS] for j in range(GROUP_SIZE)]

    for k in range(TOP_K):
        rk = jnp.zeros_like(gs)
        wk = jnp.zeros_like(gs)
        for j in range(GROUP_SIZE):
            hit = eidx[j] == e_sel[k]
            rk = rk + jnp.where(hit, rj[j], 0.0)
            wk = wk + jnp.where(hit, wn[j], 0.0)
        e_ref[0, k:k + 1, :] = e_sel[k]
        rank_ref[0, k:k + 1, :] = jnp.sum(rk, axis=0, keepdims=True).astype(jnp.int32)
        w_ref[0, k:k + 1, :] = jnp.sum(wk, axis=0, keepdims=True)


def _post_call(attn, gm, x, mod, wts, ts):
    b, s, d = x.shape
    nt = s // ts
    (goa, woa, wog, g2, wr, br, wgus, wds) = wts
    full = lambda a: pl.BlockSpec(a.shape, lambda bi, i: (0,) * a.ndim)
    tok = lambda w: pl.BlockSpec((1, ts, w), lambda bi, i: (bi, i, 0))
    rout = pl.BlockSpec((1, TOP_K, ts), lambda bi, i: (bi * nt + i, 0, 0))
    out_shapes = (
        jax.ShapeDtypeStruct((b, s, d), F32),
        jax.ShapeDtypeStruct((b * s, d // 2), jnp.uint32),
        jax.ShapeDtypeStruct((b * nt, TOP_K, ts), jnp.int32),
        jax.ShapeDtypeStruct((b * nt, TOP_K, ts), jnp.int32),
        jax.ShapeDtypeStruct((b * nt, TOP_K, ts), F32),
        jax.ShapeDtypeStruct((b * nt, N_EXPERTS, LANES), F32),
    )
    return pl.pallas_call(
        _post_kernel,
        name="post",
        grid=(b, nt),
        in_specs=[tok(N_HEADS * V_DIM), tok(GMLP_W), tok(d),
                  pl.BlockSpec((1, N_MOD, d), lambda bi, i: (bi, 0, 0)),
                  full(goa), full(woa), full(wog), full(g2), full(wr), full(br), full(wgus), full(wds)],
        out_specs=(tok(d), pl.BlockSpec((ts, d // 2), lambda bi, i: (bi * nt + i, 0)),
                   rout, rout, rout,
                   pl.BlockSpec((1, N_EXPERTS, LANES), lambda bi, i: (bi * nt + i, 0, 0))),
        out_shape=out_shapes,
        compiler_params=pltpu.CompilerParams(dimension_semantics=("arbitrary", "arbitrary"),
                                             vmem_limit_bytes=VMEM_LIMIT),
    )(attn, gm, x, mod, goa, woa, wog, g2, wr, br, wgus, wds)


def _disp_kernel(pends_ref, zfrom_ref, info_ref, lpos_ref, h_ref, xs_ref, sbuf, zbuf, sem, zsem):
    td = h_ref.shape[0]
    blk = zbuf.shape[0]

    @pl.when(pl.program_id(0) == 0)
    def _():
        zbuf[...] = jnp.zeros_like(zbuf)

        def tail_copy(e, j):
            start = pl.multiple_of(zfrom_ref[e] + j * blk, blk)
            return pltpu.make_async_copy(zbuf, xs_ref.at[pl.ds(start, blk), :], zsem)

        def n_tail(e):
            return (pends_ref[e] - zfrom_ref[e]) // blk

        def zstart(e, c):
            def one(j, c2):
                tail_copy(e, j).start()
                return c2
            return lax.fori_loop(0, n_tail(e), one, c)

        def zwait(e, c):
            def one(j, c2):
                tail_copy(e, j).wait()
                return c2
            return lax.fori_loop(0, n_tail(e), one, c)

        lax.fori_loop(0, N_EXPERTS, zstart, 0)
        lax.fori_loop(0, N_EXPERTS, zwait, 0)

    lo, hi = _unpack_bf16_pairs(h_ref[...])
    lo = lo.astype(BF16)
    hi = hi.astype(BF16)
    lpos = lpos_ref[0]
    lpos_b = [jnp.broadcast_to(lpos[k:k + 1, :], (PERM_ROWS, td)).astype(jnp.int16) for k in range(TOP_K)]
    row0 = lax.broadcasted_iota(jnp.int32, (PERM_ROWS, td), 0)
    one = jnp.ones((PERM_ROWS, td), BF16)

    def build(rb, c):
        r0 = pl.multiple_of(rb * PERM_ROWS, PERM_ROWS)
        riota = (row0 + r0).astype(jnp.int16)
        pb = jnp.zeros((PERM_ROWS, td), BF16)
        for k in range(TOP_K):
            pb = jnp.where(lpos_b[k] == riota, one, pb)
        xlo = pltpu.bitcast(_dot(pb, lo), jnp.uint32)
        xhi = pltpu.bitcast(_dot(pb, hi), jnp.uint32)
        sbuf[pl.ds(r0, PERM_ROWS), :] = (xhi & jnp.uint32(0xFFFF0000)) | (xlo >> 16)
        return c

    lax.fori_loop(0, info_ref[0, 2, 1], build, 0)

    def chunk_copy(src, dst):
        return pltpu.make_async_copy(sbuf.at[pl.ds(pl.multiple_of(src, ROW_ALIGN), DISP_CHUNK), :],
                                     xs_ref.at[pl.ds(pl.multiple_of(dst, ROW_ALIGN), DISP_CHUNK), :], sem)

    def issue(c, carry):
        chunk_copy(info_ref[0, 0, c], info_ref[0, 1, c]).start()
        return carry

    def drain(c, carry):
        chunk_copy(0, 0).wait()
        return carry

    n_chunks = info_ref[0, 2, 0]
    lax.fori_loop(0, n_chunks, issue, 0)
    lax.fori_loop(0, n_chunks, drain, 0)


def _sorted_rows(td):
    need = TOP_K * td + N_EXPERTS * (ROW_ALIGN - 1) + DISP_CHUNK
    return -(-need // PERM_ROWS) * PERM_ROWS


def _disp_call(pends, zfrom, info, lpos, h2p, n_rows, td, blk):
    t, w = h2p.shape
    nt = t // td
    return pl.pallas_call(
        _disp_kernel,
        name="disp",
        grid_spec=pltpu.PrefetchScalarGridSpec(
            num_scalar_prefetch=2,
            grid=(nt,),
            in_specs=[pl.BlockSpec((1,) + info.shape[1:], lambda i, pe, pa: (i, 0, 0), memory_space=pltpu.SMEM),
                      pl.BlockSpec((1, TOP_K, td), lambda i, pe, pa: (i, 0, 0)),
                      pl.BlockSpec((td, w), lambda i, pe, pa: (i, 0))],
            out_specs=pl.BlockSpec(memory_space=pl.ANY),
            scratch_shapes=[pltpu.VMEM((_sorted_rows(td), w), jnp.uint32),
                            pltpu.VMEM((blk, w), jnp.uint32),
                            pltpu.SemaphoreType.DMA(()), pltpu.SemaphoreType.DMA(())],
        ),
        out_shape=jax.ShapeDtypeStruct((n_rows, w), jnp.uint32),
        compiler_params=pltpu.CompilerParams(dimension_semantics=("arbitrary",),
                                             vmem_limit_bytes=VMEM_LIMIT),
    )(pends, zfrom, info, lpos, h2p)


def _exp_kernel(bexp_ref, nused_ref, xs_ref, wgu_ref, wd_ref, ys_ref):
    i = pl.program_id(0)

    @pl.when(i < nused_ref[0])
    def _():
        half = D_MODEL // 2
        lo, hi = _unpack_bf16_pairs(xs_ref[...])
        gu = (_dot(lo.astype(BF16), wgu_ref[0, :half, :])
              + _dot(hi.astype(BF16), wgu_ref[0, half:, :]))
        g, u = gu[:, :EXPERT_DIM], gu[:, EXPERT_DIM:]
        a = (g * _sigmoid(g) * u).astype(BF16)
        ys_ref[...] = _dot(a, wd_ref[0]).astype(BF16)


def _exp_call(block_exp, n_used, xs, wgu, wd, blk):
    n_rows, w = xs.shape
    n_blocks = n_rows // blk

    def row_map(i, bexp, nused):
        return (jnp.minimum(i, nused[0] - 1), 0)

    def w_map(i, bexp, nused):
        return (bexp[jnp.minimum(i, nused[0] - 1)], 0, 0)

    return pl.pallas_call(
        _exp_kernel,
        name="exp",
        grid_spec=pltpu.PrefetchScalarGridSpec(
            num_scalar_prefetch=2,
            grid=(n_blocks,),
            in_specs=[pl.BlockSpec((blk, w), row_map),
                      pl.BlockSpec((1, D_MODEL, 2 * EXPERT_DIM), w_map),
                      pl.BlockSpec((1, EXPERT_DIM, D_MODEL), w_map)],
            out_specs=pl.BlockSpec((blk, D_MODEL), row_map),
        ),
        out_shape=jax.ShapeDtypeStruct((n_rows, D_MODEL), BF16),
        compiler_params=pltpu.CompilerParams(dimension_semantics=("arbitrary",),
                                             vmem_limit_bytes=VMEM_LIMIT),
    )(block_exp, n_used, xs, wgu, wd)


def _comb_kernel(info_ref, ybase_ref, mod_ref, cpos_ref, w_ref, gf_ref, ys_ref, o_ref, ybuf, acc, sem):
    tc = ybase_ref.shape[1]
    first = jnp.logical_and(pl.program_id(0) == 0, pl.program_id(1) == 0)

    @pl.when(first)
    def _():
        ybuf[...] = jnp.zeros_like(ybuf)

    def chunk_copy(src, dst):
        return pltpu.make_async_copy(ys_ref.at[src], ybuf.at[dst], sem)

    def issue(c, carry):
        chunk_copy(info_ref[0, 0, c], c).start()
        return carry

    def drain(c, carry):
        chunk_copy(0, 0).wait()
        return carry

    n_chunks = info_ref[0, 1, 0]
    lax.fori_loop(0, n_chunks, issue, 0)
    lax.fori_loop(0, n_chunks, drain, 0)

    cpos = cpos_ref[...]
    w = w_ref[...]
    cpos_b = [jnp.broadcast_to(cpos[:, k:k + 1], (tc, LANES)).astype(jnp.int16) for k in range(TOP_K)]
    w_b = [jnp.broadcast_to(w[:, k:k + 1], (tc, LANES)).astype(BF16) for k in range(TOP_K)]
    lane = lax.broadcasted_iota(jnp.int32, (tc, LANES), 1)
    acc[...] = jnp.zeros_like(acc)

    def slab(kt, c):
        k0 = pl.multiple_of(kt * COMB_KTILE, COMB_KTILE)
        cols = []
        for j in range(COMB_KTILE // LANES):
            col = (lane + (k0 + j * LANES)).astype(jnp.int16)
            wm = jnp.zeros((tc, LANES), BF16)
            for k in range(TOP_K):
                wm = jnp.where(cpos_b[k] == col, w_b[k], wm)
            cols.append(wm)
        rows = ybuf[pl.ds(kt * (COMB_KTILE // COMB_CHUNK), COMB_KTILE // COMB_CHUNK)]
        acc[...] += _dot(jnp.concatenate(cols, axis=1), rows.reshape(COMB_KTILE, rows.shape[-1]))
        return c

    lax.fori_loop(0, info_ref[0, 1, 1], slab, 0)
    ga2 = mod_ref[0][5:6]
    o_ref[0] = _rms(ybase_ref[0] + ga2 * acc[...], gf_ref[...])


def _gather_rows(tc):
    need = TOP_K * tc + N_EXPERTS * 2 * (COMB_CHUNK - 1)
    return -(-need // COMB_KTILE) * COMB_KTILE


def _comb_call(info, ybase, mod, cpos_tok, w_tok, g_final, ys, tc):
    b, s, d = ybase.shape
    nt = s // tc
    return pl.pallas_call(
        _comb_kernel,
        name="comb",
        grid=(b, nt),
        in_specs=[pl.BlockSpec((1,) + info.shape[1:], lambda bi, i: (bi * nt + i, 0, 0), memory_space=pltpu.SMEM),
                  pl.BlockSpec((1, tc, d), lambda bi, i: (bi, i, 0)),
                  pl.BlockSpec((1, N_MOD, d), lambda bi, i: (bi, 0, 0)),
                  pl.BlockSpec((tc, TOP_K), lambda bi, i: (bi * nt + i, 0)),
                  pl.BlockSpec((tc, TOP_K), lambda bi, i: (bi * nt + i, 0)),
                  pl.BlockSpec((1, d), lambda bi, i: (0, 0)),
                  pl.BlockSpec(memory_space=pl.ANY)],
        out_specs=pl.BlockSpec((1, tc, d), lambda bi, i: (bi, i, 0)),
        out_shape=jax.ShapeDtypeStruct((b, s, d), F32),
        scratch_shapes=[pltpu.VMEM((_gather_rows(tc) // COMB_CHUNK, COMB_CHUNK, d), BF16), pltpu.VMEM((tc, d), F32),
                        pltpu.SemaphoreType.DMA(())],
        compiler_params=pltpu.CompilerParams(dimension_semantics=("arbitrary", "arbitrary"),
                                             vmem_limit_bytes=VMEM_LIMIT),
    )(info, ybase, mod, cpos_tok, w_tok, g_final, ys)


def _prep_weights(w_ada, b_ada, g_norm1, w_in, g_q_lat, w_uq, g_kv_lat, w_ukv, g_gmlp_v, w_spatial,
                  b_spatial, g_out_attn, g_out_gmlp, w_out, g_norm2, w_router, b_router, w_gate_e,
                  w_up_e, w_down_e, w_gate_s, w_up_s, w_down_s, g_final):
    row = lambda g: g.reshape(1, -1).astype(F32)
    o1 = Q_LORA
    o2 = o1 + KV_LORA
    o3 = o2 + QK_ROPE
    o4 = o3 + GMLP_W
    kr_cols = jnp.pad(w_in[:, o2:o3], ((0, 0), (QK_NOPE, LANES - QK_NOPE - QK_ROPE)))
    win = jnp.concatenate([w_in[:, :o2], w_in[:, o3:o4], w_in[:, o4:], kr_cols], axis=1).astype(BF16)
    qd = QK_NOPE + QK_ROPE
    wuq = jnp.pad(w_uq.reshape(Q_LORA, N_HEADS, qd), ((0, 0), (0, 0), (0, HEAD_PAD - qd)))
    wuq = wuq.reshape(Q_LORA, N_HEADS * HEAD_PAD).astype(BF16)
    wkv = w_ukv.reshape(KV_LORA, N_HEADS, QK_NOPE + V_DIM)
    wuk = jnp.pad(wkv[:, :, :QK_NOPE], ((0, 0), (0, 0), (0, HEAD_PAD - QK_NOPE)))
    wuk = wuk.reshape(KV_LORA, N_HEADS * HEAD_PAD).astype(BF16)
    wuv = jnp.pad(wkv[:, :, QK_NOPE:], ((0, 0), (0, 0), (0, HEAD_PAD - V_DIM)))
    wuv = wuv.reshape(KV_LORA, N_HEADS * HEAD_PAD).astype(BF16)
    vone = jnp.tile((jnp.arange(HEAD_PAD) == V_DIM).astype(F32), N_HEADS).reshape(1, -1)
    wsp = w_spatial.reshape(N_HEADS // 2, 2, CHUNK, CHUNK).transpose(0, 2, 1, 3)
    wsp = wsp.reshape(N_HEADS // 2, CHUNK, 2 * CHUNK).astype(BF16)
    bsp = jnp.repeat(jnp.transpose(b_spatial), GMLP_W // N_HEADS, axis=1).astype(F32)
    pre = (row(g_norm1), win, row(g_q_lat), wuq, row(g_kv_lat), wuk, wuv, vone, row(g_gmlp_v), wsp,
           bsp, row(g_out_gmlp))
    perm = (jnp.arange(N_GROUPS)[None, :] * GROUP_SIZE + jnp.arange(GROUP_SIZE)[:, None]).reshape(-1)
    wr = jnp.transpose(w_router)[perm].astype(BF16)
    br = b_router.astype(F32)[perm].reshape(N_EXPERTS, 1)
    mla_w = N_HEADS * V_DIM
    wgus = jnp.concatenate([w_gate_s, w_up_s], axis=1).astype(BF16)
    post = (row(g_out_attn), w_out[:mla_w].astype(BF16), w_out[mla_w:].astype(BF16), row(g_norm2),
            wr, br, wgus, w_down_s.astype(BF16))
    wgu_e = jnp.concatenate([w_gate_e, w_up_e], axis=2).astype(BF16)
    wd_e = w_down_e.astype(BF16)
    return w_ada.astype(BF16), b_ada, pre, post, (wgu_e, wd_e), row(g_final)


def _rope_tables(s):
    inv = 1.0 / (ROPE_THETA ** (jnp.arange(0, QK_ROPE, 2, dtype=F32) / QK_ROPE))
    ang = jnp.arange(s, dtype=F32)[:, None] * inv[None, :]
    cos, sin = jnp.cos(ang), jnp.sin(ang)
    z = lambda n: jnp.zeros((s, n), F32)
    tail = LANES - QK_NOPE - QK_ROPE
    c = jnp.concatenate([jnp.ones((s, QK_NOPE), F32), cos, cos, z(tail)], axis=1)
    s1 = jnp.concatenate([z(QK_NOPE), -sin, z(HALF_ROPE), z(tail)], axis=1)
    s2 = jnp.concatenate([z(QK_NOPE), z(HALF_ROPE), sin, z(tail)], axis=1)
    return c, s1, s2


def _tiles(s):
    ts = min(512, s)
    tq = min(512, s)
    tkc = min(1024, s)
    blk = 512
    tm = min(256, s)
    return ts, tq, tkc, blk, tm


def _ceil_to(x, m):
    return (x + m - 1) // m * m


def _trunk(x, c, prep, tiles=None):
    w_ada, b_ada, pre_w, post_w, exp_w, g_final = prep
    b, s, d = x.shape
    ts, tq, tkc, blk, tm = tiles or _tiles(s)
    t = b * s
    nt = t // tm
    mod = _modulation(c, w_ada, b_ada)
    q, k, v, gm = _pre_call(x, mod, _rope_tables(s), pre_w, ts)
    attn = _attn_call(q, k, v, tq, tkc)
    ybase, h2p, e_arr, lrank, w_arr, cnt = _post_call(attn, gm, x, mod, post_w, tm)

    i32 = jnp.int32
    cnt = cnt[:, :, 0].astype(i32).reshape(nt, GROUP_SIZE, N_GROUPS).transpose(0, 2, 1).reshape(nt, N_EXPERTS)
    cnt8 = _ceil_to(cnt, ROW_ALIGN)
    base8 = jnp.cumsum(cnt8, axis=0) - cnt8
    total8 = jnp.sum(cnt8, axis=0)
    padded = _ceil_to(total8 + EXPERT_SLACK, blk)
    pends = jnp.cumsum(padded).astype(i32)
    zfrom = ((pends - padded + total8) // blk * blk).astype(i32)
    dstbase = (pends - padded)[None, :] + base8
    toff8 = jnp.cumsum(cnt8, axis=1) - cnt8
    eids = jnp.arange(N_EXPERTS, dtype=i32)

    def chunk_table(nch, chunk, n_max):
        cend = jnp.cumsum(nch, axis=1)
        cidx = jnp.arange(n_max, dtype=i32)
        e_of_c = jnp.minimum(jnp.sum((cend[:, None, :] <= cidx[None, :, None]).astype(i32), axis=-1),
                             N_EXPERTS - 1)
        pick = lambda tbl: jnp.sum(jnp.where(e_of_c[..., None] == eids, tbl[:, None, :], 0), axis=-1)
        rel = lambda first_row: pick(first_row - (cend - nch) * chunk) + cidx[None, :] * chunk
        return rel, cend[:, -1]

    n_dmax = N_EXPERTS + TOP_K * tm // DISP_CHUNK
    nch_d = (cnt + DISP_CHUNK - 1) // DISP_CHUNK
    rel_d, n_dch = chunk_table(nch_d, DISP_CHUNK, n_dmax)
    n_rb = (jnp.sum(cnt8, axis=1) + DISP_CHUNK + PERM_ROWS - 1) // PERM_ROWS
    tail = lambda a, b2, n: jnp.concatenate([a[:, None], b2[:, None], jnp.zeros((nt, n - 2), i32)], axis=1)
    dinfo = jnp.stack([rel_d(toff8), rel_d(dstbase), tail(n_dch, n_rb, n_dmax)], axis=1).astype(i32)

    shift = dstbase % COMB_CHUNK
    nch_c = jnp.where(cnt > 0, (cnt + shift + COMB_CHUNK - 1) // COMB_CHUNK, 0)
    boff = (jnp.cumsum(nch_c, axis=1) - nch_c) * COMB_CHUNK
    n_cmax = _gather_rows(tm) // COMB_CHUNK
    rel_c, n_cch = chunk_table(nch_c, COMB_CHUNK, n_cmax)
    n_kt = (n_cch * COMB_CHUNK + COMB_KTILE - 1) // COMB_KTILE
    cinfo = jnp.stack([rel_c(dstbase - shift) // COMB_CHUNK, tail(n_cch, n_kt, n_cmax)], axis=1).astype(i32)

    onehot = e_arr[..., None] == eids
    lookup = lambda tbl: jnp.sum(jnp.where(onehot, tbl[:, None, None, :], 0), axis=-1)
    lpos = lookup(toff8) + lrank
    cpos = lookup(boff + shift) + lrank
    tok_major = lambda a: jnp.transpose(a, (0, 2, 1)).reshape(t, TOP_K)

    n_rows = _ceil_to(t * TOP_K + nt * N_EXPERTS * (ROW_ALIGN - 1) + N_EXPERTS * (EXPERT_SLACK + blk - 1), blk)
    n_blocks = n_rows // blk
    n_used = (pends[-1] // blk).astype(i32).reshape(1)
    block_start = jnp.arange(n_blocks, dtype=i32) * blk
    block_exp = jnp.minimum(jnp.sum((pends[None, :] <= block_start[:, None]).astype(i32), axis=1),
                            N_EXPERTS - 1)

    xs = _disp_call(pends, zfrom, dinfo, lpos.astype(i32), h2p, n_rows, tm, blk)
    ys = _exp_call(block_exp, n_used, xs, exp_w[0], exp_w[1], blk)
    ys3 = ys.reshape(n_rows // COMB_CHUNK, COMB_CHUNK, d)
    return _comb_call(cinfo, ybase, mod, tok_major(cpos).astype(i32), tok_major(w_arr), g_final, ys3, tm)


def kernel(x_prompt, x_sample, c_prompt, c_sample, w_ada, b_ada, g_norm1, w_in, g_q_lat, w_uq, g_kv_lat, w_ukv, g_gmlp_v, w_spatial, b_spatial, g_out_attn, g_out_gmlp, w_out, g_norm2, w_router, b_router, w_gate_e, w_up_e, w_down_e, w_gate_s, w_up_s, w_down_s, g_final):
    prep = _prep_weights(w_ada[0], b_ada[0], g_norm1[0], w_in[0], g_q_lat[0], w_uq[0], g_kv_lat[0],
                         w_ukv[0], g_gmlp_v[0], w_spatial[0], b_spatial[0], g_out_attn[0],
                         g_out_gmlp[0], w_out[0], g_norm2[0], w_router[0], b_router[0], w_gate_e[0],
                         w_up_e[0], w_down_e[0], w_gate_s[0], w_up_s[0], w_down_s[0], g_final)
    return (_trunk(x_prompt, c_prompt, prep), _trunk(x_sample, c_sample, prep))
```

```python
import functools
import math

import jax
import jax.numpy as jnp
from jax import lax
from jax.experimental import pallas as pl
from jax.experimental.pallas import tpu as pltpu

F32 = jnp.float32
BF16 = jnp.bfloat16

D_MODEL = 1024
N_HEADS = 8
QK_NOPE = 64
QK_ROPE = 32
V_DIM = 64
Q_LORA = 256
KV_LORA = 128
GMLP_W = 512
CHUNK = 128
N_EXPERTS = 64
TOP_K = 8
N_GROUPS = 8
TOPK_GROUPS = 4
GROUP_SIZE = N_EXPERTS // N_GROUPS
EXPERT_DIM = 256
SHARED_DIM = 256
ROUTED_SCALE = 2.5
ROPE_THETA = 10000.0
N_MOD = 6
EPS = 1e-6

LANES = 128
HEAD_PAD = 128
HALF_ROPE = QK_ROPE // 2
VMEM_LIMIT = 52 * 1024 * 1024
ROW_ALIGN = 8
DISP_CHUNK = 32
COMB_CHUNK = 16
ATTN_CHUNKS_PER_STEP = 4
PERM_ROWS = 256
COMB_KTILE = 256
EXPERT_SLACK = max(DISP_CHUNK, 2 * (COMB_CHUNK - 1))

SOFTMAX_SCALE = (QK_NOPE + QK_ROPE) ** -0.5
EXP2_SCALE = SOFTMAX_SCALE * math.log2(math.e)


def _rms(x, g):
    return x * lax.rsqrt(jnp.mean(x * x, axis=-1, keepdims=True) + EPS) * g


def _sigmoid(x):
    return 1.0 / (1.0 + jnp.exp(-x))


def _gelu_tanh(x):
    c = math.sqrt(2.0 / math.pi)
    return 0.5 * x * (1.0 + jnp.tanh(c * (x + 0.044715 * (x * x * x))))


def _dot(a, b):
    return jnp.dot(a, b, preferred_element_type=F32)


def _dot_nt(a, b):
    return lax.dot_general(a, b, (((1,), (1,)), ((), ())), preferred_element_type=F32)


def _pack_bf16_pairs(x):
    c = x.shape[1] // 2
    lo = pltpu.bitcast(x[:, :c].astype(BF16).astype(F32), jnp.uint32)
    hi = pltpu.bitcast(x[:, c:].astype(BF16).astype(F32), jnp.uint32)
    return (hi & jnp.uint32(0xFFFF0000)) | (lo >> 16)


def _unpack_bf16_pairs(w):
    lo = pltpu.bitcast(w << 16, F32)
    hi = pltpu.bitcast(w & jnp.uint32(0xFFFF0000), F32)
    return lo, hi


def _mod_kernel(c_ref, w_ref, b_ref, o_ref):
    c = c_ref[...]
    a = (c * _sigmoid(c)).astype(BF16)
    o_ref[...] = _dot(a, w_ref[...]) + b_ref[...]


def _modulation(c, w_ada_bf, b_ada):
    b = c.shape[0]
    bp = max(16, -(-b // 16) * 16)
    cp = jnp.pad(c, ((0, bp - b), (0, 0)))
    n = w_ada_bf.shape[1]
    tn = D_MODEL
    out = pl.pallas_call(
        _mod_kernel,
        name="mod",
        grid=(n // tn,),
        in_specs=[
            pl.BlockSpec((bp, D_MODEL), lambda j: (0, 0)),
            pl.BlockSpec((D_MODEL, tn), lambda j: (0, j)),
            pl.BlockSpec((1, tn), lambda j: (0, j)),
        ],
        out_specs=pl.BlockSpec((bp, tn), lambda j: (0, j)),
        out_shape=jax.ShapeDtypeStruct((bp, n), F32),
        compiler_params=pltpu.CompilerParams(dimension_semantics=("arbitrary",)),
    )(cp, w_ada_bf, b_ada.reshape(1, n))
    return out[:b].reshape(b, N_MOD, D_MODEL)


def _rope(xh, c, s1, s2):
    return (xh * c + pltpu.roll(xh, LANES - HALF_ROPE, axis=1) * s1
            + pltpu.roll(xh, HALF_ROPE, axis=1) * s2)


def _pre_kernel(x_ref, mod_ref, cos_ref, s1_ref, s2_ref, g1_ref, win_ref, gq_ref, wuq_ref,
                gkv_ref, wuk_ref, wuv_ref, vone_ref, ggv_ref, ws_ref, bs_ref, ggo_ref,
                q_ref, k_ref, v_ref, gm_ref, mix_ref):
    ts = x_ref.shape[1]
    x = x_ref[0]
    mod = mod_ref[0]
    h = _rms(x, g1_ref[...]) * (1.0 + mod[1:2]) + mod[0:1]
    z = _dot(h.astype(BF16), win_ref[...])
    o_kv = Q_LORA
    o_gu = o_kv + KV_LORA
    o_gv = o_gu + GMLP_W
    o_kr = o_gv + GMLP_W
    q_lat = z[:, :o_kv]
    kv_lat = z[:, o_kv:o_gu]
    g_u = z[:, o_gu:o_gv]
    g_v = z[:, o_gv:o_kr]
    kr = z[:, o_kr:o_kr + LANES]

    cos = cos_ref[...]
    s1 = s1_ref[...]
    s2 = s2_ref[...]

    qn = _rms(q_lat, gq_ref[...]).astype(BF16)
    q = _dot(qn, wuq_ref[...])
    kn = _rms(kv_lat, gkv_ref[...]).astype(BF16)
    kf = _dot(kn, wuk_ref[...])
    v_ref[0] = (_dot(kn, wuv_ref[...]) + vone_ref[...]).astype(BF16)
    krr = _rope(kr, cos, s1, s2)
    for hd in range(N_HEADS):
        sl = slice(hd * HEAD_PAD, (hd + 1) * HEAD_PAD)
        q_ref[0, :, sl] = (_rope(q[:, sl], cos, s1, s2) * EXP2_SCALE).astype(BF16)
        k_ref[0, :, sl] = (kf[:, sl] + krr).astype(BF16)

    u = _gelu_tanh(g_u)
    vn = _rms(_gelu_tanh(g_v), ggv_ref[...]).astype(BF16)
    lane = lax.broadcasted_iota(jnp.int32, (CHUNK, LANES), 1)
    left = lane < (LANES // 2)
    zero = jnp.zeros((CHUNK, LANES), BF16)
    for n in range(ts // CHUNK):
        rs = slice(n * CHUNK, (n + 1) * CHUNK)
        for p in range(GMLP_W // LANES):
            cs = slice(p * LANES, (p + 1) * LANES)
            vp = vn[rs, cs]
            rhs = jnp.concatenate([jnp.where(left, vp, zero), jnp.where(left, zero, vp)], axis=0)
            mix_ref[rs, cs] = _dot(ws_ref[p], rhs) + bs_ref[:, cs]
    gm = u * mix_ref[...]
    gm_ref[0] = _rms(gm, ggo_ref[...]).astype(BF16)


def _pre_call(x, mod, tabs, wts, ts):
    b, s, d = x.shape
    cos, s1, s2 = tabs
    (g1, win, gq, wuq, gkv, wuk, wuv, vone, ggv, wsp, bsp, ggo) = wts
    full = lambda a: pl.BlockSpec(a.shape, lambda bi, i: (0,) * a.ndim)
    tab = pl.BlockSpec((ts, LANES), lambda bi, i: (i, 0))
    out_shapes = (
        jax.ShapeDtypeStruct((b, s, N_HEADS * HEAD_PAD), BF16),
        jax.ShapeDtypeStruct((b, s, N_HEADS * HEAD_PAD), BF16),
        jax.ShapeDtypeStruct((b, s, N_HEADS * HEAD_PAD), BF16),
        jax.ShapeDtypeStruct((b, s, GMLP_W), BF16),
    )
    tok = lambda w: pl.BlockSpec((1, ts, w), lambda bi, i: (bi, i, 0))
    return pl.pallas_call(
        _pre_kernel,
        name="pre",
        grid=(b, s // ts),
        in_specs=[tok(d), pl.BlockSpec((1, N_MOD, d), lambda bi, i: (bi, 0, 0)), tab, tab, tab,
                  full(g1), full(win), full(gq), full(wuq), full(gkv), full(wuk), full(wuv), full(vone),
                  full(ggv), full(wsp), full(bsp), full(ggo)],
        out_specs=(tok(N_HEADS * HEAD_PAD), tok(N_HEADS * HEAD_PAD), tok(N_HEADS * HEAD_PAD), tok(GMLP_W)),
        out_shape=out_shapes,
        scratch_shapes=[pltpu.VMEM((ts, GMLP_W), F32)],
        compiler_params=pltpu.CompilerParams(dimension_semantics=("arbitrary", "arbitrary"),
                                             vmem_limit_bytes=VMEM_LIMIT),
    )(x, mod, cos, s1, s2, g1, win, gq, wuq, gkv, wuk, wuv, vone, ggv, wsp, bsp, ggo)


def _attn_kernel(q_ref, k_ref, v_ref, o_ref, *, tkc, cpb):
    tq = q_ref.shape[1]
    n_chunks = k_ref.shape[1] // tkc
    heads = [slice(hh * HEAD_PAD, (hh + 1) * HEAD_PAD) for hh in range(2)]
    qs = [q_ref[0, :, hs] for hs in heads]

    def step(i, carry):
        carry = list(carry)
        for cc in range(cpb):
            off = pl.multiple_of((i * cpb + cc) * tkc, tkc)
            for hh, hs in enumerate(heads):
                m, acc = carry[2 * hh], carry[2 * hh + 1]
                kc = k_ref[0, pl.ds(off, tkc), hs]
                vc = v_ref[0, pl.ds(off, tkc), hs]
                sc = _dot_nt(qs[hh], kc)
                m_new = jnp.maximum(m, jnp.max(sc, axis=-1, keepdims=True))
                p = jnp.exp2(sc - m_new).astype(BF16)
                carry[2 * hh + 1] = jnp.exp2(m - m_new) * acc + _dot(p, vc)
                carry[2 * hh] = m_new
        return tuple(carry)

    n_steps = n_chunks // cpb
    init = (jnp.full((tq, 1), -jnp.inf, F32), jnp.zeros((tq, HEAD_PAD), F32)) * 2
    res = step(0, init) if n_steps == 1 else lax.fori_loop(0, n_steps, step, init)
    o0 = res[1] / res[1][:, V_DIM:V_DIM + 1]
    o1 = res[3] / res[3][:, V_DIM:V_DIM + 1]
    lane = lax.broadcasted_iota(jnp.int32, (tq, HEAD_PAD), 1)
    o_ref[0] = jnp.where(lane < V_DIM, o0, pltpu.roll(o1, V_DIM, axis=1)).astype(BF16)


def _attn_call(q, k, v, tq, tkc):
    b, s, _ = q.shape
    hp = N_HEADS // 2
    resident = lambda: pl.BlockSpec((1, s, 2 * HEAD_PAD), lambda bi, h, i: (bi, 0, h),
                                    pipeline_mode=pl.Buffered(1))
    return pl.pallas_call(
        functools.partial(_attn_kernel, tkc=tkc, cpb=math.gcd(s // tkc, ATTN_CHUNKS_PER_STEP)),
        name="attn",
        grid=(b, hp, s // tq),
        in_specs=[pl.BlockSpec((1, tq, 2 * HEAD_PAD), lambda bi, h, i: (bi, i, h)),
                  resident(), resident()],
        out_specs=pl.BlockSpec((1, tq, 2 * V_DIM), lambda bi, h, i: (bi, i, h)),
        out_shape=jax.ShapeDtypeStruct((b, s, N_HEADS * V_DIM), BF16),
        compiler_params=pltpu.CompilerParams(
            dimension_semantics=("arbitrary", "arbitrary", "arbitrary"),
            vmem_limit_bytes=VMEM_LIMIT),
    )(q, k, v)


def _post_kernel(attn_ref, gm_ref, x_ref, mod_ref, goa_ref, woa_ref, wog_ref, g2_ref, wr_ref,
                 br_ref, wgus_ref, wds_ref,
                 ybase_ref, h2p_ref, e_ref, rank_ref, w_ref, cnt_ref):
    ts = x_ref.shape[1]
    mod = mod_ref[0]
    ga1, sh2, sc2, ga2 = mod[2:3], mod[3:4], mod[4:5], mod[5:6]
    an = _rms(attn_ref[0].astype(F32), goa_ref[...]).astype(BF16)
    y = _dot(an, woa_ref[...]) + _dot(gm_ref[0], wog_ref[...])
    x1 = x_ref[0] + ga1 * y
    h2 = _rms(x1, g2_ref[...]) * (1.0 + sc2) + sh2
    h2b = h2.astype(BF16)
    h2p_ref[...] = _pack_bf16_pairs(h2)

    gu = _dot(h2b, wgus_ref[...])
    g, u = gu[:, :SHARED_DIM], gu[:, SHARED_DIM:]
    a = (g * _sigmoid(g) * u).astype(BF16)
    ybase_ref[0] = x1 + ga2 * _dot(a, wds_ref[...])

    logits = _dot_nt(wr_ref[...], h2b)
    scores = _sigmoid(logits)
    biased = scores + br_ref[...]
    ninf = jnp.float32(-jnp.inf)
    bj = [biased[j * N_GROUPS:(j + 1) * N_GROUPS] for j in range(GROUP_SIZE)]
    sj = [scores[j * N_GROUPS:(j + 1) * N_GROUPS] for j in range(GROUP_SIZE)]
    m1 = bj[0]
    for j in range(1, GROUP_SIZE):
        m1 = jnp.maximum(m1, bj[j])
    found = jnp.zeros_like(m1)
    m2 = jnp.full_like(m1, ninf)
    for j in range(GROUP_SIZE):
        eq = jnp.where(bj[j] == m1, 1.0, 0.0)
        is_first = eq * (1.0 - found)
        found = jnp.maximum(found, eq)
        m2 = jnp.maximum(m2, jnp.where(is_first > 0.0, ninf, bj[j]))
    gs = m1 + m2
    gidx = lax.broadcasted_iota(jnp.int32, gs.shape, 0)
    grank = jnp.zeros_like(gs)
    for kk in range(1, N_GROUPS):
        r = pltpu.roll(gs, kk, axis=0)
        grank = grank + jnp.where(gidx >= kk, jnp.where(r >= gs, 1.0, 0.0), jnp.where(r > gs, 1.0, 0.0))
    gsel = grank < float(TOPK_GROUPS)
    masked = [jnp.where(gsel, bj[j], ninf) for j in range(GROUP_SIZE)]
    eidx = [gidx * GROUP_SIZE + j for j in range(GROUP_SIZE)]

    selm = [jnp.zeros_like(gs) for _ in range(GROUP_SIZE)]
    e_sel = []
    for _k in range(TOP_K):
        m = masked[0]
        for j in range(1, GROUP_SIZE):
            m = jnp.maximum(m, masked[j])
        m = jnp.max(m, axis=0, keepdims=True)
        cand = jnp.where(masked[0] == m, eidx[0], N_EXPERTS)
        for j in range(1, GROUP_SIZE):
            cand = jnp.minimum(cand, jnp.where(masked[j] == m, eidx[j], N_EXPERTS))
        emin = jnp.min(cand, axis=0, keepdims=True)
        e_sel.append(emin)
        for j in range(GROUP_SIZE):
            hit = eidx[j] == emin
            selm[j] = jnp.where(hit, 1.0, selm[j])
            masked[j] = jnp.where(hit, ninf, masked[j])

    wsel = [selm[j] * sj[j] for j in range(GROUP_SIZE)]
    tot = wsel[0]
    for j in range(1, GROUP_SIZE):
        tot = tot + wsel[j]
    tot = jnp.sum(tot, axis=0, keepdims=True)
    wn = [wsel[j] / tot * ROUTED_SCALE for j in range(GROUP_SIZE)]

    tm = e_ref.shape[2]
    sel = jnp.concatenate(selm, axis=0)
    tr = lax.broadcasted_iota(jnp.int32, (ts, ts), 0)
    tc = lax.broadcasted_iota(jnp.int32, (ts, ts), 1)
    sh = tm.bit_length() - 1
    same_tile = lax.shift_right_logical(tr, sh) == lax.shift_right_logical(tc, sh)
    upper = jnp.where(tr < tc, jnp.where(same_tile, 1.0, 0.0), 0.0).astype(BF16)
    rank_full = _dot(sel.astype(BF16), upper)
    rj = [rank_full[j * N_GROUPS:(j + 1) * N_GROUPS] for j in range(GROUP_SIZE)]
    subs = [slice(i * tm, (i + 1) * tm) for i in range(ts // tm)]
    for i, sub in enumerate(subs):
        cnt_ref[i] = jnp.broadcast_to(jnp.sum(sel[:, sub], axis=1, keepdims=True), (N_EXPERTS, LANES))

    for k in range(TOP_K):
        rk = jnp.zeros_like(gs)
        wk = jnp.zeros_like(gs)
        for j in range(GROUP_SIZE):
            hit = eidx[j] == e_sel[k]
            rk = rk + jnp.where(hit, rj[j], 0.0)
            wk = wk + jnp.where(hit, wn[j], 0.0)
        rk = jnp.sum(rk, axis=0, keepdims=True).astype(jnp.int32)
        wk = jnp.sum(wk, axis=0, keepdims=True)
        for i, sub in enumerate(subs):
            e_ref[i, k:k + 1, :] = e_sel[k][:, sub]
            rank_ref[i, k:k + 1, :] = rk[:, sub]
            w_ref[i, k:k + 1, :] = wk[:, sub]


def _post_call(attn, gm, x, mod, wts, ts, tm):
    b, s, d = x.shape
    nt = s // ts
    nsub = ts // tm
    ntm = b * nt * nsub
    (goa, woa, wog, g2, wr, br, wgus, wds) = wts
    full = lambda a: pl.BlockSpec(a.shape, lambda bi, i: (0,) * a.ndim)
    tok = lambda w: pl.BlockSpec((1, ts, w), lambda bi, i: (bi, i, 0))
    rout = pl.BlockSpec((nsub, TOP_K, tm), lambda bi, i: (bi * nt + i, 0, 0))
    out_shapes = (
        jax.ShapeDtypeStruct((b, s, d), F32),
        jax.ShapeDtypeStruct((b * s, d // 2), jnp.uint32),
        jax.ShapeDtypeStruct((ntm, TOP_K, tm), jnp.int32),
        jax.ShapeDtypeStruct((ntm, TOP_K, tm), jnp.int32),
        jax.ShapeDtypeStruct((ntm, TOP_K, tm), F32),
        jax.ShapeDtypeStruct((ntm, N_EXPERTS, LANES), F32),
    )
    return pl.pallas_call(
        _post_kernel,
        name="post",
        grid=(b, nt),
        in_specs=[tok(N_HEADS * V_DIM), tok(GMLP_W), tok(d),
                  pl.BlockSpec((1, N_MOD, d), lambda bi, i: (bi, 0, 0)),
                  full(goa), full(woa), full(wog), full(g2), full(wr), full(br), full(wgus), full(wds)],
        out_specs=(tok(d), pl.BlockSpec((ts, d // 2), lambda bi, i: (bi * nt + i, 0)),
                   rout, rout, rout,
                   pl.BlockSpec((nsub, N_EXPERTS, LANES), lambda bi, i: (bi * nt + i, 0, 0))),
        out_shape=out_shapes,
        compiler_params=pltpu.CompilerParams(dimension_semantics=("arbitrary", "arbitrary"),
                                             vmem_limit_bytes=VMEM_LIMIT),
    )(attn, gm, x, mod, goa, woa, wog, g2, wr, br, wgus, wds)


def _disp_kernel(pends_ref, zfrom_ref, info_ref, lpos_ref, h_ref, xs_ref, sbuf, zbuf, nprev, sems, zsem):
    td = h_ref.shape[0]
    blk = zbuf.shape[0]
    step = pl.program_id(0)
    slot = lax.rem(step, 2)

    @pl.when(step == 0)
    def _():
        nprev[0] = 0
        zbuf[...] = jnp.zeros_like(zbuf)

        def tail_copy(e, j):
            start = pl.multiple_of(zfrom_ref[e] + j * blk, blk)
            return pltpu.make_async_copy(zbuf, xs_ref.at[pl.ds(start, blk), :], zsem)

        def n_tail(e):
            return (pends_ref[e] - zfrom_ref[e]) // blk

        def zstart(e, c):
            def one(j, c2):
                tail_copy(e, j).start()
                return c2
            return lax.fori_loop(0, n_tail(e), one, c)

        def zwait(e, c):
            def one(j, c2):
                tail_copy(e, j).wait()
                return c2
            return lax.fori_loop(0, n_tail(e), one, c)

        lax.fori_loop(0, N_EXPERTS, zstart, 0)
        lax.fori_loop(0, N_EXPERTS, zwait, 0)

    lo, hi = _unpack_bf16_pairs(h_ref[...])
    lo = lo.astype(BF16)
    hi = hi.astype(BF16)
    lpos = lpos_ref[0]
    lpos_b = [jnp.broadcast_to(lpos[k:k + 1, :], (PERM_ROWS, td)).astype(jnp.int16) for k in range(TOP_K)]
    row0 = lax.broadcasted_iota(jnp.int32, (PERM_ROWS, td), 0)
    one = jnp.ones((PERM_ROWS, td), BF16)

    def build(rb, c):
        r0 = pl.multiple_of(rb * PERM_ROWS, PERM_ROWS)
        riota = (row0 + r0).astype(jnp.int16)
        pb = jnp.zeros((PERM_ROWS, td), BF16)
        for k in range(TOP_K):
            pb = jnp.where(lpos_b[k] == riota, one, pb)
        xlo = pltpu.bitcast(_dot(pb, lo), jnp.uint32)
        xhi = pltpu.bitcast(_dot(pb, hi), jnp.uint32)
        sbuf[slot, pl.ds(r0, PERM_ROWS), :] = (xhi & jnp.uint32(0xFFFF0000)) | (xlo >> 16)
        return c

    lax.fori_loop(0, info_ref[0, 2, 1], build, 0)

    def chunk_copy(src, dst, sl):
        return pltpu.make_async_copy(
            sbuf.at[sl, pl.ds(pl.multiple_of(src, ROW_ALIGN), DISP_CHUNK), :],
            xs_ref.at[pl.ds(pl.multiple_of(dst, ROW_ALIGN), DISP_CHUNK), :], sems.at[sl])

    def drain(n, sl):
        def one(c, carry):
            chunk_copy(0, 0, sl).wait()
            return carry
        lax.fori_loop(0, n, one, 0)

    def issue(c, carry):
        chunk_copy(info_ref[0, 0, c], info_ref[0, 1, c], slot).start()
        return carry

    drain(nprev[0], 1 - slot)
    n_chunks = info_ref[0, 2, 0]
    lax.fori_loop(0, n_chunks, issue, 0)
    nprev[0] = n_chunks

    @pl.when(step == pl.num_programs(0) - 1)
    def _():
        drain(n_chunks, slot)


def _sorted_rows(td):
    need = TOP_K * td + N_EXPERTS * (ROW_ALIGN - 1) + DISP_CHUNK
    return -(-need // PERM_ROWS) * PERM_ROWS


def _disp_call(pends, zfrom, info, lpos, h2p, n_rows, td, blk):
    t, w = h2p.shape
    nt = t // td
    return pl.pallas_call(
        _disp_kernel,
        name="disp",
        grid_spec=pltpu.PrefetchScalarGridSpec(
            num_scalar_prefetch=2,
            grid=(nt,),
            in_specs=[pl.BlockSpec((1,) + info.shape[1:], lambda i, pe, pa: (i, 0, 0), memory_space=pltpu.SMEM),
                      pl.BlockSpec((1, TOP_K, td), lambda i, pe, pa: (i, 0, 0)),
                      pl.BlockSpec((td, w), lambda i, pe, pa: (i, 0))],
            out_specs=pl.BlockSpec(memory_space=pl.ANY),
            scratch_shapes=[pltpu.VMEM((2, _sorted_rows(td), w), jnp.uint32),
                            pltpu.VMEM((blk, w), jnp.uint32),
                            pltpu.SMEM((1,), jnp.int32),
                            pltpu.SemaphoreType.DMA((2,)), pltpu.SemaphoreType.DMA(())],
        ),
        out_shape=jax.ShapeDtypeStruct((n_rows, w), jnp.uint32),
        compiler_params=pltpu.CompilerParams(dimension_semantics=("arbitrary",),
                                             vmem_limit_bytes=VMEM_LIMIT),
    )(pends, zfrom, info, lpos, h2p)


def _exp_kernel(bexp_ref, nused_ref, xs_ref, wgu_ref, wd_ref, ys_ref):
    i = pl.program_id(0)

    @pl.when(i < nused_ref[0])
    def _():
        half = D_MODEL // 2
        lo, hi = _unpack_bf16_pairs(xs_ref[...])
        gu = (_dot(lo.astype(BF16), wgu_ref[0, :half, :])
              + _dot(hi.astype(BF16), wgu_ref[0, half:, :]))
        g, u = gu[:, :EXPERT_DIM], gu[:, EXPERT_DIM:]
        a = (g * _sigmoid(g) * u).astype(BF16)
        ys_ref[...] = _dot(a, wd_ref[0]).astype(BF16)


def _exp_call(block_exp, n_used, xs, wgu, wd, blk):
    n_rows, w = xs.shape
    n_blocks = n_rows // blk

    def row_map(i, bexp, nused):
        return (jnp.minimum(i, nused[0] - 1), 0)

    def w_map(i, bexp, nused):
        return (bexp[jnp.minimum(i, nused[0] - 1)], 0, 0)

    return pl.pallas_call(
        _exp_kernel,
        name="exp",
        grid_spec=pltpu.PrefetchScalarGridSpec(
            num_scalar_prefetch=2,
            grid=(n_blocks,),
            in_specs=[pl.BlockSpec((blk, w), row_map),
                      pl.BlockSpec((1, D_MODEL, 2 * EXPERT_DIM), w_map),
                      pl.BlockSpec((1, EXPERT_DIM, D_MODEL), w_map)],
            out_specs=pl.BlockSpec((blk, D_MODEL), row_map),
        ),
        out_shape=jax.ShapeDtypeStruct((n_rows, D_MODEL), BF16),
        compiler_params=pltpu.CompilerParams(dimension_semantics=("arbitrary",),
                                             vmem_limit_bytes=VMEM_LIMIT),
    )(block_exp, n_used, xs, wgu, wd)


def _comb_kernel(info_ref, next_info_ref, ybase_ref, mod_ref, cpos_ref, w_ref, gf_ref, ys_ref, o_ref,
                 ybuf, acc, sems):
    tc = ybase_ref.shape[1]
    step = pl.program_id(0) * pl.num_programs(1) + pl.program_id(1)
    n_steps = pl.num_programs(0) * pl.num_programs(1)
    slot = lax.rem(step, 2)

    def chunk_copy(src, dst, sl):
        return pltpu.make_async_copy(ys_ref.at[src], ybuf.at[sl, dst], sems.at[sl])

    def fetch(iref, sl):
        def issue(c, carry):
            chunk_copy(iref[0, 0, c], c, sl).start()
            return carry
        lax.fori_loop(0, iref[0, 1, 0], issue, 0)

    @pl.when(step == 0)
    def _():
        ybuf[...] = jnp.zeros_like(ybuf)
        fetch(info_ref, 0)

    @pl.when(step + 1 < n_steps)
    def _():
        fetch(next_info_ref, 1 - slot)

    def drain(c, carry):
        chunk_copy(0, 0, slot).wait()
        return carry

    lax.fori_loop(0, info_ref[0, 1, 0], drain, 0)

    cpos = cpos_ref[...]
    w = w_ref[...]
    cpos_b = [jnp.broadcast_to(cpos[:, k:k + 1], (tc, LANES)).astype(jnp.int16) for k in range(TOP_K)]
    w_b = [jnp.broadcast_to(w[:, k:k + 1], (tc, LANES)).astype(BF16) for k in range(TOP_K)]
    lane = lax.broadcasted_iota(jnp.int32, (tc, LANES), 1)
    acc[...] = jnp.zeros_like(acc)

    def slab(kt, c):
        k0 = pl.multiple_of(kt * COMB_KTILE, COMB_KTILE)
        cols = []
        for j in range(COMB_KTILE // LANES):
            col = (lane + (k0 + j * LANES)).astype(jnp.int16)
            wm = jnp.zeros((tc, LANES), BF16)
            for k in range(TOP_K):
                wm = jnp.where(cpos_b[k] == col, w_b[k], wm)
            cols.append(wm)
        rows = ybuf[slot, pl.ds(kt * (COMB_KTILE // COMB_CHUNK), COMB_KTILE // COMB_CHUNK)]
        acc[...] += _dot(jnp.concatenate(cols, axis=1), rows.reshape(COMB_KTILE, rows.shape[-1]))
        return c

    lax.fori_loop(0, info_ref[0, 1, 1], slab, 0)
    ga2 = mod_ref[0][5:6]
    o_ref[0] = _rms(ybase_ref[0] + ga2 * acc[...], gf_ref[...])


def _gather_rows(tc):
    need = TOP_K * tc + N_EXPERTS * 2 * (COMB_CHUNK - 1)
    return -(-need // COMB_KTILE) * COMB_KTILE


def _comb_call(info, ybase, mod, cpos_tok, w_tok, g_final, ys, tc):
    b, s, d = ybase.shape
    nt = s // tc
    return pl.pallas_call(
        _comb_kernel,
        name="comb",
        grid=(b, nt),
        in_specs=[pl.BlockSpec((1,) + info.shape[1:], lambda bi, i: (bi * nt + i, 0, 0), memory_space=pltpu.SMEM),
                  pl.BlockSpec((1,) + info.shape[1:], lambda bi, i: (jnp.minimum(bi * nt + i + 1, b * nt - 1), 0, 0),
                               memory_space=pltpu.SMEM),
                  pl.BlockSpec((1, tc, d), lambda bi, i: (bi, i, 0)),
                  pl.BlockSpec((1, N_MOD, d), lambda bi, i: (bi, 0, 0)),
                  pl.BlockSpec((tc, TOP_K), lambda bi, i: (bi * nt + i, 0)),
                  pl.BlockSpec((tc, TOP_K), lambda bi, i: (bi * nt + i, 0)),
                  pl.BlockSpec((1, d), lambda bi, i: (0, 0)),
                  pl.BlockSpec(memory_space=pl.ANY)],
        out_specs=pl.BlockSpec((1, tc, d), lambda bi, i: (bi, i, 0)),
        out_shape=jax.ShapeDtypeStruct((b, s, d), F32),
        scratch_shapes=[pltpu.VMEM((2, _gather_rows(tc) // COMB_CHUNK, COMB_CHUNK, d), BF16),
                        pltpu.VMEM((tc, d), F32), pltpu.SemaphoreType.DMA((2,))],
        compiler_params=pltpu.CompilerParams(dimension_semantics=("arbitrary", "arbitrary"),
                                             vmem_limit_bytes=VMEM_LIMIT),
    )(info, info, ybase, mod, cpos_tok, w_tok, g_final, ys)


def _prep_weights(w_ada, b_ada, g_norm1, w_in, g_q_lat, w_uq, g_kv_lat, w_ukv, g_gmlp_v, w_spatial,
                  b_spatial, g_out_attn, g_out_gmlp, w_out, g_norm2, w_router, b_router, w_gate_e,
                  w_up_e, w_down_e, w_gate_s, w_up_s, w_down_s, g_final):
    row = lambda g: g.reshape(1, -1).astype(F32)
    o1 = Q_LORA
    o2 = o1 + KV_LORA
    o3 = o2 + QK_ROPE
    o4 = o3 + GMLP_W
    kr_cols = jnp.pad(w_in[:, o2:o3], ((0, 0), (QK_NOPE, LANES - QK_NOPE - QK_ROPE)))
    win = jnp.concatenate([w_in[:, :o2], w_in[:, o3:o4], w_in[:, o4:], kr_cols], axis=1).astype(BF16)
    qd = QK_NOPE + QK_ROPE
    wuq = jnp.pad(w_uq.reshape(Q_LORA, N_HEADS, qd), ((0, 0), (0, 0), (0, HEAD_PAD - qd)))
    wuq = wuq.reshape(Q_LORA, N_HEADS * HEAD_PAD).astype(BF16)
    wkv = w_ukv.reshape(KV_LORA, N_HEADS, QK_NOPE + V_DIM)
    wuk = jnp.pad(wkv[:, :, :QK_NOPE], ((0, 0), (0, 0), (0, HEAD_PAD - QK_NOPE)))
    wuk = wuk.reshape(KV_LORA, N_HEADS * HEAD_PAD).astype(BF16)
    wuv = jnp.pad(wkv[:, :, QK_NOPE:], ((0, 0), (0, 0), (0, HEAD_PAD - V_DIM)))
    wuv = wuv.reshape(KV_LORA, N_HEADS * HEAD_PAD).astype(BF16)
    vone = jnp.tile((jnp.arange(HEAD_PAD) == V_DIM).astype(F32), N_HEADS).reshape(1, -1)
    wsp = w_spatial.reshape(N_HEADS // 2, 2, CHUNK, CHUNK).transpose(0, 2, 1, 3)
    wsp = wsp.reshape(N_HEADS // 2, CHUNK, 2 * CHUNK).astype(BF16)
    bsp = jnp.repeat(jnp.transpose(b_spatial), GMLP_W // N_HEADS, axis=1).astype(F32)
    pre = (row(g_norm1), win, row(g_q_lat), wuq, row(g_kv_lat), wuk, wuv, vone, row(g_gmlp_v), wsp,
           bsp, row(g_out_gmlp))
    perm = (jnp.arange(N_GROUPS)[None, :] * GROUP_SIZE + jnp.arange(GROUP_SIZE)[:, None]).reshape(-1)
    wr = jnp.transpose(w_router)[perm].astype(BF16)
    br = b_router.astype(F32)[perm].reshape(N_EXPERTS, 1)
    mla_w = N_HEADS * V_DIM
    wgus = jnp.concatenate([w_gate_s, w_up_s], axis=1).astype(BF16)
    post = (row(g_out_attn), w_out[:mla_w].astype(BF16), w_out[mla_w:].astype(BF16), row(g_norm2),
            wr, br, wgus, w_down_s.astype(BF16))
    wgu_e = jnp.concatenate([w_gate_e, w_up_e], axis=2).astype(BF16)
    wd_e = w_down_e.astype(BF16)
    return w_ada.astype(BF16), b_ada, pre, post, (wgu_e, wd_e), row(g_final)


def _rope_tables(s):
    inv = 1.0 / (ROPE_THETA ** (jnp.arange(0, QK_ROPE, 2, dtype=F32) / QK_ROPE))
    ang = jnp.arange(s, dtype=F32)[:, None] * inv[None, :]
    cos, sin = jnp.cos(ang), jnp.sin(ang)
    z = lambda n: jnp.zeros((s, n), F32)
    tail = LANES - QK_NOPE - QK_ROPE
    c = jnp.concatenate([jnp.ones((s, QK_NOPE), F32), cos, cos, z(tail)], axis=1)
    s1 = jnp.concatenate([z(QK_NOPE), -sin, z(HALF_ROPE), z(tail)], axis=1)
    s2 = jnp.concatenate([z(QK_NOPE), z(HALF_ROPE), sin, z(tail)], axis=1)
    return c, s1, s2


def _tiles(s):
    ts = min(512, s)
    tq = min(512, s)
    tkc = min(1024, s)
    blk = 512
    tm = min(256, s)
    return ts, tq, tkc, blk, tm


def _ceil_to(x, m):
    return (x + m - 1) // m * m


def _trunk(x, c, prep, tiles=None):
    w_ada, b_ada, pre_w, post_w, exp_w, g_final = prep
    b, s, d = x.shape
    ts, tq, tkc, blk, tm = tiles or _tiles(s)
    t = b * s
    nt = t // tm
    mod = _modulation(c, w_ada, b_ada)
    q, k, v, gm = _pre_call(x, mod, _rope_tables(s), pre_w, ts)
    attn = _attn_call(q, k, v, tq, tkc)
    ybase, h2p, e_arr, lrank, w_arr, cnt = _post_call(attn, gm, x, mod, post_w, ts, tm)

    i32 = jnp.int32
    cnt = cnt[:, :, 0].astype(i32).reshape(nt, GROUP_SIZE, N_GROUPS).transpose(0, 2, 1).reshape(nt, N_EXPERTS)
    cnt8 = _ceil_to(cnt, ROW_ALIGN)
    base8 = jnp.cumsum(cnt8, axis=0) - cnt8
    total8 = jnp.sum(cnt8, axis=0)
    padded = _ceil_to(total8 + EXPERT_SLACK, blk)
    pends = jnp.cumsum(padded).astype(i32)
    zfrom = ((pends - padded + total8) // blk * blk).astype(i32)
    dstbase = (pends - padded)[None, :] + base8
    toff8 = jnp.cumsum(cnt8, axis=1) - cnt8
    eids = jnp.arange(N_EXPERTS, dtype=i32)

    def chunk_table(nch, chunk, n_max):
        cend = jnp.cumsum(nch, axis=1)
        cidx = jnp.arange(n_max, dtype=i32)
        e_of_c = jnp.minimum(jnp.sum((cend[:, None, :] <= cidx[None, :, None]).astype(i32), axis=-1),
                             N_EXPERTS - 1)
        pick = lambda tbl: jnp.sum(jnp.where(e_of_c[..., None] == eids, tbl[:, None, :], 0), axis=-1)
        rel = lambda first_row: pick(first_row - (cend - nch) * chunk) + cidx[None, :] * chunk
        return rel, cend[:, -1]

    n_dmax = N_EXPERTS + TOP_K * tm // DISP_CHUNK
    nch_d = (cnt + DISP_CHUNK - 1) // DISP_CHUNK
    rel_d, n_dch = chunk_table(nch_d, DISP_CHUNK, n_dmax)
    n_rb = (jnp.sum(cnt8, axis=1) + DISP_CHUNK + PERM_ROWS - 1) // PERM_ROWS
    tail = lambda a, b2, n: jnp.concatenate([a[:, None], b2[:, None], jnp.zeros((nt, n - 2), i32)], axis=1)
    dinfo = jnp.stack([rel_d(toff8), rel_d(dstbase), tail(n_dch, n_rb, n_dmax)], axis=1).astype(i32)

    shift = dstbase % COMB_CHUNK
    nch_c = jnp.where(cnt > 0, (cnt + shift + COMB_CHUNK - 1) // COMB_CHUNK, 0)
    boff = (jnp.cumsum(nch_c, axis=1) - nch_c) * COMB_CHUNK
    n_cmax = _gather_rows(tm) // COMB_CHUNK
    rel_c, n_cch = chunk_table(nch_c, COMB_CHUNK, n_cmax)
    n_kt = (n_cch * COMB_CHUNK + COMB_KTILE - 1) // COMB_KTILE
    cinfo = jnp.stack([rel_c(dstbase - shift) // COMB_CHUNK, tail(n_cch, n_kt, n_cmax)], axis=1).astype(i32)

    onehot = e_arr[..., None] == eids
    lookup = lambda tbl: jnp.sum(jnp.where(onehot, tbl[:, None, None, :], 0), axis=-1)
    lpos = lookup(toff8) + lrank
    cpos = lookup(boff + shift) + lrank
    tok_major = lambda a: jnp.transpose(a, (0, 2, 1)).reshape(t, TOP_K)

    n_rows = _ceil_to(t * TOP_K + nt * N_EXPERTS * (ROW_ALIGN - 1) + N_EXPERTS * (EXPERT_SLACK + blk - 1), blk)
    n_blocks = n_rows // blk
    n_used = (pends[-1] // blk).astype(i32).reshape(1)
    block_start = jnp.arange(n_blocks, dtype=i32) * blk
    block_exp = jnp.minimum(jnp.sum((pends[None, :] <= block_start[:, None]).astype(i32), axis=1),
                            N_EXPERTS - 1)

    xs = _disp_call(pends, zfrom, dinfo, lpos.astype(i32), h2p, n_rows, tm, blk)
    ys = _exp_call(block_exp, n_used, xs, exp_w[0], exp_w[1], blk)
    ys3 = ys.reshape(n_rows // COMB_CHUNK, COMB_CHUNK, d)
    return _comb_call(cinfo, ybase, mod, tok_major(cpos).astype(i32), tok_major(w_arr), g_final, ys3, tm)


def kernel(x_prompt, x_sample, c_prompt, c_sample, w_ada, b_ada, g_norm1, w_in, g_q_lat, w_uq, g_kv_lat, w_ukv, g_gmlp_v, w_spatial, b_spatial, g_out_attn, g_out_gmlp, w_out, g_norm2, w_router, b_router, w_gate_e, w_up_e, w_down_e, w_gate_s, w_up_s, w_down_s, g_final):
    prep = _prep_weights(w_ada[0], b_ada[0], g_norm1[0], w_in[0], g_q_lat[0], w_uq[0], g_kv_lat[0],
                         w_ukv[0], g_gmlp_v[0], w_spatial[0], b_spatial[0], g_out_attn[0],
                         g_out_gmlp[0], w_out[0], g_norm2[0], w_router[0], b_router[0], w_gate_e[0],
                         w_up_e[0], w_down_e[0], w_gate_s[0], w_up_s[0], w_down_s[0], g_final)
    return (_trunk(x_prompt, c_prompt, prep), _trunk(x_sample, c_sample, prep))
```

```python
import functools
import math

import jax
import jax.numpy as jnp
from jax import lax
from jax.experimental import pallas as pl
from jax.experimental.pallas import tpu as pltpu

F32 = jnp.float32
BF16 = jnp.bfloat16

D_MODEL = 1024
N_HEADS = 8
QK_NOPE = 64
QK_ROPE = 32
V_DIM = 64
Q_LORA = 256
KV_LORA = 128
GMLP_W = 512
CHUNK = 128
N_EXPERTS = 64
TOP_K = 8
N_GROUPS = 8
TOPK_GROUPS = 4
GROUP_SIZE = N_EXPERTS // N_GROUPS
EXPERT_DIM = 256
SHARED_DIM = 256
ROUTED_SCALE = 2.5
ROPE_THETA = 10000.0
N_MOD = 6
EPS = 1e-6

LANES = 128
HEAD_PAD = 128
HALF_ROPE = QK_ROPE // 2
VMEM_LIMIT = 52 * 1024 * 1024
ROW_ALIGN = 8
DISP_CHUNK = 32
COMB_CHUNK = 16
ATTN_CHUNKS_PER_STEP = 8
PERM_ROWS = 512
COMB_KTILE = 1024
EXPERT_SLACK = max(DISP_CHUNK, 2 * (COMB_CHUNK - 1))

SOFTMAX_SCALE = (QK_NOPE + QK_ROPE) ** -0.5
EXP2_SCALE = SOFTMAX_SCALE * math.log2(math.e)


def _rms(x, g):
    return x * lax.rsqrt(jnp.mean(x * x, axis=-1, keepdims=True) + EPS) * g


def _sigmoid(x):
    return 1.0 / (1.0 + jnp.exp(-x))


def _gelu_tanh(x):
    c = math.sqrt(2.0 / math.pi)
    return 0.5 * x * (1.0 + jnp.tanh(c * (x + 0.044715 * (x * x * x))))


def _dot(a, b):
    return jnp.dot(a, b, preferred_element_type=F32)


def _dot_nt(a, b):
    return lax.dot_general(a, b, (((1,), (1,)), ((), ())), preferred_element_type=F32)


def _pack_bf16_pairs(x):
    c = x.shape[1] // 2
    lo = pltpu.bitcast(x[:, :c].astype(BF16).astype(F32), jnp.uint32)
    hi = pltpu.bitcast(x[:, c:].astype(BF16).astype(F32), jnp.uint32)
    return (hi & jnp.uint32(0xFFFF0000)) | (lo >> 16)


def _unpack_bf16_pairs(w):
    lo = pltpu.bitcast(w << 16, F32)
    hi = pltpu.bitcast(w & jnp.uint32(0xFFFF0000), F32)
    return lo, hi


def _mod_kernel(c_ref, w_ref, b_ref, o_ref):
    c = c_ref[...]
    a = (c * _sigmoid(c)).astype(BF16)
    o_ref[...] = _dot(a, w_ref[...]) + b_ref[...]


def _modulation(c, w_ada_bf, b_ada):
    b = c.shape[0]
    bp = max(16, -(-b // 16) * 16)
    cp = jnp.pad(c, ((0, bp - b), (0, 0)))
    n = w_ada_bf.shape[1]
    tn = D_MODEL
    out = pl.pallas_call(
        _mod_kernel,
        name="mod",
        grid=(n // tn,),
        in_specs=[
            pl.BlockSpec((bp, D_MODEL), lambda j: (0, 0)),
            pl.BlockSpec((D_MODEL, tn), lambda j: (0, j)),
            pl.BlockSpec((1, tn), lambda j: (0, j)),
        ],
        out_specs=pl.BlockSpec((bp, tn), lambda j: (0, j)),
        out_shape=jax.ShapeDtypeStruct((bp, n), F32),
        compiler_params=pltpu.CompilerParams(dimension_semantics=("arbitrary",)),
    )(cp, w_ada_bf, b_ada.reshape(1, n))
    return out[:b].reshape(b, N_MOD, D_MODEL)


def _rope(xh, c, s1, s2):
    return (xh * c + pltpu.roll(xh, LANES - HALF_ROPE, axis=1) * s1
            + pltpu.roll(xh, HALF_ROPE, axis=1) * s2)


def _pre_kernel(x_ref, mod_ref, cos_ref, s1_ref, s2_ref, g1_ref, win_ref, gq_ref, wuq_ref,
                gkv_ref, wuk_ref, wuv_ref, vone_ref, ggv_ref, ws_ref, bs_ref, ggo_ref,
                q_ref, k_ref, v_ref, gm_ref, mix_ref):
    ts = x_ref.shape[1]
    x = x_ref[0]
    mod = mod_ref[0]
    h = _rms(x, g1_ref[...]) * (1.0 + mod[1:2]) + mod[0:1]
    z = _dot(h.astype(BF16), win_ref[...])
    o_kv = Q_LORA
    o_gu = o_kv + KV_LORA
    o_gv = o_gu + GMLP_W
    o_kr = o_gv + GMLP_W
    q_lat = z[:, :o_kv]
    kv_lat = z[:, o_kv:o_gu]
    g_u = z[:, o_gu:o_gv]
    g_v = z[:, o_gv:o_kr]
    kr = z[:, o_kr:o_kr + LANES]

    cos = cos_ref[...]
    s1 = s1_ref[...]
    s2 = s2_ref[...]

    qn = _rms(q_lat, gq_ref[...]).astype(BF16)
    q = _dot(qn, wuq_ref[...])
    kn = _rms(kv_lat, gkv_ref[...]).astype(BF16)
    kf = _dot(kn, wuk_ref[...])
    v_ref[0] = (_dot(kn, wuv_ref[...]) + vone_ref[...]).astype(BF16)
    krr = _rope(kr, cos, s1, s2)
    for hd in range(N_HEADS):
        sl = slice(hd * HEAD_PAD, (hd + 1) * HEAD_PAD)
        q_ref[0, :, sl] = (_rope(q[:, sl], cos, s1, s2) * EXP2_SCALE).astype(BF16)
        k_ref[0, :, sl] = (kf[:, sl] + krr).astype(BF16)

    u = _gelu_tanh(g_u)
    vn = _rms(_gelu_tanh(g_v), ggv_ref[...]).astype(BF16)
    lane = lax.broadcasted_iota(jnp.int32, (CHUNK, LANES), 1)
    left = lane < (LANES // 2)
    zero = jnp.zeros((CHUNK, LANES), BF16)
    for n in range(ts // CHUNK):
        rs = slice(n * CHUNK, (n + 1) * CHUNK)
        for p in range(GMLP_W // LANES):
            cs = slice(p * LANES, (p + 1) * LANES)
            vp = vn[rs, cs]
            rhs = jnp.concatenate([jnp.where(left, vp, zero), jnp.where(left, zero, vp)], axis=0)
            mix_ref[rs, cs] = _dot(ws_ref[p], rhs) + bs_ref[:, cs]
    gm = u * mix_ref[...]
    gm_ref[0] = _rms(gm, ggo_ref[...]).astype(BF16)


def _pre_call(x, mod, tabs, wts, ts):
    b, s, d = x.shape
    cos, s1, s2 = tabs
    (g1, win, gq, wuq, gkv, wuk, wuv, vone, ggv, wsp, bsp, ggo) = wts
    full = lambda a: pl.BlockSpec(a.shape, lambda bi, i: (0,) * a.ndim)
    tab = pl.BlockSpec((ts, LANES), lambda bi, i: (i, 0))
    out_shapes = (
        jax.ShapeDtypeStruct((b, s, N_HEADS * HEAD_PAD), BF16),
        jax.ShapeDtypeStruct((b, s, N_HEADS * HEAD_PAD), BF16),
        jax.ShapeDtypeStruct((b, s, N_HEADS * HEAD_PAD), BF16),
        jax.ShapeDtypeStruct((b, s, GMLP_W), BF16),
    )
    tok = lambda w: pl.BlockSpec((1, ts, w), lambda bi, i: (bi, i, 0))
    return pl.pallas_call(
        _pre_kernel,
        name="pre",
        grid=(b, s // ts),
        in_specs=[tok(d), pl.BlockSpec((1, N_MOD, d), lambda bi, i: (bi, 0, 0)), tab, tab, tab,
                  full(g1), full(win), full(gq), full(wuq), full(gkv), full(wuk), full(wuv), full(vone),
                  full(ggv), full(wsp), full(bsp), full(ggo)],
        out_specs=(tok(N_HEADS * HEAD_PAD), tok(N_HEADS * HEAD_PAD), tok(N_HEADS * HEAD_PAD), tok(GMLP_W)),
        out_shape=out_shapes,
        scratch_shapes=[pltpu.VMEM((ts, GMLP_W), F32)],
        compiler_params=pltpu.CompilerParams(dimension_semantics=("arbitrary", "arbitrary"),
                                             vmem_limit_bytes=VMEM_LIMIT),
    )(x, mod, cos, s1, s2, g1, win, gq, wuq, gkv, wuk, wuv, vone, ggv, wsp, bsp, ggo)


def _attn_kernel(q_ref, k_ref, v_ref, o_ref, *, tkc, cpb):
    tq = q_ref.shape[1]
    n_chunks = k_ref.shape[1] // tkc
    hps = q_ref.shape[2] // HEAD_PAD
    heads = [slice(hh * HEAD_PAD, (hh + 1) * HEAD_PAD) for hh in range(hps)]
    qs = [q_ref[0, :, hs] for hs in heads]

    def step(i, carry):
        carry = list(carry)
        for cc in range(cpb):
            off = pl.multiple_of((i * cpb + cc) * tkc, tkc)
            for hh, hs in enumerate(heads):
                m, acc = carry[2 * hh], carry[2 * hh + 1]
                kc = k_ref[0, pl.ds(off, tkc), hs]
                vc = v_ref[0, pl.ds(off, tkc), hs]
                sc = _dot_nt(qs[hh], kc)
                m_new = jnp.maximum(m, jnp.max(sc, axis=-1, keepdims=True))
                p = jnp.exp2(sc - m_new).astype(BF16)
                carry[2 * hh + 1] = jnp.exp2(m - m_new) * acc + _dot(p, vc)
                carry[2 * hh] = m_new
        return tuple(carry)

    n_steps = n_chunks // cpb
    init = (jnp.full((tq, 1), -jnp.inf, F32), jnp.zeros((tq, HEAD_PAD), F32)) * hps
    res = step(0, init) if n_steps == 1 else lax.fori_loop(0, n_steps, step, init)
    lane = lax.broadcasted_iota(jnp.int32, (tq, HEAD_PAD), 1)
    for pair in range(hps // 2):
        a0, a1 = res[4 * pair + 1], res[4 * pair + 3]
        o0 = a0 / a0[:, V_DIM:V_DIM + 1]
        o1 = a1 / a1[:, V_DIM:V_DIM + 1]
        o_ref[0, :, pair * HEAD_PAD:(pair + 1) * HEAD_PAD] = jnp.where(
            lane < V_DIM, o0, pltpu.roll(o1, V_DIM, axis=1)).astype(BF16)


def _attn_call(q, k, v, tq, tkc, hps):
    b, s, _ = q.shape
    resident = lambda: pl.BlockSpec((1, s, hps * HEAD_PAD), lambda bi, h, i: (bi, 0, h),
                                    pipeline_mode=pl.Buffered(1))
    return pl.pallas_call(
        functools.partial(_attn_kernel, tkc=tkc, cpb=math.gcd(s // tkc, ATTN_CHUNKS_PER_STEP)),
        name="attn",
        grid=(b, N_HEADS // hps, s // tq),
        in_specs=[pl.BlockSpec((1, tq, hps * HEAD_PAD), lambda bi, h, i: (bi, i, h)),
                  resident(), resident()],
        out_specs=pl.BlockSpec((1, tq, hps * V_DIM), lambda bi, h, i: (bi, i, h)),
        out_shape=jax.ShapeDtypeStruct((b, s, N_HEADS * V_DIM), BF16),
        compiler_params=pltpu.CompilerParams(
            dimension_semantics=("arbitrary", "arbitrary", "arbitrary"),
            vmem_limit_bytes=VMEM_LIMIT),
    )(q, k, v)


def _post_kernel(attn_ref, gm_ref, x_ref, mod_ref, goa_ref, woa_ref, wog_ref, g2_ref, wr_ref,
                 br_ref, wgus_ref, wds_ref,
                 ybase_ref, h2p_ref, e_ref, rank_ref, w_ref, cnt_ref):
    ts = x_ref.shape[1]
    mod = mod_ref[0]
    ga1, sh2, sc2, ga2 = mod[2:3], mod[3:4], mod[4:5], mod[5:6]
    an = _rms(attn_ref[0].astype(F32), goa_ref[...]).astype(BF16)
    y = _dot(an, woa_ref[...]) + _dot(gm_ref[0], wog_ref[...])
    x1 = x_ref[0] + ga1 * y
    h2 = _rms(x1, g2_ref[...]) * (1.0 + sc2) + sh2
    h2b = h2.astype(BF16)
    h2p_ref[...] = _pack_bf16_pairs(h2)

    gu = _dot(h2b, wgus_ref[...])
    g, u = gu[:, :SHARED_DIM], gu[:, SHARED_DIM:]
    a = (g * _sigmoid(g) * u).astype(BF16)
    ybase_ref[0] = x1 + ga2 * _dot(a, wds_ref[...])

    logits = _dot_nt(wr_ref[...], h2b)
    scores = _sigmoid(logits)
    biased = scores + br_ref[...]
    ninf = jnp.float32(-jnp.inf)
    bj = [biased[j * N_GROUPS:(j + 1) * N_GROUPS] for j in range(GROUP_SIZE)]
    sj = [scores[j * N_GROUPS:(j + 1) * N_GROUPS] for j in range(GROUP_SIZE)]
    m1 = bj[0]
    for j in range(1, GROUP_SIZE):
        m1 = jnp.maximum(m1, bj[j])
    found = jnp.zeros_like(m1)
    m2 = jnp.full_like(m1, ninf)
    for j in range(GROUP_SIZE):
        eq = jnp.where(bj[j] == m1, 1.0, 0.0)
        is_first = eq * (1.0 - found)
        found = jnp.maximum(found, eq)
        m2 = jnp.maximum(m2, jnp.where(is_first > 0.0, ninf, bj[j]))
    gs = m1 + m2
    gidx = lax.broadcasted_iota(jnp.int32, gs.shape, 0)
    grank = jnp.zeros_like(gs)
    for kk in range(1, N_GROUPS):
        r = pltpu.roll(gs, kk, axis=0)
        grank = grank + jnp.where(gidx >= kk, jnp.where(r >= gs, 1.0, 0.0), jnp.where(r > gs, 1.0, 0.0))
    gsel = grank < float(TOPK_GROUPS)
    masked = [jnp.where(gsel, bj[j], ninf) for j in range(GROUP_SIZE)]
    eidx = [gidx * GROUP_SIZE + j for j in range(GROUP_SIZE)]

    selm = [jnp.zeros_like(gs) for _ in range(GROUP_SIZE)]
    e_sel = []
    for _k in range(TOP_K):
        m = masked[0]
        for j in range(1, GROUP_SIZE):
            m = jnp.maximum(m, masked[j])
        m = jnp.max(m, axis=0, keepdims=True)
        cand = jnp.where(masked[0] == m, eidx[0], N_EXPERTS)
        for j in range(1, GROUP_SIZE):
            cand = jnp.minimum(cand, jnp.where(masked[j] == m, eidx[j], N_EXPERTS))
        emin = jnp.min(cand, axis=0, keepdims=True)
        e_sel.append(emin)
        for j in range(GROUP_SIZE):
            hit = eidx[j] == emin
            selm[j] = jnp.where(hit, 1.0, selm[j])
            masked[j] = jnp.where(hit, ninf, masked[j])

    wsel = [selm[j] * sj[j] for j in range(GROUP_SIZE)]
    tot = wsel[0]
    for j in range(1, GROUP_SIZE):
        tot = tot + wsel[j]
    tot = jnp.sum(tot, axis=0, keepdims=True)
    wn = [wsel[j] / tot * ROUTED_SCALE for j in range(GROUP_SIZE)]

    tm = e_ref.shape[2]
    sel = jnp.concatenate(selm, axis=0)
    tr = lax.broadcasted_iota(jnp.int32, (ts, ts), 0)
    tc = lax.broadcasted_iota(jnp.int32, (ts, ts), 1)
    sh = tm.bit_length() - 1
    same_tile = lax.shift_right_logical(tr, sh) == lax.shift_right_logical(tc, sh)
    upper = jnp.where(tr < tc, jnp.where(same_tile, 1.0, 0.0), 0.0).astype(BF16)
    rank_full = _dot(sel.astype(BF16), upper)
    rj = [rank_full[j * N_GROUPS:(j + 1) * N_GROUPS] for j in range(GROUP_SIZE)]
    subs = [slice(i * tm, (i + 1) * tm) for i in range(ts // tm)]
    for i, sub in enumerate(subs):
        cnt_ref[i] = jnp.broadcast_to(jnp.sum(sel[:, sub], axis=1, keepdims=True), (N_EXPERTS, LANES))

    for k in range(TOP_K):
        rk = jnp.zeros_like(gs)
        wk = jnp.zeros_like(gs)
        for j in range(GROUP_SIZE):
            hit = eidx[j] == e_sel[k]
            rk = rk + jnp.where(hit, rj[j], 0.0)
            wk = wk + jnp.where(hit, wn[j], 0.0)
        rk = jnp.sum(rk, axis=0, keepdims=True).astype(jnp.int32)
        wk = jnp.sum(wk, axis=0, keepdims=True)
        for i, sub in enumerate(subs):
            e_ref[i, k:k + 1, :] = e_sel[k][:, sub]
            rank_ref[i, k:k + 1, :] = rk[:, sub]
            w_ref[i, k:k + 1, :] = wk[:, sub]


def _post_call(attn, gm, x, mod, wts, ts, tm):
    b, s, d = x.shape
    nt = s // ts
    nsub = ts // tm
    ntm = b * nt * nsub
    (goa, woa, wog, g2, wr, br, wgus, wds) = wts
    full = lambda a: pl.BlockSpec(a.shape, lambda bi, i: (0,) * a.ndim)
    tok = lambda w: pl.BlockSpec((1, ts, w), lambda bi, i: (bi, i, 0))
    rout = pl.BlockSpec((nsub, TOP_K, tm), lambda bi, i: (bi * nt + i, 0, 0))
    out_shapes = (
        jax.ShapeDtypeStruct((b, s, d), F32),
        jax.ShapeDtypeStruct((b * s, d // 2), jnp.uint32),
        jax.ShapeDtypeStruct((ntm, TOP_K, tm), jnp.int32),
        jax.ShapeDtypeStruct((ntm, TOP_K, tm), jnp.int32),
        jax.ShapeDtypeStruct((ntm, TOP_K, tm), F32),
        jax.ShapeDtypeStruct((ntm, N_EXPERTS, LANES), F32),
    )
    return pl.pallas_call(
        _post_kernel,
        name="post",
        grid=(b, nt),
        in_specs=[tok(N_HEADS * V_DIM), tok(GMLP_W), tok(d),
                  pl.BlockSpec((1, N_MOD, d), lambda bi, i: (bi, 0, 0)),
                  full(goa), full(woa), full(wog), full(g2), full(wr), full(br), full(wgus), full(wds)],
        out_specs=(tok(d), pl.BlockSpec((ts, d // 2), lambda bi, i: (bi * nt + i, 0)),
                   rout, rout, rout,
                   pl.BlockSpec((nsub, N_EXPERTS, LANES), lambda bi, i: (bi * nt + i, 0, 0))),
        out_shape=out_shapes,
        compiler_params=pltpu.CompilerParams(dimension_semantics=("arbitrary", "arbitrary"),
                                             vmem_limit_bytes=VMEM_LIMIT),
    )(attn, gm, x, mod, goa, woa, wog, g2, wr, br, wgus, wds)


def _disp_kernel(pends_ref, zfrom_ref, info_ref, lpos_ref, h_ref, xs_ref, sbuf, zbuf, nprev, sems, zsem):
    td = h_ref.shape[0]
    blk = zbuf.shape[0]
    step = pl.program_id(0)
    slot = lax.rem(step, 2)

    @pl.when(step == 0)
    def _():
        nprev[0] = 0
        zbuf[...] = jnp.zeros_like(zbuf)

        def tail_copy(e, j):
            start = pl.multiple_of(zfrom_ref[e] + j * blk, blk)
            return pltpu.make_async_copy(zbuf, xs_ref.at[pl.ds(start, blk), :], zsem)

        def n_tail(e):
            return (pends_ref[e] - zfrom_ref[e]) // blk

        def zstart(e, c):
            def one(j, c2):
                tail_copy(e, j).start()
                return c2
            return lax.fori_loop(0, n_tail(e), one, c)

        def zwait(e, c):
            def one(j, c2):
                tail_copy(e, j).wait()
                return c2
            return lax.fori_loop(0, n_tail(e), one, c)

        lax.fori_loop(0, N_EXPERTS, zstart, 0)
        lax.fori_loop(0, N_EXPERTS, zwait, 0)

    lo, hi = _unpack_bf16_pairs(h_ref[...])
    lo = lo.astype(BF16)
    hi = hi.astype(BF16)
    lpos = lpos_ref[0]
    lpos_b = [jnp.broadcast_to(lpos[k:k + 1, :], (PERM_ROWS, td)).astype(jnp.int16) for k in range(TOP_K)]
    row0 = lax.broadcasted_iota(jnp.int32, (PERM_ROWS, td), 0)
    one = jnp.ones((PERM_ROWS, td), BF16)

    def build(rb, c):
        r0 = pl.multiple_of(rb * PERM_ROWS, PERM_ROWS)
        riota = (row0 + r0).astype(jnp.int16)
        pb = jnp.zeros((PERM_ROWS, td), BF16)
        for k in range(TOP_K):
            pb = jnp.where(lpos_b[k] == riota, one, pb)
        xlo = pltpu.bitcast(_dot(pb, lo), jnp.uint32)
        xhi = pltpu.bitcast(_dot(pb, hi), jnp.uint32)
        sbuf[slot, pl.ds(r0, PERM_ROWS), :] = (xhi & jnp.uint32(0xFFFF0000)) | (xlo >> 16)
        return c

    lax.fori_loop(0, info_ref[0, 2, 1], build, 0)

    def chunk_copy(src, dst, sl):
        return pltpu.make_async_copy(
            sbuf.at[sl, pl.ds(pl.multiple_of(src, ROW_ALIGN), DISP_CHUNK), :],
            xs_ref.at[pl.ds(pl.multiple_of(dst, ROW_ALIGN), DISP_CHUNK), :], sems.at[sl])

    def drain(n, sl):
        def one(c, carry):
            chunk_copy(0, 0, sl).wait()
            return carry
        lax.fori_loop(0, n, one, 0)

    def issue(c, carry):
        chunk_copy(info_ref[0, 0, c], info_ref[0, 1, c], slot).start()
        return carry

    drain(nprev[0], 1 - slot)
    n_chunks = info_ref[0, 2, 0]
    lax.fori_loop(0, n_chunks, issue, 0)
    nprev[0] = n_chunks

    @pl.when(step == pl.num_programs(0) - 1)
    def _():
        drain(n_chunks, slot)


def _sorted_rows(td):
    need = TOP_K * td + N_EXPERTS * (ROW_ALIGN - 1) + DISP_CHUNK
    return -(-need // PERM_ROWS) * PERM_ROWS


def _disp_call(pends, zfrom, info, lpos, h2p, n_rows, td, blk):
    t, w = h2p.shape
    nt = t // td
    return pl.pallas_call(
        _disp_kernel,
        name="disp",
        grid_spec=pltpu.PrefetchScalarGridSpec(
            num_scalar_prefetch=2,
            grid=(nt,),
            in_specs=[pl.BlockSpec((1,) + info.shape[1:], lambda i, pe, pa: (i, 0, 0), memory_space=pltpu.SMEM),
                      pl.BlockSpec((1, TOP_K, td), lambda i, pe, pa: (i, 0, 0)),
                      pl.BlockSpec((td, w), lambda i, pe, pa: (i, 0))],
            out_specs=pl.BlockSpec(memory_space=pl.ANY),
            scratch_shapes=[pltpu.VMEM((2, _sorted_rows(td), w), jnp.uint32),
                            pltpu.VMEM((blk, w), jnp.uint32),
                            pltpu.SMEM((1,), jnp.int32),
                            pltpu.SemaphoreType.DMA((2,)), pltpu.SemaphoreType.DMA(())],
        ),
        out_shape=jax.ShapeDtypeStruct((n_rows, w), jnp.uint32),
        compiler_params=pltpu.CompilerParams(dimension_semantics=("arbitrary",),
                                             vmem_limit_bytes=VMEM_LIMIT),
    )(pends, zfrom, info, lpos, h2p)


def _exp_kernel(bexp_ref, nused_ref, xs_ref, wgu_ref, wd_ref, ys_ref):
    i = pl.program_id(0)

    @pl.when(i < nused_ref[0])
    def _():
        half = D_MODEL // 2
        lo, hi = _unpack_bf16_pairs(xs_ref[...])
        gu = (_dot(lo.astype(BF16), wgu_ref[0, :half, :])
              + _dot(hi.astype(BF16), wgu_ref[0, half:, :]))
        g, u = gu[:, :EXPERT_DIM], gu[:, EXPERT_DIM:]
        a = (g * _sigmoid(g) * u).astype(BF16)
        ys_ref[...] = _dot(a, wd_ref[0]).astype(BF16)


def _exp_call(block_exp, n_used, xs, wgu, wd, blk):
    n_rows, w = xs.shape
    n_blocks = n_rows // blk

    def row_map(i, bexp, nused):
        return (jnp.minimum(i, nused[0] - 1), 0)

    def w_map(i, bexp, nused):
        return (bexp[jnp.minimum(i, nused[0] - 1)], 0, 0)

    return pl.pallas_call(
        _exp_kernel,
        name="exp",
        grid_spec=pltpu.PrefetchScalarGridSpec(
            num_scalar_prefetch=2,
            grid=(n_blocks,),
            in_specs=[pl.BlockSpec((blk, w), row_map),
                      pl.BlockSpec((1, D_MODEL, 2 * EXPERT_DIM), w_map),
                      pl.BlockSpec((1, EXPERT_DIM, D_MODEL), w_map)],
            out_specs=pl.BlockSpec((blk, D_MODEL), row_map),
        ),
        out_shape=jax.ShapeDtypeStruct((n_rows, D_MODEL), BF16),
        compiler_params=pltpu.CompilerParams(dimension_semantics=("arbitrary",),
                                             vmem_limit_bytes=VMEM_LIMIT),
    )(block_exp, n_used, xs, wgu, wd)


def _comb_kernel(info_ref, next_info_ref, ybase_ref, mod_ref, cpos_ref, w_ref, gf_ref, ys_ref, o_ref,
                 ybuf, acc, sems):
    tc = ybase_ref.shape[1]
    step = pl.program_id(0) * pl.num_programs(1) + pl.program_id(1)
    n_steps = pl.num_programs(0) * pl.num_programs(1)
    slot = lax.rem(step, 2)

    def chunk_copy(src, dst, sl, n_tiles):
        return pltpu.make_async_copy(ys_ref.at[pl.ds(src, n_tiles)], ybuf.at[sl, pl.ds(dst, n_tiles)],
                                     sems.at[sl])

    def fetch(iref, sl):
        for row, n_tiles in ((0, 2), (2, 1)):
            def issue(c, carry):
                chunk_copy(iref[0, row, c], iref[0, row + 1, c], sl, n_tiles).start()
                return carry
            lax.fori_loop(0, iref[0, 4, row // 2], issue, 0)

    @pl.when(step == 0)
    def _():
        ybuf[...] = jnp.zeros_like(ybuf)
        fetch(info_ref, 0)

    @pl.when(step + 1 < n_steps)
    def _():
        fetch(next_info_ref, 1 - slot)

    for row, n_tiles in ((0, 2), (2, 1)):
        def drain(c, carry):
            chunk_copy(0, 0, slot, n_tiles).wait()
            return carry
        lax.fori_loop(0, info_ref[0, 4, row // 2], drain, 0)

    cpos = cpos_ref[...]
    w = w_ref[...]
    cpos_b = [jnp.broadcast_to(cpos[:, k:k + 1], (tc, LANES)).astype(jnp.int16) for k in range(TOP_K)]
    w_b = [jnp.broadcast_to(w[:, k:k + 1], (tc, LANES)).astype(BF16) for k in range(TOP_K)]
    lane = lax.broadcasted_iota(jnp.int32, (tc, LANES), 1)
    acc[...] = jnp.zeros_like(acc)

    def slab(kt, c):
        k0 = pl.multiple_of(kt * COMB_KTILE, COMB_KTILE)
        cols = []
        for j in range(COMB_KTILE // LANES):
            col = (lane + (k0 + j * LANES)).astype(jnp.int16)
            wm = jnp.zeros((tc, LANES), BF16)
            for k in range(TOP_K):
                wm = jnp.where(cpos_b[k] == col, w_b[k], wm)
            cols.append(wm)
        rows = ybuf[slot, pl.ds(kt * (COMB_KTILE // COMB_CHUNK), COMB_KTILE // COMB_CHUNK)]
        acc[...] += _dot(jnp.concatenate(cols, axis=1), rows.reshape(COMB_KTILE, rows.shape[-1]))
        return c

    lax.fori_loop(0, info_ref[0, 4, 2], slab, 0)
    ga2 = mod_ref[0][5:6]
    o_ref[0] = _rms(ybase_ref[0] + ga2 * acc[...], gf_ref[...])


def _gather_rows(tc):
    need = TOP_K * tc + N_EXPERTS * 2 * (COMB_CHUNK - 1)
    return -(-need // COMB_KTILE) * COMB_KTILE


def _comb_call(info, ybase, mod, cpos_tok, w_tok, g_final, ys, tc):
    b, s, d = ybase.shape
    nt = s // tc
    return pl.pallas_call(
        _comb_kernel,
        name="comb",
        grid=(b, nt),
        in_specs=[pl.BlockSpec((1,) + info.shape[1:], lambda bi, i: (bi * nt + i, 0, 0), memory_space=pltpu.SMEM),
                  pl.BlockSpec((1,) + info.shape[1:], lambda bi, i: (jnp.minimum(bi * nt + i + 1, b * nt - 1), 0, 0),
                               memory_space=pltpu.SMEM),
                  pl.BlockSpec((1, tc, d), lambda bi, i: (bi, i, 0)),
                  pl.BlockSpec((1, N_MOD, d), lambda bi, i: (bi, 0, 0)),
                  pl.BlockSpec((tc, TOP_K), lambda bi, i: (bi * nt + i, 0)),
                  pl.BlockSpec((tc, TOP_K), lambda bi, i: (bi * nt + i, 0)),
                  pl.BlockSpec((1, d), lambda bi, i: (0, 0)),
                  pl.BlockSpec(memory_space=pl.ANY)],
        out_specs=pl.BlockSpec((1, tc, d), lambda bi, i: (bi, i, 0)),
        out_shape=jax.ShapeDtypeStruct((b, s, d), F32),
        scratch_shapes=[pltpu.VMEM((2, _gather_rows(tc) // COMB_CHUNK, COMB_CHUNK, d), BF16),
                        pltpu.VMEM((tc, d), F32), pltpu.SemaphoreType.DMA((2,))],
        compiler_params=pltpu.CompilerParams(dimension_semantics=("arbitrary", "arbitrary"),
                                             vmem_limit_bytes=VMEM_LIMIT),
    )(info, info, ybase, mod, cpos_tok, w_tok, g_final, ys)


def _prep_weights(w_ada, b_ada, g_norm1, w_in, g_q_lat, w_uq, g_kv_lat, w_ukv, g_gmlp_v, w_spatial,
                  b_spatial, g_out_attn, g_out_gmlp, w_out, g_norm2, w_router, b_router, w_gate_e,
                  w_up_e, w_down_e, w_gate_s, w_up_s, w_down_s, g_final):
    row = lambda g: g.reshape(1, -1).astype(F32)
    o1 = Q_LORA
    o2 = o1 + KV_LORA
    o3 = o2 + QK_ROPE
    o4 = o3 + GMLP_W
    kr_cols = jnp.pad(w_in[:, o2:o3], ((0, 0), (QK_NOPE, LANES - QK_NOPE - QK_ROPE)))
    win = jnp.concatenate([w_in[:, :o2], w_in[:, o3:o4], w_in[:, o4:], kr_cols], axis=1).astype(BF16)
    qd = QK_NOPE + QK_ROPE
    wuq = jnp.pad(w_uq.reshape(Q_LORA, N_HEADS, qd), ((0, 0), (0, 0), (0, HEAD_PAD - qd)))
    wuq = wuq.reshape(Q_LORA, N_HEADS * HEAD_PAD).astype(BF16)
    wkv = w_ukv.reshape(KV_LORA, N_HEADS, QK_NOPE + V_DIM)
    wuk = jnp.pad(wkv[:, :, :QK_NOPE], ((0, 0), (0, 0), (0, HEAD_PAD - QK_NOPE)))
    wuk = wuk.reshape(KV_LORA, N_HEADS * HEAD_PAD).astype(BF16)
    wuv = jnp.pad(wkv[:, :, QK_NOPE:], ((0, 0), (0, 0), (0, HEAD_PAD - V_DIM)))
    wuv = wuv.reshape(KV_LORA, N_HEADS * HEAD_PAD).astype(BF16)
    vone = jnp.tile((jnp.arange(HEAD_PAD) == V_DIM).astype(F32), N_HEADS).reshape(1, -1)
    wsp = w_spatial.reshape(N_HEADS // 2, 2, CHUNK, CHUNK).transpose(0, 2, 1, 3)
    wsp = wsp.reshape(N_HEADS // 2, CHUNK, 2 * CHUNK).astype(BF16)
    bsp = jnp.repeat(jnp.transpose(b_spatial), GMLP_W // N_HEADS, axis=1).astype(F32)
    pre = (row(g_norm1), win, row(g_q_lat), wuq, row(g_kv_lat), wuk, wuv, vone, row(g_gmlp_v), wsp,
           bsp, row(g_out_gmlp))
    perm = (jnp.arange(N_GROUPS)[None, :] * GROUP_SIZE + jnp.arange(GROUP_SIZE)[:, None]).reshape(-1)
    wr = jnp.transpose(w_router)[perm].astype(BF16)
    br = b_router.astype(F32)[perm].reshape(N_EXPERTS, 1)
    mla_w = N_HEADS * V_DIM
    wgus = jnp.concatenate([w_gate_s, w_up_s], axis=1).astype(BF16)
    post = (row(g_out_attn), w_out[:mla_w].astype(BF16), w_out[mla_w:].astype(BF16), row(g_norm2),
            wr, br, wgus, w_down_s.astype(BF16))
    wgu_e = jnp.concatenate([w_gate_e, w_up_e], axis=2).astype(BF16)
    wd_e = w_down_e.astype(BF16)
    return w_ada.astype(BF16), b_ada, pre, post, (wgu_e, wd_e), row(g_final)


def _rope_tables(s):
    inv = 1.0 / (ROPE_THETA ** (jnp.arange(0, QK_ROPE, 2, dtype=F32) / QK_ROPE))
    ang = jnp.arange(s, dtype=F32)[:, None] * inv[None, :]
    cos, sin = jnp.cos(ang), jnp.sin(ang)
    z = lambda n: jnp.zeros((s, n), F32)
    tail = LANES - QK_NOPE - QK_ROPE
    c = jnp.concatenate([jnp.ones((s, QK_NOPE), F32), cos, cos, z(tail)], axis=1)
    s1 = jnp.concatenate([z(QK_NOPE), -sin, z(HALF_ROPE), z(tail)], axis=1)
    s2 = jnp.concatenate([z(QK_NOPE), z(HALF_ROPE), sin, z(tail)], axis=1)
    return c, s1, s2


def _tiles(s):
    ts = min(512, s)
    tq = min(512, s)
    tkc = min(1024, s)
    blk = 1024
    tm = min(256, s)
    hps = 4 if s * 4 * HEAD_PAD * 2 * 2 <= VMEM_LIMIT // 4 else 2
    return ts, tq, tkc, blk, tm, hps


def _ceil_to(x, m):
    return (x + m - 1) // m * m


def _trunk(x, c, prep, tiles=None):
    w_ada, b_ada, pre_w, post_w, exp_w, g_final = prep
    b, s, d = x.shape
    ts, tq, tkc, blk, tm, hps = tiles or _tiles(s)
    t = b * s
    nt = t // tm
    mod = _modulation(c, w_ada, b_ada)
    q, k, v, gm = _pre_call(x, mod, _rope_tables(s), pre_w, ts)
    attn = _attn_call(q, k, v, tq, tkc, hps)
    ybase, h2p, e_arr, lrank, w_arr, cnt = _post_call(attn, gm, x, mod, post_w, ts, tm)

    i32 = jnp.int32
    cnt = cnt[:, :, 0].astype(i32).reshape(nt, GROUP_SIZE, N_GROUPS).transpose(0, 2, 1).reshape(nt, N_EXPERTS)
    cnt8 = _ceil_to(cnt, ROW_ALIGN)
    base8 = jnp.cumsum(cnt8, axis=0) - cnt8
    total8 = jnp.sum(cnt8, axis=0)
    padded = _ceil_to(total8 + EXPERT_SLACK, blk)
    pends = jnp.cumsum(padded).astype(i32)
    zfrom = ((pends - padded + total8) // blk * blk).astype(i32)
    dstbase = (pends - padded)[None, :] + base8
    toff8 = jnp.cumsum(cnt8, axis=1) - cnt8
    eids = jnp.arange(N_EXPERTS, dtype=i32)

    def chunk_table(nch, chunk, n_max):
        cend = jnp.cumsum(nch, axis=1)
        cidx = jnp.arange(n_max, dtype=i32)
        e_of_c = jnp.minimum(jnp.sum((cend[:, None, :] <= cidx[None, :, None]).astype(i32), axis=-1),
                             N_EXPERTS - 1)
        pick = lambda tbl: jnp.sum(jnp.where(e_of_c[..., None] == eids, tbl[:, None, :], 0), axis=-1)
        rel = lambda first_row: pick(first_row - (cend - nch) * chunk) + cidx[None, :] * chunk
        return rel, cend[:, -1]

    n_dmax = N_EXPERTS + TOP_K * tm // DISP_CHUNK
    nch_d = (cnt + DISP_CHUNK - 1) // DISP_CHUNK
    rel_d, n_dch = chunk_table(nch_d, DISP_CHUNK, n_dmax)
    n_rb = (jnp.sum(cnt8, axis=1) + DISP_CHUNK + PERM_ROWS - 1) // PERM_ROWS
    tail = lambda a, b2, n: jnp.concatenate([a[:, None], b2[:, None], jnp.zeros((nt, n - 2), i32)], axis=1)
    dinfo = jnp.stack([rel_d(toff8), rel_d(dstbase), tail(n_dch, n_rb, n_dmax)], axis=1).astype(i32)

    shift = dstbase % COMB_CHUNK
    nch_c = jnp.where(cnt > 0, (cnt + shift + COMB_CHUNK - 1) // COMB_CHUNK, 0)
    boff = (jnp.cumsum(nch_c, axis=1) - nch_c) * COMB_CHUNK
    n_cmax = _gather_rows(tm) // COMB_CHUNK // 2
    src_t = (dstbase - shift) // COMB_CHUNK
    dst_t = boff // COMB_CHUNK
    rel2, n_two = chunk_table(nch_c // 2, 2, n_cmax)
    rel1, n_one = chunk_table(nch_c % 2, 1, n_cmax)
    odd = nch_c // 2 * 2
    n_kt = (jnp.sum(nch_c, axis=1) * COMB_CHUNK + COMB_KTILE - 1) // COMB_KTILE
    counts = jnp.concatenate([n_two[:, None], n_one[:, None], n_kt[:, None], jnp.zeros((nt, n_cmax - 3), i32)], axis=1)
    cinfo = jnp.stack([rel2(src_t), rel2(dst_t), rel1(src_t + odd), rel1(dst_t + odd), counts], axis=1).astype(i32)

    onehot = e_arr[..., None] == eids
    lookup = lambda tbl: jnp.sum(jnp.where(onehot, tbl[:, None, None, :], 0), axis=-1)
    lpos = lookup(toff8) + lrank
    cpos = lookup(boff + shift) + lrank
    tok_major = lambda a: jnp.transpose(a, (0, 2, 1)).reshape(t, TOP_K)

    n_rows = _ceil_to(t * TOP_K + nt * N_EXPERTS * (ROW_ALIGN - 1) + N_EXPERTS * (EXPERT_SLACK + blk - 1), blk)
    n_blocks = n_rows // blk
    n_used = (pends[-1] // blk).astype(i32).reshape(1)
    block_start = jnp.arange(n_blocks, dtype=i32) * blk
    block_exp = jnp.minimum(jnp.sum((pends[None, :] <= block_start[:, None]).astype(i32), axis=1),
                            N_EXPERTS - 1)

    xs = _disp_call(pends, zfrom, dinfo, lpos.astype(i32), h2p, n_rows, tm, blk)
    ys = _exp_call(block_exp, n_used, xs, exp_w[0], exp_w[1], blk)
    ys3 = ys.reshape(n_rows // COMB_CHUNK, COMB_CHUNK, d)
    return _comb_call(cinfo, ybase, mod, tok_major(cpos).astype(i32), tok_major(w_arr), g_final, ys3, tm)


def kernel(x_prompt, x_sample, c_prompt, c_sample, w_ada, b_ada, g_norm1, w_in, g_q_lat, w_uq, g_kv_lat, w_ukv, g_gmlp_v, w_spatial, b_spatial, g_out_attn, g_out_gmlp, w_out, g_norm2, w_router, b_router, w_gate_e, w_up_e, w_down_e, w_gate_s, w_up_s, w_down_s, g_final):
    prep = _prep_weights(w_ada[0], b_ada[0], g_norm1[0], w_in[0], g_q_lat[0], w_uq[0], g_kv_lat[0],
                         w_ukv[0], g_gmlp_v[0], w_spatial[0], b_spatial[0], g_out_attn[0],
                         g_out_gmlp[0], w_out[0], g_norm2[0], w_router[0], b_router[0], w_gate_e[0],
                         w_up_e[0], w_down_e[0], w_gate_s[0], w_up_s[0], w_down_s[0], g_final)
    return (_trunk(x_prompt, c_prompt, prep), _trunk(x_sample, c_sample, prep))
```

```python
import functools
import math

import jax
import jax.numpy as jnp
from jax import lax
from jax.experimental import pallas as pl
from jax.experimental.pallas import tpu as pltpu

F32 = jnp.float32
BF16 = jnp.bfloat16

D_MODEL = 1024
N_HEADS = 8
QK_NOPE = 64
QK_ROPE = 32
V_DIM = 64
Q_LORA = 256
KV_LORA = 128
GMLP_W = 512
CHUNK = 128
N_EXPERTS = 64
TOP_K = 8
N_GROUPS = 8
TOPK_GROUPS = 4
GROUP_SIZE = N_EXPERTS // N_GROUPS
EXPERT_DIM = 256
SHARED_DIM = 256
ROUTED_SCALE = 2.5
ROPE_THETA = 10000.0
N_MOD = 6
EPS = 1e-6

LANES = 128
HEAD_PAD = 128
HALF_ROPE = QK_ROPE // 2
VMEM_LIMIT = 52 * 1024 * 1024
ROW_ALIGN = 8
DISP_CHUNK = 48
COMB_CHUNK = 16
COMB_CHUNK_TILES = (3, 2, 1)
ATTN_CHUNKS_PER_STEP = 8
PERM_ROWS = 512
COMB_KTILE = 1024
EXPERT_SLACK = max(DISP_CHUNK, 2 * (COMB_CHUNK - 1))

SOFTMAX_SCALE = (QK_NOPE + QK_ROPE) ** -0.5
EXP2_SCALE = SOFTMAX_SCALE * math.log2(math.e)


def _rms(x, g):
    return x * lax.rsqrt(jnp.mean(x * x, axis=-1, keepdims=True) + EPS) * g


def _sigmoid(x):
    return 1.0 / (1.0 + jnp.exp(-x))


def _gelu_tanh(x):
    c = math.sqrt(2.0 / math.pi)
    return 0.5 * x * (1.0 + jnp.tanh(c * (x + 0.044715 * (x * x * x))))


def _dot(a, b):
    return jnp.dot(a, b, preferred_element_type=F32)


def _dot_nt(a, b):
    return lax.dot_general(a, b, (((1,), (1,)), ((), ())), preferred_element_type=F32)


def _pack_bf16_pairs(x):
    c = x.shape[1] // 2
    lo = pltpu.bitcast(x[:, :c].astype(BF16).astype(F32), jnp.uint32)
    hi = pltpu.bitcast(x[:, c:].astype(BF16).astype(F32), jnp.uint32)
    return (hi & jnp.uint32(0xFFFF0000)) | (lo >> 16)


def _unpack_bf16_pairs(w):
    lo = pltpu.bitcast(w << 16, F32)
    hi = pltpu.bitcast(w & jnp.uint32(0xFFFF0000), F32)
    return lo, hi


def _mod_kernel(c_ref, w_ref, b_ref, o_ref):
    c = c_ref[...]
    a = (c * _sigmoid(c)).astype(BF16)
    o_ref[...] = _dot(a, w_ref[...]) + b_ref[...]


def _modulation(c, w_ada_bf, b_ada):
    b = c.shape[0]
    bp = max(16, -(-b // 16) * 16)
    cp = jnp.pad(c, ((0, bp - b), (0, 0)))
    n = w_ada_bf.shape[1]
    tn = D_MODEL
    out = pl.pallas_call(
        _mod_kernel,
        name="mod",
        grid=(n // tn,),
        in_specs=[
            pl.BlockSpec((bp, D_MODEL), lambda j: (0, 0)),
            pl.BlockSpec((D_MODEL, tn), lambda j: (0, j)),
            pl.BlockSpec((1, tn), lambda j: (0, j)),
        ],
        out_specs=pl.BlockSpec((bp, tn), lambda j: (0, j)),
        out_shape=jax.ShapeDtypeStruct((bp, n), F32),
        compiler_params=pltpu.CompilerParams(dimension_semantics=("arbitrary",)),
    )(cp, w_ada_bf, b_ada.reshape(1, n))
    return out[:b].reshape(b, N_MOD, D_MODEL)


def _rope(xh, c, s1, s2):
    return (xh * c + pltpu.roll(xh, LANES - HALF_ROPE, axis=1) * s1
            + pltpu.roll(xh, HALF_ROPE, axis=1) * s2)


def _pre_kernel(x_ref, mod_ref, cos_ref, s1_ref, s2_ref, g1_ref, win_ref, gq_ref, wuq_ref,
                gkv_ref, wuk_ref, wuv_ref, vone_ref, ggv_ref, ws_ref, bs_ref, ggo_ref,
                q_ref, k_ref, v_ref, gm_ref, mix_ref):
    ts = x_ref.shape[1]
    x = x_ref[0]
    mod = mod_ref[0]
    h = _rms(x, g1_ref[...]) * (1.0 + mod[1:2]) + mod[0:1]
    z = _dot(h.astype(BF16), win_ref[...])
    o_kv = Q_LORA
    o_gu = o_kv + KV_LORA
    o_gv = o_gu + GMLP_W
    o_kr = o_gv + GMLP_W
    q_lat = z[:, :o_kv]
    kv_lat = z[:, o_kv:o_gu]
    g_u = z[:, o_gu:o_gv]
    g_v = z[:, o_gv:o_kr]
    kr = z[:, o_kr:o_kr + LANES]

    cos = cos_ref[...]
    s1 = s1_ref[...]
    s2 = s2_ref[...]

    qn = _rms(q_lat, gq_ref[...]).astype(BF16)
    q = _dot(qn, wuq_ref[...])
    kn = _rms(kv_lat, gkv_ref[...]).astype(BF16)
    kf = _dot(kn, wuk_ref[...])
    v_ref[0] = (_dot(kn, wuv_ref[...]) + vone_ref[...]).astype(BF16)
    krr = _rope(kr, cos, s1, s2)
    for hd in range(N_HEADS):
        sl = slice(hd * HEAD_PAD, (hd + 1) * HEAD_PAD)
        q_ref[0, :, sl] = (_rope(q[:, sl], cos, s1, s2) * EXP2_SCALE).astype(BF16)
        k_ref[0, :, sl] = (kf[:, sl] + krr).astype(BF16)

    u = _gelu_tanh(g_u)
    vn = _rms(_gelu_tanh(g_v), ggv_ref[...]).astype(BF16)
    lane = lax.broadcasted_iota(jnp.int32, (CHUNK, LANES), 1)
    left = lane < (LANES // 2)
    zero = jnp.zeros((CHUNK, LANES), BF16)
    for n in range(ts // CHUNK):
        rs = slice(n * CHUNK, (n + 1) * CHUNK)
        for p in range(GMLP_W // LANES):
            cs = slice(p * LANES, (p + 1) * LANES)
            vp = vn[rs, cs]
            rhs = jnp.concatenate([jnp.where(left, vp, zero), jnp.where(left, zero, vp)], axis=0)
            mix_ref[rs, cs] = _dot(ws_ref[p], rhs) + bs_ref[:, cs]
    gm = u * mix_ref[...]
    gm_ref[0] = _rms(gm, ggo_ref[...]).astype(BF16)


def _pre_call(x, mod, tabs, wts, ts):
    b, s, d = x.shape
    cos, s1, s2 = tabs
    (g1, win, gq, wuq, gkv, wuk, wuv, vone, ggv, wsp, bsp, ggo) = wts
    full = lambda a: pl.BlockSpec(a.shape, lambda bi, i: (0,) * a.ndim)
    tab = pl.BlockSpec((ts, LANES), lambda bi, i: (i, 0))
    out_shapes = (
        jax.ShapeDtypeStruct((b, s, N_HEADS * HEAD_PAD), BF16),
        jax.ShapeDtypeStruct((b, s, N_HEADS * HEAD_PAD), BF16),
        jax.ShapeDtypeStruct((b, s, N_HEADS * HEAD_PAD), BF16),
        jax.ShapeDtypeStruct((b, s, GMLP_W), BF16),
    )
    tok = lambda w: pl.BlockSpec((1, ts, w), lambda bi, i: (bi, i, 0))
    return pl.pallas_call(
        _pre_kernel,
        name="pre",
        grid=(b, s // ts),
        in_specs=[tok(d), pl.BlockSpec((1, N_MOD, d), lambda bi, i: (bi, 0, 0)), tab, tab, tab,
                  full(g1), full(win), full(gq), full(wuq), full(gkv), full(wuk), full(wuv), full(vone),
                  full(ggv), full(wsp), full(bsp), full(ggo)],
        out_specs=(tok(N_HEADS * HEAD_PAD), tok(N_HEADS * HEAD_PAD), tok(N_HEADS * HEAD_PAD), tok(GMLP_W)),
        out_shape=out_shapes,
        scratch_shapes=[pltpu.VMEM((ts, GMLP_W), F32)],
        compiler_params=pltpu.CompilerParams(dimension_semantics=("arbitrary", "arbitrary"),
                                             vmem_limit_bytes=VMEM_LIMIT),
    )(x, mod, cos, s1, s2, g1, win, gq, wuq, gkv, wuk, wuv, vone, ggv, wsp, bsp, ggo)


def _attn_kernel(q_ref, k_ref, v_ref, o_ref, *, tkc, cpb):
    tq = q_ref.shape[1]
    n_chunks = k_ref.shape[1] // tkc
    hps = q_ref.shape[2] // HEAD_PAD
    heads = [slice(hh * HEAD_PAD, (hh + 1) * HEAD_PAD) for hh in range(hps)]
    qs = [q_ref[0, :, hs] for hs in heads]

    def step(i, carry):
        carry = list(carry)
        for cc in range(cpb):
            off = pl.multiple_of((i * cpb + cc) * tkc, tkc)
            for hh, hs in enumerate(heads):
                m, acc = carry[2 * hh], carry[2 * hh + 1]
                kc = k_ref[0, pl.ds(off, tkc), hs]
                vc = v_ref[0, pl.ds(off, tkc), hs]
                sc = _dot_nt(qs[hh], kc)
                m_new = jnp.maximum(m, jnp.max(sc, axis=-1, keepdims=True))
                p = jnp.exp2(sc - m_new).astype(BF16)
                carry[2 * hh + 1] = jnp.exp2(m - m_new) * acc + _dot(p, vc)
                carry[2 * hh] = m_new
        return tuple(carry)

    n_steps = n_chunks // cpb
    init = (jnp.full((tq, 1), -jnp.inf, F32), jnp.zeros((tq, HEAD_PAD), F32)) * hps
    res = step(0, init) if n_steps == 1 else lax.fori_loop(0, n_steps, step, init)
    lane = lax.broadcasted_iota(jnp.int32, (tq, HEAD_PAD), 1)
    for pair in range(hps // 2):
        a0, a1 = res[4 * pair + 1], res[4 * pair + 3]
        o0 = a0 / a0[:, V_DIM:V_DIM + 1]
        o1 = a1 / a1[:, V_DIM:V_DIM + 1]
        o_ref[0, :, pair * HEAD_PAD:(pair + 1) * HEAD_PAD] = jnp.where(
            lane < V_DIM, o0, pltpu.roll(o1, V_DIM, axis=1)).astype(BF16)


def _attn_call(q, k, v, tq, tkc, hps):
    b, s, _ = q.shape
    resident = lambda: pl.BlockSpec((1, s, hps * HEAD_PAD), lambda bi, h, i: (bi, 0, h),
                                    pipeline_mode=pl.Buffered(1))
    return pl.pallas_call(
        functools.partial(_attn_kernel, tkc=tkc, cpb=math.gcd(s // tkc, ATTN_CHUNKS_PER_STEP)),
        name="attn",
        grid=(b, N_HEADS // hps, s // tq),
        in_specs=[pl.BlockSpec((1, tq, hps * HEAD_PAD), lambda bi, h, i: (bi, i, h)),
                  resident(), resident()],
        out_specs=pl.BlockSpec((1, tq, hps * V_DIM), lambda bi, h, i: (bi, i, h)),
        out_shape=jax.ShapeDtypeStruct((b, s, N_HEADS * V_DIM), BF16),
        compiler_params=pltpu.CompilerParams(
            dimension_semantics=("arbitrary", "arbitrary", "arbitrary"),
            vmem_limit_bytes=VMEM_LIMIT),
    )(q, k, v)


def _post_kernel(attn_ref, gm_ref, x_ref, mod_ref, goa_ref, woa_ref, wog_ref, g2_ref, wr_ref,
                 br_ref, wgus_ref, wds_ref,
                 ybase_ref, h2p_ref, e_ref, rank_ref, w_ref, cnt_ref):
    ts = x_ref.shape[1]
    mod = mod_ref[0]
    ga1, sh2, sc2, ga2 = mod[2:3], mod[3:4], mod[4:5], mod[5:6]
    an = _rms(attn_ref[0].astype(F32), goa_ref[...]).astype(BF16)
    y = _dot(an, woa_ref[...]) + _dot(gm_ref[0], wog_ref[...])
    x1 = x_ref[0] + ga1 * y
    h2 = _rms(x1, g2_ref[...]) * (1.0 + sc2) + sh2
    h2b = h2.astype(BF16)
    h2p_ref[...] = _pack_bf16_pairs(h2)

    gu = _dot(h2b, wgus_ref[...])
    g, u = gu[:, :SHARED_DIM], gu[:, SHARED_DIM:]
    a = (g * _sigmoid(g) * u).astype(BF16)
    ybase_ref[0] = x1 + ga2 * _dot(a, wds_ref[...])

    logits = _dot_nt(wr_ref[...], h2b)
    scores = _sigmoid(logits)
    biased = scores + br_ref[...]
    ninf = jnp.float32(-jnp.inf)
    bj = [biased[j * N_GROUPS:(j + 1) * N_GROUPS] for j in range(GROUP_SIZE)]
    sj = [scores[j * N_GROUPS:(j + 1) * N_GROUPS] for j in range(GROUP_SIZE)]
    m1 = bj[0]
    for j in range(1, GROUP_SIZE):
        m1 = jnp.maximum(m1, bj[j])
    found = jnp.zeros_like(m1)
    m2 = jnp.full_like(m1, ninf)
    for j in range(GROUP_SIZE):
        eq = jnp.where(bj[j] == m1, 1.0, 0.0)
        is_first = eq * (1.0 - found)
        found = jnp.maximum(found, eq)
        m2 = jnp.maximum(m2, jnp.where(is_first > 0.0, ninf, bj[j]))
    gs = m1 + m2
    gidx = lax.broadcasted_iota(jnp.int32, gs.shape, 0)
    grank = jnp.zeros_like(gs)
    for kk in range(1, N_GROUPS):
        r = pltpu.roll(gs, kk, axis=0)
        grank = grank + jnp.where(gidx >= kk, jnp.where(r >= gs, 1.0, 0.0), jnp.where(r > gs, 1.0, 0.0))
    gsel = grank < float(TOPK_GROUPS)
    masked = [jnp.where(gsel, bj[j], ninf) for j in range(GROUP_SIZE)]
    eidx = [gidx * GROUP_SIZE + j for j in range(GROUP_SIZE)]

    selm = [jnp.zeros_like(gs) for _ in range(GROUP_SIZE)]
    e_sel = []
    for _k in range(TOP_K):
        m = masked[0]
        for j in range(1, GROUP_SIZE):
            m = jnp.maximum(m, masked[j])
        m = jnp.max(m, axis=0, keepdims=True)
        cand = jnp.where(masked[0] == m, eidx[0], N_EXPERTS)
        for j in range(1, GROUP_SIZE):
            cand = jnp.minimum(cand, jnp.where(masked[j] == m, eidx[j], N_EXPERTS))
        emin = jnp.min(cand, axis=0, keepdims=True)
        e_sel.append(emin)
        for j in range(GROUP_SIZE):
            hit = eidx[j] == emin
            selm[j] = jnp.where(hit, 1.0, selm[j])
            masked[j] = jnp.where(hit, ninf, masked[j])

    wsel = [selm[j] * sj[j] for j in range(GROUP_SIZE)]
    tot = wsel[0]
    for j in range(1, GROUP_SIZE):
        tot = tot + wsel[j]
    tot = jnp.sum(tot, axis=0, keepdims=True)
    wn = [wsel[j] / tot * ROUTED_SCALE for j in range(GROUP_SIZE)]

    tm = e_ref.shape[2]
    sel = jnp.concatenate(selm, axis=0)
    tr = lax.broadcasted_iota(jnp.int32, (ts, ts), 0)
    tc = lax.broadcasted_iota(jnp.int32, (ts, ts), 1)
    sh = tm.bit_length() - 1
    same_tile = lax.shift_right_logical(tr, sh) == lax.shift_right_logical(tc, sh)
    upper = jnp.where(tr < tc, jnp.where(same_tile, 1.0, 0.0), 0.0).astype(BF16)
    rank_full = _dot(sel.astype(BF16), upper)
    rj = [rank_full[j * N_GROUPS:(j + 1) * N_GROUPS] for j in range(GROUP_SIZE)]
    subs = [slice(i * tm, (i + 1) * tm) for i in range(ts // tm)]
    for i, sub in enumerate(subs):
        cnt_ref[i] = jnp.broadcast_to(jnp.sum(sel[:, sub], axis=1, keepdims=True), (N_EXPERTS, LANES))

    for k in range(TOP_K):
        rk = jnp.zeros_like(gs)
        wk = jnp.zeros_like(gs)
        for j in range(GROUP_SIZE):
            hit = eidx[j] == e_sel[k]
            rk = rk + jnp.where(hit, rj[j], 0.0)
            wk = wk + jnp.where(hit, wn[j], 0.0)
        rk = jnp.sum(rk, axis=0, keepdims=True).astype(jnp.int32)
        wk = jnp.sum(wk, axis=0, keepdims=True)
        for i, sub in enumerate(subs):
            e_ref[i, k:k + 1, :] = e_sel[k][:, sub]
            rank_ref[i, k:k + 1, :] = rk[:, sub]
            w_ref[i, k:k + 1, :] = wk[:, sub]


def _post_call(attn, gm, x, mod, wts, ts, tm):
    b, s, d = x.shape
    nt = s // ts
    nsub = ts // tm
    ntm = b * nt * nsub
    (goa, woa, wog, g2, wr, br, wgus, wds) = wts
    full = lambda a: pl.BlockSpec(a.shape, lambda bi, i: (0,) * a.ndim)
    tok = lambda w: pl.BlockSpec((1, ts, w), lambda bi, i: (bi, i, 0))
    rout = pl.BlockSpec((nsub, TOP_K, tm), lambda bi, i: (bi * nt + i, 0, 0))
    out_shapes = (
        jax.ShapeDtypeStruct((b, s, d), F32),
        jax.ShapeDtypeStruct((b * s, d // 2), jnp.uint32),
        jax.ShapeDtypeStruct((ntm, TOP_K, tm), jnp.int32),
        jax.ShapeDtypeStruct((ntm, TOP_K, tm), jnp.int32),
        jax.ShapeDtypeStruct((ntm, TOP_K, tm), F32),
        jax.ShapeDtypeStruct((ntm, N_EXPERTS, LANES), F32),
    )
    return pl.pallas_call(
        _post_kernel,
        name="post",
        grid=(b, nt),
        in_specs=[tok(N_HEADS * V_DIM), tok(GMLP_W), tok(d),
                  pl.BlockSpec((1, N_MOD, d), lambda bi, i: (bi, 0, 0)),
                  full(goa), full(woa), full(wog), full(g2), full(wr), full(br), full(wgus), full(wds)],
        out_specs=(tok(d), pl.BlockSpec((ts, d // 2), lambda bi, i: (bi * nt + i, 0)),
                   rout, rout, rout,
                   pl.BlockSpec((nsub, N_EXPERTS, LANES), lambda bi, i: (bi * nt + i, 0, 0))),
        out_shape=out_shapes,
        compiler_params=pltpu.CompilerParams(dimension_semantics=("arbitrary", "arbitrary"),
                                             vmem_limit_bytes=VMEM_LIMIT),
    )(attn, gm, x, mod, goa, woa, wog, g2, wr, br, wgus, wds)


def _disp_kernel(pends_ref, zfrom_ref, info_ref, lpos_ref, h_ref, xs_ref, sbuf, zbuf, nprev, sems, zsem):
    td = h_ref.shape[0]
    blk = zbuf.shape[0]
    step = pl.program_id(0)
    slot = lax.rem(step, 2)

    @pl.when(step == 0)
    def _():
        nprev[0] = 0
        zbuf[...] = jnp.zeros_like(zbuf)

        def tail_copy(e, j):
            start = pl.multiple_of(zfrom_ref[e] + j * blk, blk)
            return pltpu.make_async_copy(zbuf, xs_ref.at[pl.ds(start, blk), :], zsem)

        def n_tail(e):
            return (pends_ref[e] - zfrom_ref[e]) // blk

        def zstart(e, c):
            def one(j, c2):
                tail_copy(e, j).start()
                return c2
            return lax.fori_loop(0, n_tail(e), one, c)

        def zwait(e, c):
            def one(j, c2):
                tail_copy(e, j).wait()
                return c2
            return lax.fori_loop(0, n_tail(e), one, c)

        lax.fori_loop(0, N_EXPERTS, zstart, 0)
        lax.fori_loop(0, N_EXPERTS, zwait, 0)

    lo, hi = _unpack_bf16_pairs(h_ref[...])
    lo = lo.astype(BF16)
    hi = hi.astype(BF16)
    lpos = lpos_ref[0]
    lpos_b = [jnp.broadcast_to(lpos[k:k + 1, :], (PERM_ROWS, td)).astype(jnp.int16) for k in range(TOP_K)]
    row0 = lax.broadcasted_iota(jnp.int32, (PERM_ROWS, td), 0)
    one = jnp.ones((PERM_ROWS, td), BF16)

    def build(rb, c):
        r0 = pl.multiple_of(rb * PERM_ROWS, PERM_ROWS)
        riota = (row0 + r0).astype(jnp.int16)
        pb = jnp.zeros((PERM_ROWS, td), BF16)
        for k in range(TOP_K):
            pb = jnp.where(lpos_b[k] == riota, one, pb)
        xlo = pltpu.bitcast(_dot(pb, lo), jnp.uint32)
        xhi = pltpu.bitcast(_dot(pb, hi), jnp.uint32)
        sbuf[slot, pl.ds(r0, PERM_ROWS), :] = (xhi & jnp.uint32(0xFFFF0000)) | (xlo >> 16)
        return c

    lax.fori_loop(0, info_ref[0, 2, 1], build, 0)

    def chunk_copy(src, dst, sl):
        return pltpu.make_async_copy(
            sbuf.at[sl, pl.ds(pl.multiple_of(src, ROW_ALIGN), DISP_CHUNK), :],
            xs_ref.at[pl.ds(pl.multiple_of(dst, ROW_ALIGN), DISP_CHUNK), :], sems.at[sl])

    def drain(n, sl):
        def one(c, carry):
            chunk_copy(0, 0, sl).wait()
            return carry
        lax.fori_loop(0, n, one, 0)

    def issue(c, carry):
        chunk_copy(info_ref[0, 0, c], info_ref[0, 1, c], slot).start()
        return carry

    drain(nprev[0], 1 - slot)
    n_chunks = info_ref[0, 2, 0]
    lax.fori_loop(0, n_chunks, issue, 0)
    nprev[0] = n_chunks

    @pl.when(step == pl.num_programs(0) - 1)
    def _():
        drain(n_chunks, slot)


def _sorted_rows(td):
    need = TOP_K * td + N_EXPERTS * (ROW_ALIGN - 1) + DISP_CHUNK
    return -(-need // PERM_ROWS) * PERM_ROWS


def _disp_call(pends, zfrom, info, lpos, h2p, n_rows, td, blk):
    t, w = h2p.shape
    nt = t // td
    return pl.pallas_call(
        _disp_kernel,
        name="disp",
        grid_spec=pltpu.PrefetchScalarGridSpec(
            num_scalar_prefetch=2,
            grid=(nt,),
            in_specs=[pl.BlockSpec((1,) + info.shape[1:], lambda i, pe, pa: (i, 0, 0), memory_space=pltpu.SMEM),
                      pl.BlockSpec((1, TOP_K, td), lambda i, pe, pa: (i, 0, 0)),
                      pl.BlockSpec((td, w), lambda i, pe, pa: (i, 0))],
            out_specs=pl.BlockSpec(memory_space=pl.ANY),
            scratch_shapes=[pltpu.VMEM((2, _sorted_rows(td), w), jnp.uint32),
                            pltpu.VMEM((blk, w), jnp.uint32),
                            pltpu.SMEM((1,), jnp.int32),
                            pltpu.SemaphoreType.DMA((2,)), pltpu.SemaphoreType.DMA(())],
        ),
        out_shape=jax.ShapeDtypeStruct((n_rows, w), jnp.uint32),
        compiler_params=pltpu.CompilerParams(dimension_semantics=("arbitrary",),
                                             vmem_limit_bytes=VMEM_LIMIT),
    )(pends, zfrom, info, lpos, h2p)


def _exp_kernel(bexp_ref, nused_ref, xs_ref, wgu_ref, wd_ref, ys_ref):
    i = pl.program_id(0)

    @pl.when(i < nused_ref[0])
    def _():
        half = D_MODEL // 2
        lo, hi = _unpack_bf16_pairs(xs_ref[...])
        gu = (_dot(lo.astype(BF16), wgu_ref[0, :half, :])
              + _dot(hi.astype(BF16), wgu_ref[0, half:, :]))
        g, u = gu[:, :EXPERT_DIM], gu[:, EXPERT_DIM:]
        a = (g * _sigmoid(g) * u).astype(BF16)
        ys_ref[...] = _dot(a, wd_ref[0]).astype(BF16)


def _exp_call(block_exp, n_used, xs, wgu, wd, blk):
    n_rows, w = xs.shape
    n_blocks = n_rows // blk

    def row_map(i, bexp, nused):
        return (jnp.minimum(i, nused[0] - 1), 0)

    def w_map(i, bexp, nused):
        return (bexp[jnp.minimum(i, nused[0] - 1)], 0, 0)

    return pl.pallas_call(
        _exp_kernel,
        name="exp",
        grid_spec=pltpu.PrefetchScalarGridSpec(
            num_scalar_prefetch=2,
            grid=(n_blocks,),
            in_specs=[pl.BlockSpec((blk, w), row_map),
                      pl.BlockSpec((1, D_MODEL, 2 * EXPERT_DIM), w_map),
                      pl.BlockSpec((1, EXPERT_DIM, D_MODEL), w_map)],
            out_specs=pl.BlockSpec((blk, D_MODEL), row_map),
        ),
        out_shape=jax.ShapeDtypeStruct((n_rows, D_MODEL), BF16),
        compiler_params=pltpu.CompilerParams(dimension_semantics=("arbitrary",),
                                             vmem_limit_bytes=VMEM_LIMIT),
    )(block_exp, n_used, xs, wgu, wd)


def _comb_kernel(info_ref, next_info_ref, ybase_ref, mod_ref, cpos_ref, w_ref, gf_ref, ys_ref, o_ref,
                 ybuf, acc, sems):
    tc = ybase_ref.shape[1]
    step = pl.program_id(0) * pl.num_programs(1) + pl.program_id(1)
    n_steps = pl.num_programs(0) * pl.num_programs(1)
    slot = lax.rem(step, 2)

    def chunk_copy(src, dst, sl, n_tiles):
        return pltpu.make_async_copy(ys_ref.at[pl.ds(src, n_tiles)], ybuf.at[sl, pl.ds(dst, n_tiles)],
                                     sems.at[sl])

    def fetch(iref, sl):
        for i, n_tiles in enumerate(COMB_CHUNK_TILES):
            def issue(c, carry):
                chunk_copy(iref[0, 2 * i, c], iref[0, 2 * i + 1, c], sl, n_tiles).start()
                return carry
            lax.fori_loop(0, iref[0, 2 * len(COMB_CHUNK_TILES), i], issue, 0)

    @pl.when(step == 0)
    def _():
        ybuf[...] = jnp.zeros_like(ybuf)
        fetch(info_ref, 0)

    @pl.when(step + 1 < n_steps)
    def _():
        fetch(next_info_ref, 1 - slot)

    for i, n_tiles in enumerate(COMB_CHUNK_TILES):
        def drain(c, carry):
            chunk_copy(0, 0, slot, n_tiles).wait()
            return carry
        lax.fori_loop(0, info_ref[0, 2 * len(COMB_CHUNK_TILES), i], drain, 0)

    cpos = cpos_ref[...]
    w = w_ref[...]
    cpos_b = [jnp.broadcast_to(cpos[:, k:k + 1], (tc, LANES)).astype(jnp.int16) for k in range(TOP_K)]
    w_b = [jnp.broadcast_to(w[:, k:k + 1], (tc, LANES)).astype(BF16) for k in range(TOP_K)]
    lane = lax.broadcasted_iota(jnp.int32, (tc, LANES), 1)
    acc[...] = jnp.zeros_like(acc)

    def slab(kt, c):
        k0 = pl.multiple_of(kt * COMB_KTILE, COMB_KTILE)
        cols = []
        for j in range(COMB_KTILE // LANES):
            col = (lane + (k0 + j * LANES)).astype(jnp.int16)
            wm = jnp.zeros((tc, LANES), BF16)
            for k in range(TOP_K):
                wm = jnp.where(cpos_b[k] == col, w_b[k], wm)
            cols.append(wm)
        rows = ybuf[slot, pl.ds(kt * (COMB_KTILE // COMB_CHUNK), COMB_KTILE // COMB_CHUNK)]
        acc[...] += _dot(jnp.concatenate(cols, axis=1), rows.reshape(COMB_KTILE, rows.shape[-1]))
        return c

    lax.fori_loop(0, info_ref[0, 2 * len(COMB_CHUNK_TILES), len(COMB_CHUNK_TILES)], slab, 0)
    ga2 = mod_ref[0][5:6]
    o_ref[0] = _rms(ybase_ref[0] + ga2 * acc[...], gf_ref[...])


def _gather_rows(tc):
    need = TOP_K * tc + N_EXPERTS * 2 * (COMB_CHUNK - 1)
    return -(-need // COMB_KTILE) * COMB_KTILE


def _comb_call(info, ybase, mod, cpos_tok, w_tok, g_final, ys, tc):
    b, s, d = ybase.shape
    nt = s // tc
    return pl.pallas_call(
        _comb_kernel,
        name="comb",
        grid=(b, nt),
        in_specs=[pl.BlockSpec((1,) + info.shape[1:], lambda bi, i: (bi * nt + i, 0, 0), memory_space=pltpu.SMEM),
                  pl.BlockSpec((1,) + info.shape[1:], lambda bi, i: (jnp.minimum(bi * nt + i + 1, b * nt - 1), 0, 0),
                               memory_space=pltpu.SMEM),
                  pl.BlockSpec((1, tc, d), lambda bi, i: (bi, i, 0)),
                  pl.BlockSpec((1, N_MOD, d), lambda bi, i: (bi, 0, 0)),
                  pl.BlockSpec((tc, TOP_K), lambda bi, i: (bi * nt + i, 0)),
                  pl.BlockSpec((tc, TOP_K), lambda bi, i: (bi * nt + i, 0)),
                  pl.BlockSpec((1, d), lambda bi, i: (0, 0)),
                  pl.BlockSpec(memory_space=pl.ANY)],
        out_specs=pl.BlockSpec((1, tc, d), lambda bi, i: (bi, i, 0)),
        out_shape=jax.ShapeDtypeStruct((b, s, d), F32),
        scratch_shapes=[pltpu.VMEM((2, _gather_rows(tc) // COMB_CHUNK, COMB_CHUNK, d), BF16),
                        pltpu.VMEM((tc, d), F32), pltpu.SemaphoreType.DMA((2,))],
        compiler_params=pltpu.CompilerParams(dimension_semantics=("arbitrary", "arbitrary"),
                                             vmem_limit_bytes=VMEM_LIMIT),
    )(info, info, ybase, mod, cpos_tok, w_tok, g_final, ys)


def _prep_weights(w_ada, b_ada, g_norm1, w_in, g_q_lat, w_uq, g_kv_lat, w_ukv, g_gmlp_v, w_spatial,
                  b_spatial, g_out_attn, g_out_gmlp, w_out, g_norm2, w_router, b_router, w_gate_e,
                  w_up_e, w_down_e, w_gate_s, w_up_s, w_down_s, g_final):
    row = lambda g: g.reshape(1, -1).astype(F32)
    o1 = Q_LORA
    o2 = o1 + KV_LORA
    o3 = o2 + QK_ROPE
    o4 = o3 + GMLP_W
    kr_cols = jnp.pad(w_in[:, o2:o3], ((0, 0), (QK_NOPE, LANES - QK_NOPE - QK_ROPE)))
    win = jnp.concatenate([w_in[:, :o2], w_in[:, o3:o4], w_in[:, o4:], kr_cols], axis=1).astype(BF16)
    qd = QK_NOPE + QK_ROPE
    wuq = jnp.pad(w_uq.reshape(Q_LORA, N_HEADS, qd), ((0, 0), (0, 0), (0, HEAD_PAD - qd)))
    wuq = wuq.reshape(Q_LORA, N_HEADS * HEAD_PAD).astype(BF16)
    wkv = w_ukv.reshape(KV_LORA, N_HEADS, QK_NOPE + V_DIM)
    wuk = jnp.pad(wkv[:, :, :QK_NOPE], ((0, 0), (0, 0), (0, HEAD_PAD - QK_NOPE)))
    wuk = wuk.reshape(KV_LORA, N_HEADS * HEAD_PAD).astype(BF16)
    wuv = jnp.pad(wkv[:, :, QK_NOPE:], ((0, 0), (0, 0), (0, HEAD_PAD - V_DIM)))
    wuv = wuv.reshape(KV_LORA, N_HEADS * HEAD_PAD).astype(BF16)
    vone = jnp.tile((jnp.arange(HEAD_PAD) == V_DIM).astype(F32), N_HEADS).reshape(1, -1)
    wsp = w_spatial.reshape(N_HEADS // 2, 2, CHUNK, CHUNK).transpose(0, 2, 1, 3)
    wsp = wsp.reshape(N_HEADS // 2, CHUNK, 2 * CHUNK).astype(BF16)
    bsp = jnp.repeat(jnp.transpose(b_spatial), GMLP_W // N_HEADS, axis=1).astype(F32)
    pre = (row(g_norm1), win, row(g_q_lat), wuq, row(g_kv_lat), wuk, wuv, vone, row(g_gmlp_v), wsp,
           bsp, row(g_out_gmlp))
    perm = (jnp.arange(N_GROUPS)[None, :] * GROUP_SIZE + jnp.arange(GROUP_SIZE)[:, None]).reshape(-1)
    wr = jnp.transpose(w_router)[perm].astype(BF16)
    br = b_router.astype(F32)[perm].reshape(N_EXPERTS, 1)
    mla_w = N_HEADS * V_DIM
    wgus = jnp.concatenate([w_gate_s, w_up_s], axis=1).astype(BF16)
    post = (row(g_out_attn), w_out[:mla_w].astype(BF16), w_out[mla_w:].astype(BF16), row(g_norm2),
            wr, br, wgus, w_down_s.astype(BF16))
    wgu_e = jnp.concatenate([w_gate_e, w_up_e], axis=2).astype(BF16)
    wd_e = w_down_e.astype(BF16)
    return w_ada.astype(BF16), b_ada, pre, post, (wgu_e, wd_e), row(g_final)


def _rope_tables(s):
    inv = 1.0 / (ROPE_THETA ** (jnp.arange(0, QK_ROPE, 2, dtype=F32) / QK_ROPE))
    ang = jnp.arange(s, dtype=F32)[:, None] * inv[None, :]
    cos, sin = jnp.cos(ang), jnp.sin(ang)
    z = lambda n: jnp.zeros((s, n), F32)
    tail = LANES - QK_NOPE - QK_ROPE
    c = jnp.concatenate([jnp.ones((s, QK_NOPE), F32), cos, cos, z(tail)], axis=1)
    s1 = jnp.concatenate([z(QK_NOPE), -sin, z(HALF_ROPE), z(tail)], axis=1)
    s2 = jnp.concatenate([z(QK_NOPE), z(HALF_ROPE), sin, z(tail)], axis=1)
    return c, s1, s2


def _tiles(s):
    ts = min(512, s)
    tq = min(512, s)
    tkc = min(1024, s)
    blk = 1024
    tm = min(256, s)
    hps = 8 if s * 8 * HEAD_PAD * 2 * 2 <= VMEM_LIMIT // 4 else 2
    return ts, tq, tkc, blk, tm, hps


def _ceil_to(x, m):
    return (x + m - 1) // m * m


def _trunk(x, c, prep, tiles=None):
    w_ada, b_ada, pre_w, post_w, exp_w, g_final = prep
    b, s, d = x.shape
    ts, tq, tkc, blk, tm, hps = tiles or _tiles(s)
    t = b * s
    nt = t // tm
    mod = _modulation(c, w_ada, b_ada)
    q, k, v, gm = _pre_call(x, mod, _rope_tables(s), pre_w, ts)
    attn = _attn_call(q, k, v, tq, tkc, hps)
    ybase, h2p, e_arr, lrank, w_arr, cnt = _post_call(attn, gm, x, mod, post_w, ts, tm)

    i32 = jnp.int32
    cnt = cnt[:, :, 0].astype(i32).reshape(nt, GROUP_SIZE, N_GROUPS).transpose(0, 2, 1).reshape(nt, N_EXPERTS)
    cnt8 = _ceil_to(cnt, ROW_ALIGN)
    base8 = jnp.cumsum(cnt8, axis=0) - cnt8
    total8 = jnp.sum(cnt8, axis=0)
    padded = _ceil_to(total8 + EXPERT_SLACK, blk)
    pends = jnp.cumsum(padded).astype(i32)
    zfrom = ((pends - padded + total8) // blk * blk).astype(i32)
    dstbase = (pends - padded)[None, :] + base8
    toff8 = jnp.cumsum(cnt8, axis=1) - cnt8
    eids = jnp.arange(N_EXPERTS, dtype=i32)

    def chunk_table(nch, chunk, n_max):
        cend = jnp.cumsum(nch, axis=1)
        cidx = jnp.arange(n_max, dtype=i32)
        e_of_c = jnp.minimum(jnp.sum((cend[:, None, :] <= cidx[None, :, None]).astype(i32), axis=-1),
                             N_EXPERTS - 1)
        pick = lambda tbl: jnp.sum(jnp.where(e_of_c[..., None] == eids, tbl[:, None, :], 0), axis=-1)
        rel = lambda first_row: pick(first_row - (cend - nch) * chunk) + cidx[None, :] * chunk
        return rel, cend[:, -1]

    n_dmax = N_EXPERTS + TOP_K * tm // DISP_CHUNK
    nch_d = (cnt + DISP_CHUNK - 1) // DISP_CHUNK
    rel_d, n_dch = chunk_table(nch_d, DISP_CHUNK, n_dmax)
    n_rb = (jnp.sum(cnt8, axis=1) + DISP_CHUNK + PERM_ROWS - 1) // PERM_ROWS
    tail = lambda a, b2, n: jnp.concatenate([a[:, None], b2[:, None], jnp.zeros((nt, n - 2), i32)], axis=1)
    dinfo = jnp.stack([rel_d(toff8), rel_d(dstbase), tail(n_dch, n_rb, n_dmax)], axis=1).astype(i32)

    shift = dstbase % COMB_CHUNK
    nch_c = jnp.where(cnt > 0, (cnt + shift + COMB_CHUNK - 1) // COMB_CHUNK, 0)
    boff = (jnp.cumsum(nch_c, axis=1) - nch_c) * COMB_CHUNK
    big = COMB_CHUNK_TILES[0]
    n_cmax = max(_gather_rows(tm) // COMB_CHUNK // big, N_EXPERTS)
    src_t = (dstbase - shift) // COMB_CHUNK
    dst_t = boff // COMB_CHUNK
    done = nch_c // big * big
    rows, totals = [], []
    for size in COMB_CHUNK_TILES:
        n_size = nch_c // big if size == big else (nch_c - done == size).astype(i32)
        first = 0 if size == big else done
        rel, total = chunk_table(n_size, size, n_cmax)
        rows += [rel(src_t + first), rel(dst_t + first)]
        totals.append(total[:, None])
    n_kt = (jnp.sum(nch_c, axis=1) * COMB_CHUNK + COMB_KTILE - 1) // COMB_KTILE
    counts = jnp.concatenate(totals + [n_kt[:, None], jnp.zeros((nt, n_cmax - len(totals) - 1), i32)], axis=1)
    cinfo = jnp.stack(rows + [counts], axis=1).astype(i32)

    onehot = e_arr[..., None] == eids
    lookup = lambda tbl: jnp.sum(jnp.where(onehot, tbl[:, None, None, :], 0), axis=-1)
    lpos = lookup(toff8) + lrank
    cpos = lookup(boff + shift) + lrank
    tok_major = lambda a: jnp.transpose(a, (0, 2, 1)).reshape(t, TOP_K)

    n_rows = _ceil_to(t * TOP_K + nt * N_EXPERTS * (ROW_ALIGN - 1) + N_EXPERTS * (EXPERT_SLACK + blk - 1), blk)
    n_blocks = n_rows // blk
    n_used = (pends[-1] // blk).astype(i32).reshape(1)
    block_start = jnp.arange(n_blocks, dtype=i32) * blk
    block_exp = jnp.minimum(jnp.sum((pends[None, :] <= block_start[:, None]).astype(i32), axis=1),
                            N_EXPERTS - 1)

    xs = _disp_call(pends, zfrom, dinfo, lpos.astype(i32), h2p, n_rows, tm, blk)
    ys = _exp_call(block_exp, n_used, xs, exp_w[0], exp_w[1], blk)
    ys3 = ys.reshape(n_rows // COMB_CHUNK, COMB_CHUNK, d)
    return _comb_call(cinfo, ybase, mod, tok_major(cpos).astype(i32), tok_major(w_arr), g_final, ys3, tm)


def kernel(x_prompt, x_sample, c_prompt, c_sample, w_ada, b_ada, g_norm1, w_in, g_q_lat, w_uq, g_kv_lat, w_ukv, g_gmlp_v, w_spatial, b_spatial, g_out_attn, g_out_gmlp, w_out, g_norm2, w_router, b_router, w_gate_e, w_up_e, w_down_e, w_gate_s, w_up_s, w_down_s, g_final):
    prep = _prep_weights(w_ada[0], b_ada[0], g_norm1[0], w_in[0], g_q_lat[0], w_uq[0], g_kv_lat[0],
                         w_ukv[0], g_gmlp_v[0], w_spatial[0], b_spatial[0], g_out_attn[0],
                         g_out_gmlp[0], w_out[0], g_norm2[0], w_router[0], b_router[0], w_gate_e[0],
                         w_up_e[0], w_down_e[0], w_gate_s[0], w_up_s[0], w_down_s[0], g_final)
    return (_trunk(x_prompt, c_prompt, prep), _trunk(x_sample, c_sample, prep))
```

```python
import functools
import math

import jax
import jax.numpy as jnp
from jax import lax
from jax.experimental import pallas as pl
from jax.experimental.pallas import tpu as pltpu

F32 = jnp.float32
BF16 = jnp.bfloat16

D_MODEL = 1024
N_HEADS = 8
QK_NOPE = 64
QK_ROPE = 32
V_DIM = 64
Q_LORA = 256
KV_LORA = 128
GMLP_W = 512
CHUNK = 128
N_EXPERTS = 64
TOP_K = 8
N_GROUPS = 8
TOPK_GROUPS = 4
GROUP_SIZE = N_EXPERTS // N_GROUPS
EXPERT_DIM = 256
SHARED_DIM = 256
ROUTED_SCALE = 2.5
ROPE_THETA = 10000.0
N_MOD = 6
EPS = 1e-6

LANES = 128
HEAD_PAD = 128
HALF_ROPE = QK_ROPE // 2
VMEM_LIMIT = 52 * 1024 * 1024
ROW_ALIGN = 8
DISP_CHUNK = 48
COMB_CHUNK = 16
COMB_CHUNK_TILES = (3, 2, 1)
ATTN_CHUNKS_PER_STEP = 8
PERM_ROWS = 512
COMB_KTILE = 1024
EXPERT_SLACK = max(DISP_CHUNK, 2 * (COMB_CHUNK - 1))

SOFTMAX_SCALE = (QK_NOPE + QK_ROPE) ** -0.5
EXP2_SCALE = SOFTMAX_SCALE * math.log2(math.e)


def _rms(x, g):
    return x * lax.rsqrt(jnp.mean(x * x, axis=-1, keepdims=True) + EPS) * g


def _sigmoid(x):
    return 1.0 / (1.0 + jnp.exp(-x))


def _gelu_tanh(x):
    c = math.sqrt(2.0 / math.pi)
    return 0.5 * x * (1.0 + jnp.tanh(c * (x + 0.044715 * (x * x * x))))


def _dot(a, b):
    return jnp.dot(a, b, preferred_element_type=F32)


def _dot_nt(a, b):
    return lax.dot_general(a, b, (((1,), (1,)), ((), ())), preferred_element_type=F32)


def _pack_bf16_pairs(x):
    c = x.shape[1] // 2
    lo = pltpu.bitcast(x[:, :c].astype(BF16).astype(F32), jnp.uint32)
    hi = pltpu.bitcast(x[:, c:].astype(BF16).astype(F32), jnp.uint32)
    return (hi & jnp.uint32(0xFFFF0000)) | (lo >> 16)


def _unpack_bf16_pairs(w):
    lo = pltpu.bitcast(w << 16, F32)
    hi = pltpu.bitcast(w & jnp.uint32(0xFFFF0000), F32)
    return lo, hi


def _mod_kernel(c_ref, w_ref, b_ref, o_ref):
    c = c_ref[...]
    a = (c * _sigmoid(c)).astype(BF16)
    o_ref[...] = _dot(a, w_ref[...]) + b_ref[...]


def _modulation(c, w_ada_bf, b_ada):
    b = c.shape[0]
    bp = max(16, -(-b // 16) * 16)
    cp = jnp.pad(c, ((0, bp - b), (0, 0)))
    n = w_ada_bf.shape[1]
    tn = D_MODEL
    out = pl.pallas_call(
        _mod_kernel,
        name="mod",
        grid=(n // tn,),
        in_specs=[
            pl.BlockSpec((bp, D_MODEL), lambda j: (0, 0)),
            pl.BlockSpec((D_MODEL, tn), lambda j: (0, j)),
            pl.BlockSpec((1, tn), lambda j: (0, j)),
        ],
        out_specs=pl.BlockSpec((bp, tn), lambda j: (0, j)),
        out_shape=jax.ShapeDtypeStruct((bp, n), F32),
        compiler_params=pltpu.CompilerParams(dimension_semantics=("arbitrary",)),
    )(cp, w_ada_bf, b_ada.reshape(1, n))
    return out[:b].reshape(b, N_MOD, D_MODEL)


def _rope(xh, c, s1, s2):
    return (xh * c + pltpu.roll(xh, LANES - HALF_ROPE, axis=1) * s1
            + pltpu.roll(xh, HALF_ROPE, axis=1) * s2)


def _pre_kernel(x_ref, mod_ref, cos_ref, s1_ref, s2_ref, g1_ref, win_ref, gq_ref, wuq_ref,
                gkv_ref, wuk_ref, wuv_ref, vone_ref, ggv_ref, ws_ref, bs_ref, ggo_ref,
                q_ref, k_ref, v_ref, gm_ref, mix_ref):
    ts = x_ref.shape[1]
    x = x_ref[0]
    mod = mod_ref[0]
    h = _rms(x, g1_ref[...]) * (1.0 + mod[1:2]) + mod[0:1]
    z = _dot(h.astype(BF16), win_ref[...])
    o_kv = Q_LORA
    o_gu = o_kv + KV_LORA
    o_gv = o_gu + GMLP_W
    o_kr = o_gv + GMLP_W
    q_lat = z[:, :o_kv]
    kv_lat = z[:, o_kv:o_gu]
    g_u = z[:, o_gu:o_gv]
    g_v = z[:, o_gv:o_kr]
    kr = z[:, o_kr:o_kr + LANES]

    cos = cos_ref[...]
    s1 = s1_ref[...]
    s2 = s2_ref[...]

    qn = _rms(q_lat, gq_ref[...]).astype(BF16)
    q = _dot(qn, wuq_ref[...])
    kn = _rms(kv_lat, gkv_ref[...]).astype(BF16)
    kf = _dot(kn, wuk_ref[...])
    v_ref[0] = (_dot(kn, wuv_ref[...]) + vone_ref[...]).astype(BF16)
    krr = _rope(kr, cos, s1, s2)
    for hd in range(N_HEADS):
        sl = slice(hd * HEAD_PAD, (hd + 1) * HEAD_PAD)
        q_ref[0, :, sl] = (_rope(q[:, sl], cos, s1, s2) * EXP2_SCALE).astype(BF16)
        k_ref[0, :, sl] = (kf[:, sl] + krr).astype(BF16)

    u = _gelu_tanh(g_u)
    vn = _rms(_gelu_tanh(g_v), ggv_ref[...]).astype(BF16)
    lane = lax.broadcasted_iota(jnp.int32, (CHUNK, LANES), 1)
    left = lane < (LANES // 2)
    zero = jnp.zeros((CHUNK, LANES), BF16)
    for n in range(ts // CHUNK):
        rs = slice(n * CHUNK, (n + 1) * CHUNK)
        for p in range(GMLP_W // LANES):
            cs = slice(p * LANES, (p + 1) * LANES)
            vp = vn[rs, cs]
            rhs = jnp.concatenate([jnp.where(left, vp, zero), jnp.where(left, zero, vp)], axis=0)
            mix_ref[rs, cs] = _dot(ws_ref[p], rhs) + bs_ref[:, cs]
    gm = u * mix_ref[...]
    gm_ref[0] = _rms(gm, ggo_ref[...]).astype(BF16)


def _pre_call(x, mod, tabs, wts, ts):
    b, s, d = x.shape
    cos, s1, s2 = tabs
    (g1, win, gq, wuq, gkv, wuk, wuv, vone, ggv, wsp, bsp, ggo) = wts
    full = lambda a: pl.BlockSpec(a.shape, lambda bi, i: (0,) * a.ndim)
    tab = pl.BlockSpec((ts, LANES), lambda bi, i: (i, 0))
    out_shapes = (
        jax.ShapeDtypeStruct((b, s, N_HEADS * HEAD_PAD), BF16),
        jax.ShapeDtypeStruct((b, s, N_HEADS * HEAD_PAD), BF16),
        jax.ShapeDtypeStruct((b, s, N_HEADS * HEAD_PAD), BF16),
        jax.ShapeDtypeStruct((b, s, GMLP_W), BF16),
    )
    tok = lambda w: pl.BlockSpec((1, ts, w), lambda bi, i: (bi, i, 0))
    return pl.pallas_call(
        _pre_kernel,
        name="pre",
        grid=(b, s // ts),
        in_specs=[tok(d), pl.BlockSpec((1, N_MOD, d), lambda bi, i: (bi, 0, 0)), tab, tab, tab,
                  full(g1), full(win), full(gq), full(wuq), full(gkv), full(wuk), full(wuv), full(vone),
                  full(ggv), full(wsp), full(bsp), full(ggo)],
        out_specs=(tok(N_HEADS * HEAD_PAD), tok(N_HEADS * HEAD_PAD), tok(N_HEADS * HEAD_PAD), tok(GMLP_W)),
        out_shape=out_shapes,
        scratch_shapes=[pltpu.VMEM((ts, GMLP_W), F32)],
        compiler_params=pltpu.CompilerParams(dimension_semantics=("arbitrary", "arbitrary"),
                                             vmem_limit_bytes=VMEM_LIMIT),
    )(x, mod, cos, s1, s2, g1, win, gq, wuq, gkv, wuk, wuv, vone, ggv, wsp, bsp, ggo)


def _attn_kernel(q_ref, k_ref, v_ref, o_ref, *, tkc, cpb):
    tq = q_ref.shape[1]
    n_chunks = k_ref.shape[1] // tkc
    hps = q_ref.shape[2] // HEAD_PAD
    heads = [slice(hh * HEAD_PAD, (hh + 1) * HEAD_PAD) for hh in range(hps)]
    qs = [q_ref[0, :, hs] for hs in heads]

    def step(i, carry):
        carry = list(carry)
        for cc in range(cpb):
            off = pl.multiple_of((i * cpb + cc) * tkc, tkc)
            for hh, hs in enumerate(heads):
                m, acc = carry[2 * hh], carry[2 * hh + 1]
                kc = k_ref[0, pl.ds(off, tkc), hs]
                vc = v_ref[0, pl.ds(off, tkc), hs]
                sc = _dot_nt(qs[hh], kc)
                m_new = jnp.maximum(m, jnp.max(sc, axis=-1, keepdims=True))
                p = jnp.exp2(sc - m_new).astype(BF16)
                carry[2 * hh + 1] = jnp.exp2(m - m_new) * acc + _dot(p, vc)
                carry[2 * hh] = m_new
        return tuple(carry)

    n_steps = n_chunks // cpb
    init = (jnp.full((tq, 1), -jnp.inf, F32), jnp.zeros((tq, HEAD_PAD), F32)) * hps
    res = step(0, init) if n_steps == 1 else lax.fori_loop(0, n_steps, step, init)
    lane = lax.broadcasted_iota(jnp.int32, (tq, HEAD_PAD), 1)
    for pair in range(hps // 2):
        a0, a1 = res[4 * pair + 1], res[4 * pair + 3]
        o0 = a0 / a0[:, V_DIM:V_DIM + 1]
        o1 = a1 / a1[:, V_DIM:V_DIM + 1]
        o_ref[0, :, pair * HEAD_PAD:(pair + 1) * HEAD_PAD] = jnp.where(
            lane < V_DIM, o0, pltpu.roll(o1, V_DIM, axis=1)).astype(BF16)


def _attn_call(q, k, v, tq, tkc, hps):
    b, s, _ = q.shape
    resident = lambda: pl.BlockSpec((1, s, hps * HEAD_PAD), lambda bi, h, i: (bi, 0, h),
                                    pipeline_mode=pl.Buffered(1))
    return pl.pallas_call(
        functools.partial(_attn_kernel, tkc=tkc, cpb=math.gcd(s // tkc, ATTN_CHUNKS_PER_STEP)),
        name="attn",
        grid=(b, N_HEADS // hps, s // tq),
        in_specs=[pl.BlockSpec((1, tq, hps * HEAD_PAD), lambda bi, h, i: (bi, i, h)),
                  resident(), resident()],
        out_specs=pl.BlockSpec((1, tq, hps * V_DIM), lambda bi, h, i: (bi, i, h)),
        out_shape=jax.ShapeDtypeStruct((b, s, N_HEADS * V_DIM), BF16),
        compiler_params=pltpu.CompilerParams(
            dimension_semantics=("arbitrary", "arbitrary", "arbitrary"),
            vmem_limit_bytes=VMEM_LIMIT),
    )(q, k, v)


def _post_kernel(attn_ref, gm_ref, x_ref, mod_ref, goa_ref, woa_ref, wog_ref, g2_ref, wr_ref,
                 br_ref, wgus_ref, wds_ref,
                 ybase_ref, h2p_ref, e_ref, rank_ref, w_ref, cnt_ref):
    ts = x_ref.shape[1]
    mod = mod_ref[0]
    ga1, sh2, sc2, ga2 = mod[2:3], mod[3:4], mod[4:5], mod[5:6]
    an = _rms(attn_ref[0].astype(F32), goa_ref[...]).astype(BF16)
    y = _dot(an, woa_ref[...]) + _dot(gm_ref[0], wog_ref[...])
    x1 = x_ref[0] + ga1 * y
    h2 = _rms(x1, g2_ref[...]) * (1.0 + sc2) + sh2
    h2b = h2.astype(BF16)
    h2p_ref[...] = _pack_bf16_pairs(h2)

    gu = _dot(h2b, wgus_ref[...])
    g, u = gu[:, :SHARED_DIM], gu[:, SHARED_DIM:]
    a = (g * _sigmoid(g) * u).astype(BF16)
    ybase_ref[0] = x1 + ga2 * _dot(a, wds_ref[...])

    logits = _dot_nt(wr_ref[...], h2b)
    scores = _sigmoid(logits)
    biased = scores + br_ref[...]
    ninf = jnp.float32(-jnp.inf)
    bj = [biased[j * N_GROUPS:(j + 1) * N_GROUPS] for j in range(GROUP_SIZE)]
    sj = [scores[j * N_GROUPS:(j + 1) * N_GROUPS] for j in range(GROUP_SIZE)]
    m1 = bj[0]
    for j in range(1, GROUP_SIZE):
        m1 = jnp.maximum(m1, bj[j])
    found = jnp.zeros_like(m1)
    m2 = jnp.full_like(m1, ninf)
    for j in range(GROUP_SIZE):
        eq = jnp.where(bj[j] == m1, 1.0, 0.0)
        is_first = eq * (1.0 - found)
        found = jnp.maximum(found, eq)
        m2 = jnp.maximum(m2, jnp.where(is_first > 0.0, ninf, bj[j]))
    gs = m1 + m2
    gidx = lax.broadcasted_iota(jnp.int32, gs.shape, 0)
    grank = jnp.zeros_like(gs)
    for kk in range(1, N_GROUPS):
        r = pltpu.roll(gs, kk, axis=0)
        grank = grank + jnp.where(gidx >= kk, jnp.where(r >= gs, 1.0, 0.0), jnp.where(r > gs, 1.0, 0.0))
    gsel = grank < float(TOPK_GROUPS)
    masked = [jnp.where(gsel, bj[j], ninf) for j in range(GROUP_SIZE)]
    eidx = [gidx * GROUP_SIZE + j for j in range(GROUP_SIZE)]

    selm = [jnp.zeros_like(gs) for _ in range(GROUP_SIZE)]
    e_sel = []
    for _k in range(TOP_K):
        m = masked[0]
        for j in range(1, GROUP_SIZE):
            m = jnp.maximum(m, masked[j])
        m = jnp.max(m, axis=0, keepdims=True)
        cand = jnp.where(masked[0] == m, eidx[0], N_EXPERTS)
        for j in range(1, GROUP_SIZE):
            cand = jnp.minimum(cand, jnp.where(masked[j] == m, eidx[j], N_EXPERTS))
        emin = jnp.min(cand, axis=0, keepdims=True)
        e_sel.append(emin)
        for j in range(GROUP_SIZE):
            hit = eidx[j] == emin
            selm[j] = jnp.where(hit, 1.0, selm[j])
            masked[j] = jnp.where(hit, ninf, masked[j])

    wsel = [selm[j] * sj[j] for j in range(GROUP_SIZE)]
    tot = wsel[0]
    for j in range(1, GROUP_SIZE):
        tot = tot + wsel[j]
    tot = jnp.sum(tot, axis=0, keepdims=True)
    wn = [wsel[j] / tot * ROUTED_SCALE for j in range(GROUP_SIZE)]

    tm = e_ref.shape[2]
    sel = jnp.concatenate(selm, axis=0)
    tr = lax.broadcasted_iota(jnp.int32, (ts, ts), 0)
    tc = lax.broadcasted_iota(jnp.int32, (ts, ts), 1)
    sh = tm.bit_length() - 1
    same_tile = lax.shift_right_logical(tr, sh) == lax.shift_right_logical(tc, sh)
    upper = jnp.where(tr < tc, jnp.where(same_tile, 1.0, 0.0), 0.0).astype(BF16)
    rank_full = _dot(sel.astype(BF16), upper)
    rj = [rank_full[j * N_GROUPS:(j + 1) * N_GROUPS] for j in range(GROUP_SIZE)]
    subs = [slice(i * tm, (i + 1) * tm) for i in range(ts // tm)]
    for i, sub in enumerate(subs):
        cnt_ref[i] = jnp.broadcast_to(jnp.sum(sel[:, sub], axis=1, keepdims=True), (N_EXPERTS, LANES))

    for k in range(TOP_K):
        rk = jnp.zeros_like(gs)
        wk = jnp.zeros_like(gs)
        for j in range(GROUP_SIZE):
            hit = eidx[j] == e_sel[k]
            rk = rk + jnp.where(hit, rj[j], 0.0)
            wk = wk + jnp.where(hit, wn[j], 0.0)
        rk = jnp.sum(rk, axis=0, keepdims=True).astype(jnp.int32)
        wk = jnp.sum(wk, axis=0, keepdims=True)
        for i, sub in enumerate(subs):
            e_ref[i, k:k + 1, :] = e_sel[k][:, sub]
            rank_ref[i, k:k + 1, :] = rk[:, sub]
            w_ref[i, k:k + 1, :] = wk[:, sub]


def _post_call(attn, gm, x, mod, wts, ts, tm):
    b, s, d = x.shape
    nt = s // ts
    nsub = ts // tm
    ntm = b * nt * nsub
    (goa, woa, wog, g2, wr, br, wgus, wds) = wts
    full = lambda a: pl.BlockSpec(a.shape, lambda bi, i: (0,) * a.ndim)
    tok = lambda w: pl.BlockSpec((1, ts, w), lambda bi, i: (bi, i, 0))
    rout = pl.BlockSpec((nsub, TOP_K, tm), lambda bi, i: (bi * nt + i, 0, 0))
    out_shapes = (
        jax.ShapeDtypeStruct((b, s, d), F32),
        jax.ShapeDtypeStruct((b * s, d // 2), jnp.uint32),
        jax.ShapeDtypeStruct((ntm, TOP_K, tm), jnp.int32),
        jax.ShapeDtypeStruct((ntm, TOP_K, tm), jnp.int32),
        jax.ShapeDtypeStruct((ntm, TOP_K, tm), F32),
        jax.ShapeDtypeStruct((ntm, N_EXPERTS, LANES), F32),
    )
    return pl.pallas_call(
        _post_kernel,
        name="post",
        grid=(b, nt),
        in_specs=[tok(N_HEADS * V_DIM), tok(GMLP_W), tok(d),
                  pl.BlockSpec((1, N_MOD, d), lambda bi, i: (bi, 0, 0)),
                  full(goa), full(woa), full(wog), full(g2), full(wr), full(br), full(wgus), full(wds)],
        out_specs=(tok(d), pl.BlockSpec((ts, d // 2), lambda bi, i: (bi * nt + i, 0)),
                   rout, rout, rout,
                   pl.BlockSpec((nsub, N_EXPERTS, LANES), lambda bi, i: (bi * nt + i, 0, 0))),
        out_shape=out_shapes,
        compiler_params=pltpu.CompilerParams(dimension_semantics=("arbitrary", "arbitrary"),
                                             vmem_limit_bytes=VMEM_LIMIT),
    )(attn, gm, x, mod, goa, woa, wog, g2, wr, br, wgus, wds)


def _disp_kernel(pends_ref, zfrom_ref, info_ref, lpos_ref, h_ref, xs_ref, sbuf, zbuf, nprev, sems, zsem):
    td = h_ref.shape[0]
    blk = zbuf.shape[0]
    step = pl.program_id(0)
    slot = lax.rem(step, 2)

    @pl.when(step == 0)
    def _():
        nprev[0] = 0
        zbuf[...] = jnp.zeros_like(zbuf)

        def tail_copy(e, j):
            start = pl.multiple_of(zfrom_ref[e] + j * blk, blk)
            return pltpu.make_async_copy(zbuf, xs_ref.at[pl.ds(start, blk), :], zsem)

        def n_tail(e):
            return (pends_ref[e] - zfrom_ref[e]) // blk

        def zstart(e, c):
            def one(j, c2):
                tail_copy(e, j).start()
                return c2
            return lax.fori_loop(0, n_tail(e), one, c)

        def zwait(e, c):
            def one(j, c2):
                tail_copy(e, j).wait()
                return c2
            return lax.fori_loop(0, n_tail(e), one, c)

        lax.fori_loop(0, N_EXPERTS, zstart, 0)
        lax.fori_loop(0, N_EXPERTS, zwait, 0)

    lo, hi = _unpack_bf16_pairs(h_ref[...])
    lo = lo.astype(BF16)
    hi = hi.astype(BF16)
    lpos = lpos_ref[0]
    lpos_b = [jnp.broadcast_to(lpos[k:k + 1, :], (PERM_ROWS, td)).astype(jnp.int16) for k in range(TOP_K)]
    row0 = lax.broadcasted_iota(jnp.int32, (PERM_ROWS, td), 0)
    one = jnp.ones((PERM_ROWS, td), BF16)

    def build(rb, c):
        r0 = pl.multiple_of(rb * PERM_ROWS, PERM_ROWS)
        riota = (row0 + r0).astype(jnp.int16)
        pb = jnp.zeros((PERM_ROWS, td), BF16)
        for k in range(TOP_K):
            pb = jnp.where(lpos_b[k] == riota, one, pb)
        xlo = pltpu.bitcast(_dot(pb, lo), jnp.uint32)
        xhi = pltpu.bitcast(_dot(pb, hi), jnp.uint32)
        sbuf[slot, pl.ds(r0, PERM_ROWS), :] = (xhi & jnp.uint32(0xFFFF0000)) | (xlo >> 16)
        return c

    lax.fori_loop(0, info_ref[0, 2, 1], build, 0)

    def chunk_copy(src, dst, sl):
        return pltpu.make_async_copy(
            sbuf.at[sl, pl.ds(pl.multiple_of(src, ROW_ALIGN), DISP_CHUNK), :],
            xs_ref.at[pl.ds(pl.multiple_of(dst, ROW_ALIGN), DISP_CHUNK), :], sems.at[sl])

    def drain(n, sl):
        def one(c, carry):
            chunk_copy(0, 0, sl).wait()
            return carry
        lax.fori_loop(0, n, one, 0)

    def issue(c, carry):
        chunk_copy(info_ref[0, 0, c], info_ref[0, 1, c], slot).start()
        return carry

    drain(nprev[0], 1 - slot)
    n_chunks = info_ref[0, 2, 0]
    lax.fori_loop(0, n_chunks, issue, 0)
    nprev[0] = n_chunks

    @pl.when(step == pl.num_programs(0) - 1)
    def _():
        drain(n_chunks, slot)


def _sorted_rows(td):
    need = TOP_K * td + N_EXPERTS * (ROW_ALIGN - 1) + DISP_CHUNK
    return -(-need // PERM_ROWS) * PERM_ROWS


def _disp_call(pends, zfrom, info, lpos, h2p, n_rows, td, blk):
    t, w = h2p.shape
    nt = t // td
    return pl.pallas_call(
        _disp_kernel,
        name="disp",
        grid_spec=pltpu.PrefetchScalarGridSpec(
            num_scalar_prefetch=2,
            grid=(nt,),
            in_specs=[pl.BlockSpec((1,) + info.shape[1:], lambda i, pe, pa: (i, 0, 0), memory_space=pltpu.SMEM),
                      pl.BlockSpec((1, TOP_K, td), lambda i, pe, pa: (i, 0, 0)),
                      pl.BlockSpec((td, w), lambda i, pe, pa: (i, 0))],
            out_specs=pl.BlockSpec(memory_space=pl.ANY),
            scratch_shapes=[pltpu.VMEM((2, _sorted_rows(td), w), jnp.uint32),
                            pltpu.VMEM((blk, w), jnp.uint32),
                            pltpu.SMEM((1,), jnp.int32),
                            pltpu.SemaphoreType.DMA((2,)), pltpu.SemaphoreType.DMA(())],
        ),
        out_shape=jax.ShapeDtypeStruct((n_rows, w), jnp.uint32),
        compiler_params=pltpu.CompilerParams(dimension_semantics=("arbitrary",),
                                             vmem_limit_bytes=VMEM_LIMIT),
    )(pends, zfrom, info, lpos, h2p)


def _exp_kernel(bexp_ref, nused_ref, xs_ref, wgu_ref, wd_ref, ys_ref):
    i = pl.program_id(0)

    @pl.when(i < nused_ref[0])
    def _():
        half = D_MODEL // 2
        lo, hi = _unpack_bf16_pairs(xs_ref[...])
        gu = (_dot(lo.astype(BF16), wgu_ref[0, :half, :])
              + _dot(hi.astype(BF16), wgu_ref[0, half:, :]))
        g, u = gu[:, :EXPERT_DIM], gu[:, EXPERT_DIM:]
        a = (g * _sigmoid(g) * u).astype(BF16)
        ys_ref[...] = _dot(a, wd_ref[0]).astype(BF16)


def _exp_call(block_exp, n_used, xs, wgu, wd, blk):
    n_rows, w = xs.shape
    n_blocks = n_rows // blk

    def row_map(i, bexp, nused):
        return (jnp.minimum(i, nused[0] - 1), 0)

    def w_map(i, bexp, nused):
        return (bexp[jnp.minimum(i, nused[0] - 1)], 0, 0)

    return pl.pallas_call(
        _exp_kernel,
        name="exp",
        grid_spec=pltpu.PrefetchScalarGridSpec(
            num_scalar_prefetch=2,
            grid=(n_blocks,),
            in_specs=[pl.BlockSpec((blk, w), row_map),
                      pl.BlockSpec((1, D_MODEL, 2 * EXPERT_DIM), w_map),
                      pl.BlockSpec((1, EXPERT_DIM, D_MODEL), w_map)],
            out_specs=pl.BlockSpec((blk, D_MODEL), row_map),
        ),
        out_shape=jax.ShapeDtypeStruct((n_rows, D_MODEL), BF16),
        compiler_params=pltpu.CompilerParams(dimension_semantics=("arbitrary",),
                                             vmem_limit_bytes=VMEM_LIMIT),
    )(block_exp, n_used, xs, wgu, wd)


def _comb_kernel(info_ref, next_info_ref, ybase_ref, mod_ref, cpos_ref, w_ref, gf_ref, ys_ref, o_ref,
                 ybuf, acc, sems):
    tc = ybase_ref.shape[1]
    step = pl.program_id(0) * pl.num_programs(1) + pl.program_id(1)
    n_steps = pl.num_programs(0) * pl.num_programs(1)
    slot = lax.rem(step, 2)

    def chunk_copy(src, dst, sl, n_tiles):
        return pltpu.make_async_copy(ys_ref.at[pl.ds(src, n_tiles)], ybuf.at[sl, pl.ds(dst, n_tiles)],
                                     sems.at[sl])

    def fetch(iref, sl):
        for i, n_tiles in enumerate(COMB_CHUNK_TILES):
            def issue(c, carry):
                chunk_copy(iref[0, 2 * i, c], iref[0, 2 * i + 1, c], sl, n_tiles).start()
                return carry
            lax.fori_loop(0, iref[0, 2 * len(COMB_CHUNK_TILES), i], issue, 0)

    @pl.when(step == 0)
    def _():
        ybuf[...] = jnp.zeros_like(ybuf)
        fetch(info_ref, 0)

    @pl.when(step + 1 < n_steps)
    def _():
        fetch(next_info_ref, 1 - slot)

    for i, n_tiles in enumerate(COMB_CHUNK_TILES):
        def drain(c, carry):
            chunk_copy(0, 0, slot, n_tiles).wait()
            return carry
        lax.fori_loop(0, info_ref[0, 2 * len(COMB_CHUNK_TILES), i], drain, 0)

    cpos = cpos_ref[...]
    w = w_ref[...]
    cpos_b = [jnp.broadcast_to(cpos[:, k:k + 1], (tc, LANES)).astype(jnp.int16) for k in range(TOP_K)]
    w_b = [jnp.broadcast_to(w[:, k:k + 1], (tc, LANES)).astype(BF16) for k in range(TOP_K)]
    lane = lax.broadcasted_iota(jnp.int32, (tc, LANES), 1)
    acc[...] = jnp.zeros_like(acc)

    def slab(kt, c):
        k0 = pl.multiple_of(kt * COMB_KTILE, COMB_KTILE)
        cols = []
        for j in range(COMB_KTILE // LANES):
            col = (lane + (k0 + j * LANES)).astype(jnp.int16)
            wm = jnp.zeros((tc, LANES), BF16)
            for k in range(TOP_K):
                wm = jnp.where(cpos_b[k] == col, w_b[k], wm)
            cols.append(wm)
        rows = ybuf[slot, pl.ds(kt * (COMB_KTILE // COMB_CHUNK), COMB_KTILE // COMB_CHUNK)]
        acc[...] += _dot(jnp.concatenate(cols, axis=1), rows.reshape(COMB_KTILE, rows.shape[-1]))
        return c

    lax.fori_loop(0, info_ref[0, 2 * len(COMB_CHUNK_TILES), len(COMB_CHUNK_TILES)], slab, 0)
    ga2 = mod_ref[0][5:6]
    o_ref[0] = _rms(ybase_ref[0] + ga2 * acc[...], gf_ref[...])


def _gather_rows(tc):
    need = TOP_K * tc + N_EXPERTS * 2 * (COMB_CHUNK - 1)
    return -(-need // COMB_KTILE) * COMB_KTILE


def _comb_call(info, ybase, mod, cpos_tok, w_tok, g_final, ys, tc):
    b, s, d = ybase.shape
    nt = s // tc
    return pl.pallas_call(
        _comb_kernel,
        name="comb",
        grid=(b, nt),
        in_specs=[pl.BlockSpec((1,) + info.shape[1:], lambda bi, i: (bi * nt + i, 0, 0), memory_space=pltpu.SMEM),
                  pl.BlockSpec((1,) + info.shape[1:], lambda bi, i: (jnp.minimum(bi * nt + i + 1, b * nt - 1), 0, 0),
                               memory_space=pltpu.SMEM),
                  pl.BlockSpec((1, tc, d), lambda bi, i: (bi, i, 0)),
                  pl.BlockSpec((1, N_MOD, d), lambda bi, i: (bi, 0, 0)),
                  pl.BlockSpec((tc, TOP_K), lambda bi, i: (bi * nt + i, 0)),
                  pl.BlockSpec((tc, TOP_K), lambda bi, i: (bi * nt + i, 0)),
                  pl.BlockSpec((1, d), lambda bi, i: (0, 0)),
                  pl.BlockSpec(memory_space=pl.ANY)],
        out_specs=pl.BlockSpec((1, tc, d), lambda bi, i: (bi, i, 0)),
        out_shape=jax.ShapeDtypeStruct((b, s, d), F32),
        scratch_shapes=[pltpu.VMEM((2, _gather_rows(tc) // COMB_CHUNK, COMB_CHUNK, d), BF16),
                        pltpu.VMEM((tc, d), F32), pltpu.SemaphoreType.DMA((2,))],
        compiler_params=pltpu.CompilerParams(dimension_semantics=("arbitrary", "arbitrary"),
                                             vmem_limit_bytes=VMEM_LIMIT),
    )(info, info, ybase, mod, cpos_tok, w_tok, g_final, ys)


def _prep_weights(w_ada, b_ada, g_norm1, w_in, g_q_lat, w_uq, g_kv_lat, w_ukv, g_gmlp_v, w_spatial,
                  b_spatial, g_out_attn, g_out_gmlp, w_out, g_norm2, w_router, b_router, w_gate_e,
                  w_up_e, w_down_e, w_gate_s, w_up_s, w_down_s, g_final):
    row = lambda g: g.reshape(1, -1).astype(F32)
    o1 = Q_LORA
    o2 = o1 + KV_LORA
    o3 = o2 + QK_ROPE
    o4 = o3 + GMLP_W
    kr_cols = jnp.pad(w_in[:, o2:o3], ((0, 0), (QK_NOPE, LANES - QK_NOPE - QK_ROPE)))
    win = jnp.concatenate([w_in[:, :o2], w_in[:, o3:o4], w_in[:, o4:], kr_cols], axis=1).astype(BF16)
    qd = QK_NOPE + QK_ROPE
    wuq = jnp.pad(w_uq.reshape(Q_LORA, N_HEADS, qd), ((0, 0), (0, 0), (0, HEAD_PAD - qd)))
    wuq = wuq.reshape(Q_LORA, N_HEADS * HEAD_PAD).astype(BF16)
    wkv = w_ukv.reshape(KV_LORA, N_HEADS, QK_NOPE + V_DIM)
    wuk = jnp.pad(wkv[:, :, :QK_NOPE], ((0, 0), (0, 0), (0, HEAD_PAD - QK_NOPE)))
    wuk = wuk.reshape(KV_LORA, N_HEADS * HEAD_PAD).astype(BF16)
    wuv = jnp.pad(wkv[:, :, QK_NOPE:], ((0, 0), (0, 0), (0, HEAD_PAD - V_DIM)))
    wuv = wuv.reshape(KV_LORA, N_HEADS * HEAD_PAD).astype(BF16)
    vone = jnp.tile((jnp.arange(HEAD_PAD) == V_DIM).astype(F32), N_HEADS).reshape(1, -1)
    wsp = w_spatial.reshape(N_HEADS // 2, 2, CHUNK, CHUNK).transpose(0, 2, 1, 3)
    wsp = wsp.reshape(N_HEADS // 2, CHUNK, 2 * CHUNK).astype(BF16)
    bsp = jnp.repeat(jnp.transpose(b_spatial), GMLP_W // N_HEADS, axis=1).astype(F32)
    pre = (row(g_norm1), win, row(g_q_lat), wuq, row(g_kv_lat), wuk, wuv, vone, row(g_gmlp_v), wsp,
           bsp, row(g_out_gmlp))
    perm = (jnp.arange(N_GROUPS)[None, :] * GROUP_SIZE + jnp.arange(GROUP_SIZE)[:, None]).reshape(-1)
    wr = jnp.transpose(w_router)[perm].astype(BF16)
    br = b_router.astype(F32)[perm].reshape(N_EXPERTS, 1)
    mla_w = N_HEADS * V_DIM
    wgus = jnp.concatenate([w_gate_s, w_up_s], axis=1).astype(BF16)
    post = (row(g_out_attn), w_out[:mla_w].astype(BF16), w_out[mla_w:].astype(BF16), row(g_norm2),
            wr, br, wgus, w_down_s.astype(BF16))
    wgu_e = jnp.concatenate([w_gate_e, w_up_e], axis=2).astype(BF16)
    wd_e = w_down_e.astype(BF16)
    return w_ada.astype(BF16), b_ada, pre, post, (wgu_e, wd_e), row(g_final)


def _rope_tables(s):
    inv = 1.0 / (ROPE_THETA ** (jnp.arange(0, QK_ROPE, 2, dtype=F32) / QK_ROPE))
    ang = jnp.arange(s, dtype=F32)[:, None] * inv[None, :]
    cos, sin = jnp.cos(ang), jnp.sin(ang)
    z = lambda n: jnp.zeros((s, n), F32)
    tail = LANES - QK_NOPE - QK_ROPE
    c = jnp.concatenate([jnp.ones((s, QK_NOPE), F32), cos, cos, z(tail)], axis=1)
    s1 = jnp.concatenate([z(QK_NOPE), -sin, z(HALF_ROPE), z(tail)], axis=1)
    s2 = jnp.concatenate([z(QK_NOPE), z(HALF_ROPE), sin, z(tail)], axis=1)
    return c, s1, s2


def _tiles(s):
    ts = min(1024, s)
    tq = min(512, s)
    tkc = min(1024, s)
    blk = 1024
    tm = min(256, s)
    hps = 8 if s * 8 * HEAD_PAD * 2 * 2 <= VMEM_LIMIT // 4 else 2
    return ts, tq, tkc, blk, tm, hps


def _ceil_to(x, m):
    return (x + m - 1) // m * m


def _trunk(x, c, prep, tiles=None):
    w_ada, b_ada, pre_w, post_w, exp_w, g_final = prep
    b, s, d = x.shape
    ts, tq, tkc, blk, tm, hps = tiles or _tiles(s)
    t = b * s
    nt = t // tm
    mod = _modulation(c, w_ada, b_ada)
    q, k, v, gm = _pre_call(x, mod, _rope_tables(s), pre_w, ts)
    attn = _attn_call(q, k, v, tq, tkc, hps)
    ybase, h2p, e_arr, lrank, w_arr, cnt = _post_call(attn, gm, x, mod, post_w, min(ts, 512), tm)

    i32 = jnp.int32
    cnt = cnt[:, :, 0].astype(i32).reshape(nt, GROUP_SIZE, N_GROUPS).transpose(0, 2, 1).reshape(nt, N_EXPERTS)
    cnt8 = _ceil_to(cnt, ROW_ALIGN)
    base8 = jnp.cumsum(cnt8, axis=0) - cnt8
    total8 = jnp.sum(cnt8, axis=0)
    padded = _ceil_to(total8 + EXPERT_SLACK, blk)
    pends = jnp.cumsum(padded).astype(i32)
    zfrom = ((pends - padded + total8) // blk * blk).astype(i32)
    dstbase = (pends - padded)[None, :] + base8
    toff8 = jnp.cumsum(cnt8, axis=1) - cnt8
    eids = jnp.arange(N_EXPERTS, dtype=i32)

    def chunk_table(nch, chunk, n_max):
        cend = jnp.cumsum(nch, axis=1)
        cidx = jnp.arange(n_max, dtype=i32)
        e_of_c = jnp.minimum(jnp.sum((cend[:, None, :] <= cidx[None, :, None]).astype(i32), axis=-1),
                             N_EXPERTS - 1)
        pick = lambda tbl: jnp.sum(jnp.where(e_of_c[..., None] == eids, tbl[:, None, :], 0), axis=-1)
        rel = lambda first_row: pick(first_row - (cend - nch) * chunk) + cidx[None, :] * chunk
        return rel, cend[:, -1]

    n_dmax = N_EXPERTS + TOP_K * tm // DISP_CHUNK
    nch_d = (cnt + DISP_CHUNK - 1) // DISP_CHUNK
    rel_d, n_dch = chunk_table(nch_d, DISP_CHUNK, n_dmax)
    n_rb = (jnp.sum(cnt8, axis=1) + DISP_CHUNK + PERM_ROWS - 1) // PERM_ROWS
    tail = lambda a, b2, n: jnp.concatenate([a[:, None], b2[:, None], jnp.zeros((nt, n - 2), i32)], axis=1)
    dinfo = jnp.stack([rel_d(toff8), rel_d(dstbase), tail(n_dch, n_rb, n_dmax)], axis=1).astype(i32)

    shift = dstbase % COMB_CHUNK
    nch_c = jnp.where(cnt > 0, (cnt + shift + COMB_CHUNK - 1) // COMB_CHUNK, 0)
    boff = (jnp.cumsum(nch_c, axis=1) - nch_c) * COMB_CHUNK
    big = COMB_CHUNK_TILES[0]
    n_cmax = max(_gather_rows(tm) // COMB_CHUNK // big, N_EXPERTS)
    src_t = (dstbase - shift) // COMB_CHUNK
    dst_t = boff // COMB_CHUNK
    done = nch_c // big * big
    rows, totals = [], []
    for size in COMB_CHUNK_TILES:
        n_size = nch_c // big if size == big else (nch_c - done == size).astype(i32)
        first = 0 if size == big else done
        rel, total = chunk_table(n_size, size, n_cmax)
        rows += [rel(src_t + first), rel(dst_t + first)]
        totals.append(total[:, None])
    n_kt = (jnp.sum(nch_c, axis=1) * COMB_CHUNK + COMB_KTILE - 1) // COMB_KTILE
    counts = jnp.concatenate(totals + [n_kt[:, None], jnp.zeros((nt, n_cmax - len(totals) - 1), i32)], axis=1)
    cinfo = jnp.stack(rows + [counts], axis=1).astype(i32)

    onehot = e_arr[..., None] == eids
    lookup = lambda tbl: jnp.sum(jnp.where(onehot, tbl[:, None, None, :], 0), axis=-1)
    lpos = lookup(toff8) + lrank
    cpos = lookup(boff + shift) + lrank
    tok_major = lambda a: jnp.transpose(a, (0, 2, 1)).reshape(t, TOP_K)

    n_rows = _ceil_to(t * TOP_K + nt * N_EXPERTS * (ROW_ALIGN - 1) + N_EXPERTS * (EXPERT_SLACK + blk - 1), blk)
    n_blocks = n_rows // blk
    n_used = (pends[-1] // blk).astype(i32).reshape(1)
    block_start = jnp.arange(n_blocks, dtype=i32) * blk
    block_exp = jnp.minimum(jnp.sum((pends[None, :] <= block_start[:, None]).astype(i32), axis=1),
                            N_EXPERTS - 1)

    xs = _disp_call(pends, zfrom, dinfo, lpos.astype(i32), h2p, n_rows, tm, blk)
    ys = _exp_call(block_exp, n_used, xs, exp_w[0], exp_w[1], blk)
    ys3 = ys.reshape(n_rows // COMB_CHUNK, COMB_CHUNK, d)
    return _comb_call(cinfo, ybase, mod, tok_major(cpos).astype(i32), tok_major(w_arr), g_final, ys3, tm)


def kernel(x_prompt, x_sample, c_prompt, c_sample, w_ada, b_ada, g_norm1, w_in, g_q_lat, w_uq, g_kv_lat, w_ukv, g_gmlp_v, w_spatial, b_spatial, g_out_attn, g_out_gmlp, w_out, g_norm2, w_router, b_router, w_gate_e, w_up_e, w_down_e, w_gate_s, w_up_s, w_down_s, g_final):
    prep = _prep_weights(w_ada[0], b_ada[0], g_norm1[0], w_in[0], g_q_lat[0], w_uq[0], g_kv_lat[0],
                         w_ukv[0], g_gmlp_v[0], w_spatial[0], b_spatial[0], g_out_attn[0],
                         g_out_gmlp[0], w_out[0], g_norm2[0], w_router[0], b_router[0], w_gate_e[0],
                         w_up_e[0], w_down_e[0], w_gate_s[0], w_up_s[0], w_down_s[0], g_final)
    return (_trunk(x_prompt, c_prompt, prep), _trunk(x_sample, c_sample, prep))
```

```python
import functools
import math

import jax
import jax.numpy as jnp
from jax import lax
from jax.experimental import pallas as pl
from jax.experimental.pallas import tpu as pltpu

F32 = jnp.float32
BF16 = jnp.bfloat16

D_MODEL = 1024
N_HEADS = 8
QK_NOPE = 64
QK_ROPE = 32
V_DIM = 64
Q_LORA = 256
KV_LORA = 128
GMLP_W = 512
CHUNK = 128
N_EXPERTS = 64
TOP_K = 8
N_GROUPS = 8
TOPK_GROUPS = 4
GROUP_SIZE = N_EXPERTS // N_GROUPS
EXPERT_DIM = 256
SHARED_DIM = 256
ROUTED_SCALE = 2.5
ROPE_THETA = 10000.0
N_MOD = 6
EPS = 1e-6

LANES = 128
HEAD_PAD = 128
HALF_ROPE = QK_ROPE // 2
VMEM_LIMIT = 52 * 1024 * 1024
ROW_ALIGN = 8
DISP_CHUNK = 48
COMB_CHUNK = 16
COMB_CHUNK_TILES = (3, 2, 1)
ATTN_CHUNKS_PER_STEP = 8
PERM_ROWS = 512
COMB_KTILE = 1024
EXPERT_SLACK = max(DISP_CHUNK, 2 * (COMB_CHUNK - 1))

SOFTMAX_SCALE = (QK_NOPE + QK_ROPE) ** -0.5
EXP2_SCALE = SOFTMAX_SCALE * math.log2(math.e)


def _rms(x, g):
    return x * lax.rsqrt(jnp.mean(x * x, axis=-1, keepdims=True) + EPS) * g


def _sigmoid(x):
    return 1.0 / (1.0 + jnp.exp(-x))


def _gelu_tanh(x):
    c = math.sqrt(2.0 / math.pi)
    return 0.5 * x * (1.0 + jnp.tanh(c * (x + 0.044715 * (x * x * x))))


def _dot(a, b):
    return jnp.dot(a, b, preferred_element_type=F32)


def _dot_nt(a, b):
    return lax.dot_general(a, b, (((1,), (1,)), ((), ())), preferred_element_type=F32)


def _pack_bf16_pairs(x):
    c = x.shape[1] // 2
    lo = pltpu.bitcast(x[:, :c].astype(BF16).astype(F32), jnp.uint32)
    hi = pltpu.bitcast(x[:, c:].astype(BF16).astype(F32), jnp.uint32)
    return (hi & jnp.uint32(0xFFFF0000)) | (lo >> 16)


def _unpack_bf16_pairs(w):
    lo = pltpu.bitcast(w << 16, F32)
    hi = pltpu.bitcast(w & jnp.uint32(0xFFFF0000), F32)
    return lo, hi


def _mod_kernel(c_ref, w_ref, b_ref, o_ref):
    c = c_ref[...]
    a = (c * _sigmoid(c)).astype(BF16)
    o_ref[...] = _dot(a, w_ref[...]) + b_ref[...]


def _modulation(c, w_ada_bf, b_ada):
    b = c.shape[0]
    bp = max(16, -(-b // 16) * 16)
    cp = jnp.pad(c, ((0, bp - b), (0, 0)))
    n = w_ada_bf.shape[1]
    tn = D_MODEL
    out = pl.pallas_call(
        _mod_kernel,
        name="mod",
        grid=(n // tn,),
        in_specs=[
            pl.BlockSpec((bp, D_MODEL), lambda j: (0, 0)),
            pl.BlockSpec((D_MODEL, tn), lambda j: (0, j)),
            pl.BlockSpec((1, tn), lambda j: (0, j)),
        ],
        out_specs=pl.BlockSpec((bp, tn), lambda j: (0, j)),
        out_shape=jax.ShapeDtypeStruct((bp, n), F32),
        compiler_params=pltpu.CompilerParams(dimension_semantics=("arbitrary",)),
    )(cp, w_ada_bf, b_ada.reshape(1, n))
    return out[:b].reshape(b, N_MOD, D_MODEL)


def _rope(xh, c, s1, s2):
    return (xh * c + pltpu.roll(xh, LANES - HALF_ROPE, axis=1) * s1
            + pltpu.roll(xh, HALF_ROPE, axis=1) * s2)


def _pre_kernel(x_ref, mod_ref, cos_ref, s1_ref, s2_ref, g1_ref, win_ref, gq_ref, wuq_ref,
                gkv_ref, wuk_ref, wuv_ref, vone_ref, ggv_ref, ws_ref, bs_ref, ggo_ref,
                q_ref, k_ref, v_ref, gm_ref, mix_ref):
    ts = x_ref.shape[1]
    x = x_ref[0]
    mod = mod_ref[0]
    h = _rms(x, g1_ref[...]) * (1.0 + mod[1:2]) + mod[0:1]
    z = _dot(h.astype(BF16), win_ref[...])
    o_kv = Q_LORA
    o_gu = o_kv + KV_LORA
    o_gv = o_gu + GMLP_W
    o_kr = o_gv + GMLP_W
    q_lat = z[:, :o_kv]
    kv_lat = z[:, o_kv:o_gu]
    g_u = z[:, o_gu:o_gv]
    g_v = z[:, o_gv:o_kr]
    kr = z[:, o_kr:o_kr + LANES]

    cos = cos_ref[...]
    s1 = s1_ref[...]
    s2 = s2_ref[...]

    qn = _rms(q_lat, gq_ref[...]).astype(BF16)
    q = _dot(qn, wuq_ref[...])
    kn = _rms(kv_lat, gkv_ref[...]).astype(BF16)
    kf = _dot(kn, wuk_ref[...])
    v_ref[0] = (_dot(kn, wuv_ref[...]) + vone_ref[...]).astype(BF16)
    krr = _rope(kr, cos, s1, s2)
    for hd in range(N_HEADS):
        sl = slice(hd * HEAD_PAD, (hd + 1) * HEAD_PAD)
        q_ref[0, :, sl] = (_rope(q[:, sl], cos, s1, s2) * EXP2_SCALE).astype(BF16)
        k_ref[0, sl, :] = jnp.transpose(kf[:, sl] + krr).astype(BF16)

    u = _gelu_tanh(g_u)
    vn = _rms(_gelu_tanh(g_v), ggv_ref[...]).astype(BF16)
    lane = lax.broadcasted_iota(jnp.int32, (CHUNK, LANES), 1)
    left = lane < (LANES // 2)
    zero = jnp.zeros((CHUNK, LANES), BF16)
    for n in range(ts // CHUNK):
        rs = slice(n * CHUNK, (n + 1) * CHUNK)
        for p in range(GMLP_W // LANES):
            cs = slice(p * LANES, (p + 1) * LANES)
            vp = vn[rs, cs]
            rhs = jnp.concatenate([jnp.where(left, vp, zero), jnp.where(left, zero, vp)], axis=0)
            mix_ref[rs, cs] = _dot(ws_ref[p], rhs) + bs_ref[:, cs]
    gm = u * mix_ref[...]
    gm_ref[0] = _rms(gm, ggo_ref[...]).astype(BF16)


def _pre_call(x, mod, tabs, wts, ts):
    b, s, d = x.shape
    cos, s1, s2 = tabs
    (g1, win, gq, wuq, gkv, wuk, wuv, vone, ggv, wsp, bsp, ggo) = wts
    full = lambda a: pl.BlockSpec(a.shape, lambda bi, i: (0,) * a.ndim)
    tab = pl.BlockSpec((ts, LANES), lambda bi, i: (i, 0))
    out_shapes = (
        jax.ShapeDtypeStruct((b, s, N_HEADS * HEAD_PAD), BF16),
        jax.ShapeDtypeStruct((b, N_HEADS * HEAD_PAD, s), BF16),
        jax.ShapeDtypeStruct((b, s, N_HEADS * HEAD_PAD), BF16),
        jax.ShapeDtypeStruct((b, s, GMLP_W), BF16),
    )
    tok = lambda w: pl.BlockSpec((1, ts, w), lambda bi, i: (bi, i, 0))
    tok_t = pl.BlockSpec((1, N_HEADS * HEAD_PAD, ts), lambda bi, i: (bi, 0, i))
    return pl.pallas_call(
        _pre_kernel,
        name="pre",
        grid=(b, s // ts),
        in_specs=[tok(d), pl.BlockSpec((1, N_MOD, d), lambda bi, i: (bi, 0, 0)), tab, tab, tab,
                  full(g1), full(win), full(gq), full(wuq), full(gkv), full(wuk), full(wuv), full(vone),
                  full(ggv), full(wsp), full(bsp), full(ggo)],
        out_specs=(tok(N_HEADS * HEAD_PAD), tok_t, tok(N_HEADS * HEAD_PAD), tok(GMLP_W)),
        out_shape=out_shapes,
        scratch_shapes=[pltpu.VMEM((ts, GMLP_W), F32)],
        compiler_params=pltpu.CompilerParams(dimension_semantics=("arbitrary", "arbitrary"),
                                             vmem_limit_bytes=VMEM_LIMIT),
    )(x, mod, cos, s1, s2, g1, win, gq, wuq, gkv, wuk, wuv, vone, ggv, wsp, bsp, ggo)


def _attn_kernel(q_ref, k_ref, v_ref, o_ref, *, tkc, cpb):
    tq = q_ref.shape[1]
    n_chunks = v_ref.shape[1] // tkc
    hps = q_ref.shape[2] // HEAD_PAD
    heads = [slice(hh * HEAD_PAD, (hh + 1) * HEAD_PAD) for hh in range(hps)]
    qs = [q_ref[0, :, hs] for hs in heads]

    def step(i, carry):
        carry = list(carry)
        for cc in range(cpb):
            off = pl.multiple_of((i * cpb + cc) * tkc, tkc)
            for hh, hs in enumerate(heads):
                m, acc = carry[2 * hh], carry[2 * hh + 1]
                kc = k_ref[0, hs, pl.ds(off, tkc)]
                vc = v_ref[0, pl.ds(off, tkc), hs]
                sc = _dot(qs[hh], kc)
                m_new = jnp.maximum(m, jnp.max(sc, axis=-1, keepdims=True))
                p = jnp.exp2(sc - m_new).astype(BF16)
                carry[2 * hh + 1] = jnp.exp2(m - m_new) * acc + _dot(p, vc)
                carry[2 * hh] = m_new
        return tuple(carry)

    n_steps = n_chunks // cpb
    init = (jnp.full((tq, 1), -jnp.inf, F32), jnp.zeros((tq, HEAD_PAD), F32)) * hps
    res = step(0, init) if n_steps == 1 else lax.fori_loop(0, n_steps, step, init)
    lane = lax.broadcasted_iota(jnp.int32, (tq, HEAD_PAD), 1)
    for pair in range(hps // 2):
        a0, a1 = res[4 * pair + 1], res[4 * pair + 3]
        o0 = a0 / a0[:, V_DIM:V_DIM + 1]
        o1 = a1 / a1[:, V_DIM:V_DIM + 1]
        o_ref[0, :, pair * HEAD_PAD:(pair + 1) * HEAD_PAD] = jnp.where(
            lane < V_DIM, o0, pltpu.roll(o1, V_DIM, axis=1)).astype(BF16)


def _attn_call(q, k, v, tq, tkc, hps):
    b, s, _ = q.shape
    k_res = pl.BlockSpec((1, hps * HEAD_PAD, s), lambda bi, h, i: (bi, h, 0), pipeline_mode=pl.Buffered(1))
    v_res = pl.BlockSpec((1, s, hps * HEAD_PAD), lambda bi, h, i: (bi, 0, h), pipeline_mode=pl.Buffered(1))
    return pl.pallas_call(
        functools.partial(_attn_kernel, tkc=tkc, cpb=math.gcd(s // tkc, ATTN_CHUNKS_PER_STEP)),
        name="attn",
        grid=(b, N_HEADS // hps, s // tq),
        in_specs=[pl.BlockSpec((1, tq, hps * HEAD_PAD), lambda bi, h, i: (bi, i, h)), k_res, v_res],
        out_specs=pl.BlockSpec((1, tq, hps * V_DIM), lambda bi, h, i: (bi, i, h)),
        out_shape=jax.ShapeDtypeStruct((b, s, N_HEADS * V_DIM), BF16),
        compiler_params=pltpu.CompilerParams(
            dimension_semantics=("arbitrary", "arbitrary", "arbitrary"),
            vmem_limit_bytes=VMEM_LIMIT),
    )(q, k, v)


def _post_kernel(attn_ref, gm_ref, x_ref, mod_ref, goa_ref, woa_ref, wog_ref, g2_ref, wr_ref,
                 br_ref, wgus_ref, wds_ref,
                 ybase_ref, h2p_ref, e_ref, rank_ref, w_ref, cnt_ref):
    ts = x_ref.shape[1]
    mod = mod_ref[0]
    ga1, sh2, sc2, ga2 = mod[2:3], mod[3:4], mod[4:5], mod[5:6]
    an = _rms(attn_ref[0].astype(F32), goa_ref[...]).astype(BF16)
    y = _dot(an, woa_ref[...]) + _dot(gm_ref[0], wog_ref[...])
    x1 = x_ref[0] + ga1 * y
    h2 = _rms(x1, g2_ref[...]) * (1.0 + sc2) + sh2
    h2b = h2.astype(BF16)
    h2p_ref[...] = _pack_bf16_pairs(h2)

    gu = _dot(h2b, wgus_ref[...])
    g, u = gu[:, :SHARED_DIM], gu[:, SHARED_DIM:]
    a = (g * _sigmoid(g) * u).astype(BF16)
    ybase_ref[0] = x1 + ga2 * _dot(a, wds_ref[...])

    logits = _dot_nt(wr_ref[...], h2b)
    scores = _sigmoid(logits)
    biased = scores + br_ref[...]
    ninf = jnp.float32(-jnp.inf)
    bj = [biased[j * N_GROUPS:(j + 1) * N_GROUPS] for j in range(GROUP_SIZE)]
    sj = [scores[j * N_GROUPS:(j + 1) * N_GROUPS] for j in range(GROUP_SIZE)]
    m1 = bj[0]
    for j in range(1, GROUP_SIZE):
        m1 = jnp.maximum(m1, bj[j])
    found = jnp.zeros_like(m1)
    m2 = jnp.full_like(m1, ninf)
    for j in range(GROUP_SIZE):
        eq = jnp.where(bj[j] == m1, 1.0, 0.0)
        is_first = eq * (1.0 - found)
        found = jnp.maximum(found, eq)
        m2 = jnp.maximum(m2, jnp.where(is_first > 0.0, ninf, bj[j]))
    gs = m1 + m2
    gidx = lax.broadcasted_iota(jnp.int32, gs.shape, 0)
    grank = jnp.zeros_like(gs)
    for kk in range(1, N_GROUPS):
        r = pltpu.roll(gs, kk, axis=0)
        grank = grank + jnp.where(gidx >= kk, jnp.where(r >= gs, 1.0, 0.0), jnp.where(r > gs, 1.0, 0.0))
    gsel = grank < float(TOPK_GROUPS)
    masked = [jnp.where(gsel, bj[j], ninf) for j in range(GROUP_SIZE)]
    eidx = [gidx * GROUP_SIZE + j for j in range(GROUP_SIZE)]

    selm = [jnp.zeros_like(gs) for _ in range(GROUP_SIZE)]
    e_sel = []
    for _k in range(TOP_K):
        m = masked[0]
        for j in range(1, GROUP_SIZE):
            m = jnp.maximum(m, masked[j])
        m = jnp.max(m, axis=0, keepdims=True)
        cand = jnp.where(masked[0] == m, eidx[0], N_EXPERTS)
        for j in range(1, GROUP_SIZE):
            cand = jnp.minimum(cand, jnp.where(masked[j] == m, eidx[j], N_EXPERTS))
        emin = jnp.min(cand, axis=0, keepdims=True)
        e_sel.append(emin)
        for j in range(GROUP_SIZE):
            hit = eidx[j] == emin
            selm[j] = jnp.where(hit, 1.0, selm[j])
            masked[j] = jnp.where(hit, ninf, masked[j])

    wsel = [selm[j] * sj[j] for j in range(GROUP_SIZE)]
    tot = wsel[0]
    for j in range(1, GROUP_SIZE):
        tot = tot + wsel[j]
    tot = jnp.sum(tot, axis=0, keepdims=True)
    wn = [wsel[j] / tot * ROUTED_SCALE for j in range(GROUP_SIZE)]

    tm = e_ref.shape[2]
    sel = jnp.concatenate(selm, axis=0)
    tr = lax.broadcasted_iota(jnp.int32, (ts, ts), 0)
    tc = lax.broadcasted_iota(jnp.int32, (ts, ts), 1)
    sh = tm.bit_length() - 1
    same_tile = lax.shift_right_logical(tr, sh) == lax.shift_right_logical(tc, sh)
    upper = jnp.where(tr < tc, jnp.where(same_tile, 1.0, 0.0), 0.0).astype(BF16)
    rank_full = _dot(sel.astype(BF16), upper)
    rj = [rank_full[j * N_GROUPS:(j + 1) * N_GROUPS] for j in range(GROUP_SIZE)]
    subs = [slice(i * tm, (i + 1) * tm) for i in range(ts // tm)]
    for i, sub in enumerate(subs):
        cnt_ref[i] = jnp.broadcast_to(jnp.sum(sel[:, sub], axis=1, keepdims=True), (N_EXPERTS, LANES))

    for k in range(TOP_K):
        rk = jnp.zeros_like(gs)
        wk = jnp.zeros_like(gs)
        for j in range(GROUP_SIZE):
            hit = eidx[j] == e_sel[k]
            rk = rk + jnp.where(hit, rj[j], 0.0)
            wk = wk + jnp.where(hit, wn[j], 0.0)
        rk = jnp.sum(rk, axis=0, keepdims=True).astype(jnp.int32)
        wk = jnp.sum(wk, axis=0, keepdims=True)
        for i, sub in enumerate(subs):
            e_ref[i, k:k + 1, :] = e_sel[k][:, sub]
            rank_ref[i, k:k + 1, :] = rk[:, sub]
            w_ref[i, k:k + 1, :] = wk[:, sub]


def _post_call(attn, gm, x, mod, wts, ts, tm):
    b, s, d = x.shape
    nt = s // ts
    nsub = ts // tm
    ntm = b * nt * nsub
    (goa, woa, wog, g2, wr, br, wgus, wds) = wts
    full = lambda a: pl.BlockSpec(a.shape, lambda bi, i: (0,) * a.ndim)
    tok = lambda w: pl.BlockSpec((1, ts, w), lambda bi, i: (bi, i, 0))
    rout = pl.BlockSpec((nsub, TOP_K, tm), lambda bi, i: (bi * nt + i, 0, 0))
    out_shapes = (
        jax.ShapeDtypeStruct((b, s, d), F32),
        jax.ShapeDtypeStruct((b * s, d // 2), jnp.uint32),
        jax.ShapeDtypeStruct((ntm, TOP_K, tm), jnp.int32),
        jax.ShapeDtypeStruct((ntm, TOP_K, tm), jnp.int32),
        jax.ShapeDtypeStruct((ntm, TOP_K, tm), F32),
        jax.ShapeDtypeStruct((ntm, N_EXPERTS, LANES), F32),
    )
    return pl.pallas_call(
        _post_kernel,
        name="post",
        grid=(b, nt),
        in_specs=[tok(N_HEADS * V_DIM), tok(GMLP_W), tok(d),
                  pl.BlockSpec((1, N_MOD, d), lambda bi, i: (bi, 0, 0)),
                  full(goa), full(woa), full(wog), full(g2), full(wr), full(br), full(wgus), full(wds)],
        out_specs=(tok(d), pl.BlockSpec((ts, d // 2), lambda bi, i: (bi * nt + i, 0)),
                   rout, rout, rout,
                   pl.BlockSpec((nsub, N_EXPERTS, LANES), lambda bi, i: (bi * nt + i, 0, 0))),
        out_shape=out_shapes,
        compiler_params=pltpu.CompilerParams(dimension_semantics=("arbitrary", "arbitrary"),
                                             vmem_limit_bytes=VMEM_LIMIT),
    )(attn, gm, x, mod, goa, woa, wog, g2, wr, br, wgus, wds)


def _disp_kernel(pends_ref, zfrom_ref, info_ref, lpos_ref, h_ref, xs_ref, sbuf, zbuf, nprev, sems, zsem):
    td = h_ref.shape[0]
    blk = zbuf.shape[0]
    step = pl.program_id(0)
    slot = lax.rem(step, 2)

    @pl.when(step == 0)
    def _():
        nprev[0] = 0
        zbuf[...] = jnp.zeros_like(zbuf)

        def tail_copy(e, j):
            start = pl.multiple_of(zfrom_ref[e] + j * blk, blk)
            return pltpu.make_async_copy(zbuf, xs_ref.at[pl.ds(start, blk), :], zsem)

        def n_tail(e):
            return (pends_ref[e] - zfrom_ref[e]) // blk

        def zstart(e, c):
            def one(j, c2):
                tail_copy(e, j).start()
                return c2
            return lax.fori_loop(0, n_tail(e), one, c)

        def zwait(e, c):
            def one(j, c2):
                tail_copy(e, j).wait()
                return c2
            return lax.fori_loop(0, n_tail(e), one, c)

        lax.fori_loop(0, N_EXPERTS, zstart, 0)
        lax.fori_loop(0, N_EXPERTS, zwait, 0)

    lo, hi = _unpack_bf16_pairs(h_ref[...])
    lo = lo.astype(BF16)
    hi = hi.astype(BF16)
    lpos = lpos_ref[0]
    lpos_b = [jnp.broadcast_to(lpos[k:k + 1, :], (PERM_ROWS, td)).astype(jnp.int16) for k in range(TOP_K)]
    row0 = lax.broadcasted_iota(jnp.int32, (PERM_ROWS, td), 0)
    one = jnp.ones((PERM_ROWS, td), BF16)

    def build(rb, c):
        r0 = pl.multiple_of(rb * PERM_ROWS, PERM_ROWS)
        riota = (row0 + r0).astype(jnp.int16)
        pb = jnp.zeros((PERM_ROWS, td), BF16)
        for k in range(TOP_K):
            pb = jnp.where(lpos_b[k] == riota, one, pb)
        xlo = pltpu.bitcast(_dot(pb, lo), jnp.uint32)
        xhi = pltpu.bitcast(_dot(pb, hi), jnp.uint32)
        sbuf[slot, pl.ds(r0, PERM_ROWS), :] = (xhi & jnp.uint32(0xFFFF0000)) | (xlo >> 16)
        return c

    lax.fori_loop(0, info_ref[0, 2, 1], build, 0)

    def chunk_copy(src, dst, sl):
        return pltpu.make_async_copy(
            sbuf.at[sl, pl.ds(pl.multiple_of(src, ROW_ALIGN), DISP_CHUNK), :],
            xs_ref.at[pl.ds(pl.multiple_of(dst, ROW_ALIGN), DISP_CHUNK), :], sems.at[sl])

    def drain(n, sl):
        def one(c, carry):
            chunk_copy(0, 0, sl).wait()
            return carry
        lax.fori_loop(0, n, one, 0)

    def issue(c, carry):
        chunk_copy(info_ref[0, 0, c], info_ref[0, 1, c], slot).start()
        return carry

    drain(nprev[0], 1 - slot)
    n_chunks = info_ref[0, 2, 0]
    lax.fori_loop(0, n_chunks, issue, 0)
    nprev[0] = n_chunks

    @pl.when(step == pl.num_programs(0) - 1)
    def _():
        drain(n_chunks, slot)


def _sorted_rows(td):
    need = TOP_K * td + N_EXPERTS * (ROW_ALIGN - 1) + DISP_CHUNK
    return -(-need // PERM_ROWS) * PERM_ROWS


def _disp_call(pends, zfrom, info, lpos, h2p, n_rows, td, blk):
    t, w = h2p.shape
    nt = t // td
    return pl.pallas_call(
        _disp_kernel,
        name="disp",
        grid_spec=pltpu.PrefetchScalarGridSpec(
            num_scalar_prefetch=2,
            grid=(nt,),
            in_specs=[pl.BlockSpec((1,) + info.shape[1:], lambda i, pe, pa: (i, 0, 0), memory_space=pltpu.SMEM),
                      pl.BlockSpec((1, TOP_K, td), lambda i, pe, pa: (i, 0, 0)),
                      pl.BlockSpec((td, w), lambda i, pe, pa: (i, 0))],
            out_specs=pl.BlockSpec(memory_space=pl.ANY),
            scratch_shapes=[pltpu.VMEM((2, _sorted_rows(td), w), jnp.uint32),
                            pltpu.VMEM((blk, w), jnp.uint32),
                            pltpu.SMEM((1,), jnp.int32),
                            pltpu.SemaphoreType.DMA((2,)), pltpu.SemaphoreType.DMA(())],
        ),
        out_shape=jax.ShapeDtypeStruct((n_rows, w), jnp.uint32),
        compiler_params=pltpu.CompilerParams(dimension_semantics=("arbitrary",),
                                             vmem_limit_bytes=VMEM_LIMIT),
    )(pends, zfrom, info, lpos, h2p)


def _exp_kernel(bexp_ref, nused_ref, xs_ref, wgu_ref, wd_ref, ys_ref):
    i = pl.program_id(0)

    @pl.when(i < nused_ref[0])
    def _():
        half = D_MODEL // 2
        lo, hi = _unpack_bf16_pairs(xs_ref[...])
        gu = (_dot(lo.astype(BF16), wgu_ref[0, :half, :])
              + _dot(hi.astype(BF16), wgu_ref[0, half:, :]))
        g, u = gu[:, :EXPERT_DIM], gu[:, EXPERT_DIM:]
        a = (g * _sigmoid(g) * u).astype(BF16)
        ys_ref[...] = _dot(a, wd_ref[0]).astype(BF16)


def _exp_call(block_exp, n_used, xs, wgu, wd, blk):
    n_rows, w = xs.shape
    n_blocks = n_rows // blk

    def row_map(i, bexp, nused):
        return (jnp.minimum(i, nused[0] - 1), 0)

    def w_map(i, bexp, nused):
        return (bexp[jnp.minimum(i, nused[0] - 1)], 0, 0)

    return pl.pallas_call(
        _exp_kernel,
        name="exp",
        grid_spec=pltpu.PrefetchScalarGridSpec(
            num_scalar_prefetch=2,
            grid=(n_blocks,),
            in_specs=[pl.BlockSpec((blk, w), row_map),
                      pl.BlockSpec((1, D_MODEL, 2 * EXPERT_DIM), w_map),
                      pl.BlockSpec((1, EXPERT_DIM, D_MODEL), w_map)],
            out_specs=pl.BlockSpec((blk, D_MODEL), row_map),
        ),
        out_shape=jax.ShapeDtypeStruct((n_rows, D_MODEL), BF16),
        compiler_params=pltpu.CompilerParams(dimension_semantics=("arbitrary",),
                                             vmem_limit_bytes=VMEM_LIMIT),
    )(block_exp, n_used, xs, wgu, wd)


def _comb_kernel(info_ref, next_info_ref, ybase_ref, mod_ref, cpos_ref, w_ref, gf_ref, ys_ref, o_ref,
                 ybuf, acc, sems):
    tc = ybase_ref.shape[1]
    step = pl.program_id(0) * pl.num_programs(1) + pl.program_id(1)
    n_steps = pl.num_programs(0) * pl.num_programs(1)
    slot = lax.rem(step, 2)

    def chunk_copy(src, dst, sl, n_tiles):
        return pltpu.make_async_copy(ys_ref.at[pl.ds(src, n_tiles)], ybuf.at[sl, pl.ds(dst, n_tiles)],
                                     sems.at[sl])

    def fetch(iref, sl):
        for i, n_tiles in enumerate(COMB_CHUNK_TILES):
            def issue(c, carry):
                chunk_copy(iref[0, 2 * i, c], iref[0, 2 * i + 1, c], sl, n_tiles).start()
                return carry
            lax.fori_loop(0, iref[0, 2 * len(COMB_CHUNK_TILES), i], issue, 0)

    @pl.when(step == 0)
    def _():
        ybuf[...] = jnp.zeros_like(ybuf)
        fetch(info_ref, 0)

    @pl.when(step + 1 < n_steps)
    def _():
        fetch(next_info_ref, 1 - slot)

    for i, n_tiles in enumerate(COMB_CHUNK_TILES):
        def drain(c, carry):
            chunk_copy(0, 0, slot, n_tiles).wait()
            return carry
        lax.fori_loop(0, info_ref[0, 2 * len(COMB_CHUNK_TILES), i], drain, 0)

    cpos = cpos_ref[...]
    w = w_ref[...]
    cpos_b = [jnp.broadcast_to(cpos[:, k:k + 1], (tc, LANES)).astype(jnp.int16) for k in range(TOP_K)]
    w_b = [jnp.broadcast_to(w[:, k:k + 1], (tc, LANES)).astype(BF16) for k in range(TOP_K)]
    lane = lax.broadcasted_iota(jnp.int32, (tc, LANES), 1)
    acc[...] = jnp.zeros_like(acc)

    def slab(kt, c):
        k0 = pl.multiple_of(kt * COMB_KTILE, COMB_KTILE)
        cols = []
        for j in range(COMB_KTILE // LANES):
            col = (lane + (k0 + j * LANES)).astype(jnp.int16)
            wm = jnp.zeros((tc, LANES), BF16)
            for k in range(TOP_K):
                wm = jnp.where(cpos_b[k] == col, w_b[k], wm)
            cols.append(wm)
        rows = ybuf[slot, pl.ds(kt * (COMB_KTILE // COMB_CHUNK), COMB_KTILE // COMB_CHUNK)]
        acc[...] += _dot(jnp.concatenate(cols, axis=1), rows.reshape(COMB_KTILE, rows.shape[-1]))
        return c

    lax.fori_loop(0, info_ref[0, 2 * len(COMB_CHUNK_TILES), len(COMB_CHUNK_TILES)], slab, 0)
    ga2 = mod_ref[0][5:6]
    o_ref[0] = _rms(ybase_ref[0] + ga2 * acc[...], gf_ref[...])


def _gather_rows(tc):
    need = TOP_K * tc + N_EXPERTS * 2 * (COMB_CHUNK - 1)
    return -(-need // COMB_KTILE) * COMB_KTILE


def _comb_call(info, ybase, mod, cpos_tok, w_tok, g_final, ys, tc):
    b, s, d = ybase.shape
    nt = s // tc
    return pl.pallas_call(
        _comb_kernel,
        name="comb",
        grid=(b, nt),
        in_specs=[pl.BlockSpec((1,) + info.shape[1:], lambda bi, i: (bi * nt + i, 0, 0), memory_space=pltpu.SMEM),
                  pl.BlockSpec((1,) + info.shape[1:], lambda bi, i: (jnp.minimum(bi * nt + i + 1, b * nt - 1), 0, 0),
                               memory_space=pltpu.SMEM),
                  pl.BlockSpec((1, tc, d), lambda bi, i: (bi, i, 0)),
                  pl.BlockSpec((1, N_MOD, d), lambda bi, i: (bi, 0, 0)),
                  pl.BlockSpec((tc, TOP_K), lambda bi, i: (bi * nt + i, 0)),
                  pl.BlockSpec((tc, TOP_K), lambda bi, i: (bi * nt + i, 0)),
                  pl.BlockSpec((1, d), lambda bi, i: (0, 0)),
                  pl.BlockSpec(memory_space=pl.ANY)],
        out_specs=pl.BlockSpec((1, tc, d), lambda bi, i: (bi, i, 0)),
        out_shape=jax.ShapeDtypeStruct((b, s, d), F32),
        scratch_shapes=[pltpu.VMEM((2, _gather_rows(tc) // COMB_CHUNK, COMB_CHUNK, d), BF16),
                        pltpu.VMEM((tc, d), F32), pltpu.SemaphoreType.DMA((2,))],
        compiler_params=pltpu.CompilerParams(dimension_semantics=("arbitrary", "arbitrary"),
                                             vmem_limit_bytes=VMEM_LIMIT),
    )(info, info, ybase, mod, cpos_tok, w_tok, g_final, ys)


def _prep_weights(w_ada, b_ada, g_norm1, w_in, g_q_lat, w_uq, g_kv_lat, w_ukv, g_gmlp_v, w_spatial,
                  b_spatial, g_out_attn, g_out_gmlp, w_out, g_norm2, w_router, b_router, w_gate_e,
                  w_up_e, w_down_e, w_gate_s, w_up_s, w_down_s, g_final):
    row = lambda g: g.reshape(1, -1).astype(F32)
    o1 = Q_LORA
    o2 = o1 + KV_LORA
    o3 = o2 + QK_ROPE
    o4 = o3 + GMLP_W
    kr_cols = jnp.pad(w_in[:, o2:o3], ((0, 0), (QK_NOPE, LANES - QK_NOPE - QK_ROPE)))
    win = jnp.concatenate([w_in[:, :o2], w_in[:, o3:o4], w_in[:, o4:], kr_cols], axis=1).astype(BF16)
    qd = QK_NOPE + QK_ROPE
    wuq = jnp.pad(w_uq.reshape(Q_LORA, N_HEADS, qd), ((0, 0), (0, 0), (0, HEAD_PAD - qd)))
    wuq = wuq.reshape(Q_LORA, N_HEADS * HEAD_PAD).astype(BF16)
    wkv = w_ukv.reshape(KV_LORA, N_HEADS, QK_NOPE + V_DIM)
    wuk = jnp.pad(wkv[:, :, :QK_NOPE], ((0, 0), (0, 0), (0, HEAD_PAD - QK_NOPE)))
    wuk = wuk.reshape(KV_LORA, N_HEADS * HEAD_PAD).astype(BF16)
    wuv = jnp.pad(wkv[:, :, QK_NOPE:], ((0, 0), (0, 0), (0, HEAD_PAD - V_DIM)))
    wuv = wuv.reshape(KV_LORA, N_HEADS * HEAD_PAD).astype(BF16)
    vone = jnp.tile((jnp.arange(HEAD_PAD) == V_DIM).astype(F32), N_HEADS).reshape(1, -1)
    wsp = w_spatial.reshape(N_HEADS // 2, 2, CHUNK, CHUNK).transpose(0, 2, 1, 3)
    wsp = wsp.reshape(N_HEADS // 2, CHUNK, 2 * CHUNK).astype(BF16)
    bsp = jnp.repeat(jnp.transpose(b_spatial), GMLP_W // N_HEADS, axis=1).astype(F32)
    pre = (row(g_norm1), win, row(g_q_lat), wuq, row(g_kv_lat), wuk, wuv, vone, row(g_gmlp_v), wsp,
           bsp, row(g_out_gmlp))
    perm = (jnp.arange(N_GROUPS)[None, :] * GROUP_SIZE + jnp.arange(GROUP_SIZE)[:, None]).reshape(-1)
    wr = jnp.transpose(w_router)[perm].astype(BF16)
    br = b_router.astype(F32)[perm].reshape(N_EXPERTS, 1)
    mla_w = N_HEADS * V_DIM
    wgus = jnp.concatenate([w_gate_s, w_up_s], axis=1).astype(BF16)
    post = (row(g_out_attn), w_out[:mla_w].astype(BF16), w_out[mla_w:].astype(BF16), row(g_norm2),
            wr, br, wgus, w_down_s.astype(BF16))
    wgu_e = jnp.concatenate([w_gate_e, w_up_e], axis=2).astype(BF16)
    wd_e = w_down_e.astype(BF16)
    return w_ada.astype(BF16), b_ada, pre, post, (wgu_e, wd_e), row(g_final)


def _rope_tables(s):
    inv = 1.0 / (ROPE_THETA ** (jnp.arange(0, QK_ROPE, 2, dtype=F32) / QK_ROPE))
    ang = jnp.arange(s, dtype=F32)[:, None] * inv[None, :]
    cos, sin = jnp.cos(ang), jnp.sin(ang)
    z = lambda n: jnp.zeros((s, n), F32)
    tail = LANES - QK_NOPE - QK_ROPE
    c = jnp.concatenate([jnp.ones((s, QK_NOPE), F32), cos, cos, z(tail)], axis=1)
    s1 = jnp.concatenate([z(QK_NOPE), -sin, z(HALF_ROPE), z(tail)], axis=1)
    s2 = jnp.concatenate([z(QK_NOPE), z(HALF_ROPE), sin, z(tail)], axis=1)
    return c, s1, s2


def _tiles(s):
    ts = min(1024, s)
    tq = min(512, s)
    tkc = min(1024, s)
    blk = 1024
    tm = min(256, s)
    hps = 8 if s * 8 * HEAD_PAD * 2 * 2 <= VMEM_LIMIT // 4 else 2
    return ts, tq, tkc, blk, tm, hps


def _ceil_to(x, m):
    return (x + m - 1) // m * m


def _trunk(x, c, prep, tiles=None):
    w_ada, b_ada, pre_w, post_w, exp_w, g_final = prep
    b, s, d = x.shape
    ts, tq, tkc, blk, tm, hps = tiles or _tiles(s)
    t = b * s
    nt = t // tm
    mod = _modulation(c, w_ada, b_ada)
    q, k, v, gm = _pre_call(x, mod, _rope_tables(s), pre_w, ts)
    attn = _attn_call(q, k, v, tq, tkc, hps)
    ybase, h2p, e_arr, lrank, w_arr, cnt = _post_call(attn, gm, x, mod, post_w, min(ts, 512), tm)

    i32 = jnp.int32
    cnt = cnt[:, :, 0].astype(i32).reshape(nt, GROUP_SIZE, N_GROUPS).transpose(0, 2, 1).reshape(nt, N_EXPERTS)
    cnt8 = _ceil_to(cnt, ROW_ALIGN)
    base8 = jnp.cumsum(cnt8, axis=0) - cnt8
    total8 = jnp.sum(cnt8, axis=0)
    padded = _ceil_to(total8 + EXPERT_SLACK, blk)
    pends = jnp.cumsum(padded).astype(i32)
    zfrom = ((pends - padded + total8) // blk * blk).astype(i32)
    dstbase = (pends - padded)[None, :] + base8
    toff8 = jnp.cumsum(cnt8, axis=1) - cnt8
    eids = jnp.arange(N_EXPERTS, dtype=i32)

    def chunk_table(nch, chunk, n_max):
        cend = jnp.cumsum(nch, axis=1)
        cidx = jnp.arange(n_max, dtype=i32)
        e_of_c = jnp.minimum(jnp.sum((cend[:, None, :] <= cidx[None, :, None]).astype(i32), axis=-1),
                             N_EXPERTS - 1)
        pick = lambda tbl: jnp.sum(jnp.where(e_of_c[..., None] == eids, tbl[:, None, :], 0), axis=-1)
        rel = lambda first_row: pick(first_row - (cend - nch) * chunk) + cidx[None, :] * chunk
        return rel, cend[:, -1]

    n_dmax = N_EXPERTS + TOP_K * tm // DISP_CHUNK
    nch_d = (cnt + DISP_CHUNK - 1) // DISP_CHUNK
    rel_d, n_dch = chunk_table(nch_d, DISP_CHUNK, n_dmax)
    n_rb = (jnp.sum(cnt8, axis=1) + DISP_CHUNK + PERM_ROWS - 1) // PERM_ROWS
    tail = lambda a, b2, n: jnp.concatenate([a[:, None], b2[:, None], jnp.zeros((nt, n - 2), i32)], axis=1)
    dinfo = jnp.stack([rel_d(toff8), rel_d(dstbase), tail(n_dch, n_rb, n_dmax)], axis=1).astype(i32)

    shift = dstbase % COMB_CHUNK
    nch_c = jnp.where(cnt > 0, (cnt + shift + COMB_CHUNK - 1) // COMB_CHUNK, 0)
    boff = (jnp.cumsum(nch_c, axis=1) - nch_c) * COMB_CHUNK
    big = COMB_CHUNK_TILES[0]
    n_cmax = max(_gather_rows(tm) // COMB_CHUNK // big, N_EXPERTS)
    src_t = (dstbase - shift) // COMB_CHUNK
    dst_t = boff // COMB_CHUNK
    done = nch_c // big * big
    rows, totals = [], []
    for size in COMB_CHUNK_TILES:
        n_size = nch_c // big if size == big else (nch_c - done == size).astype(i32)
        first = 0 if size == big else done
        rel, total = chunk_table(n_size, size, n_cmax)
        rows += [rel(src_t + first), rel(dst_t + first)]
        totals.append(total[:, None])
    n_kt = (jnp.sum(nch_c, axis=1) * COMB_CHUNK + COMB_KTILE - 1) // COMB_KTILE
    counts = jnp.concatenate(totals + [n_kt[:, None], jnp.zeros((nt, n_cmax - len(totals) - 1), i32)], axis=1)
    cinfo = jnp.stack(rows + [counts], axis=1).astype(i32)

    onehot = e_arr[..., None] == eids
    lookup = lambda tbl: jnp.sum(jnp.where(onehot, tbl[:, None, None, :], 0), axis=-1)
    lpos = lookup(toff8) + lrank
    cpos = lookup(boff + shift) + lrank
    tok_major = lambda a: jnp.transpose(a, (0, 2, 1)).reshape(t, TOP_K)

    n_rows = _ceil_to(t * TOP_K + nt * N_EXPERTS * (ROW_ALIGN - 1) + N_EXPERTS * (EXPERT_SLACK + blk - 1), blk)
    n_blocks = n_rows // blk
    n_used = (pends[-1] // blk).astype(i32).reshape(1)
    block_start = jnp.arange(n_blocks, dtype=i32) * blk
    block_exp = jnp.minimum(jnp.sum((pends[None, :] <= block_start[:, None]).astype(i32), axis=1),
                            N_EXPERTS - 1)

    xs = _disp_call(pends, zfrom, dinfo, lpos.astype(i32), h2p, n_rows, tm, blk)
    ys = _exp_call(block_exp, n_used, xs, exp_w[0], exp_w[1], blk)
    ys3 = ys.reshape(n_rows // COMB_CHUNK, COMB_CHUNK, d)
    return _comb_call(cinfo, ybase, mod, tok_major(cpos).astype(i32), tok_major(w_arr), g_final, ys3, tm)


def kernel(x_prompt, x_sample, c_prompt, c_sample, w_ada, b_ada, g_norm1, w_in, g_q_lat, w_uq, g_kv_lat, w_ukv, g_gmlp_v, w_spatial, b_spatial, g_out_attn, g_out_gmlp, w_out, g_norm2, w_router, b_router, w_gate_e, w_up_e, w_down_e, w_gate_s, w_up_s, w_down_s, g_final):
    prep = _prep_weights(w_ada[0], b_ada[0], g_norm1[0], w_in[0], g_q_lat[0], w_uq[0], g_kv_lat[0],
                         w_ukv[0], g_gmlp_v[0], w_spatial[0], b_spatial[0], g_out_attn[0],
                         g_out_gmlp[0], w_out[0], g_norm2[0], w_router[0], b_router[0], w_gate_e[0],
                         w_up_e[0], w_down_e[0], w_gate_s[0], w_up_s[0], w_down_s[0], g_final)
    return (_trunk(x_prompt, c_prompt, prep), _trunk(x_sample, c_sample, prep))
```

```python
import functools
import math

import jax
import jax.numpy as jnp
from jax import lax
from jax.experimental import pallas as pl
from jax.experimental.pallas import tpu as pltpu

F32 = jnp.float32
BF16 = jnp.bfloat16

D_MODEL = 1024
N_HEADS = 8
QK_NOPE = 64
QK_ROPE = 32
V_DIM = 64
Q_LORA = 256
KV_LORA = 128
GMLP_W = 512
CHUNK = 128
N_EXPERTS = 64
TOP_K = 8
N_GROUPS = 8
TOPK_GROUPS = 4
GROUP_SIZE = N_EXPERTS // N_GROUPS
EXPERT_DIM = 256
SHARED_DIM = 256
ROUTED_SCALE = 2.5
ROPE_THETA = 10000.0
N_MOD = 6
EPS = 1e-6

LANES = 128
HEAD_PAD = 128
HALF_ROPE = QK_ROPE // 2
VMEM_LIMIT = 52 * 1024 * 1024
ROW_ALIGN = 8
DISP_CHUNK = 48
COMB_CHUNK = 16
COMB_CHUNK_TILES = (3, 2, 1)
ATTN_CHUNKS_PER_STEP = 8
PERM_ROWS = 512
COMB_KTILE = 1024
EXPERT_SLACK = max(DISP_CHUNK, 2 * (COMB_CHUNK - 1))

SOFTMAX_SCALE = (QK_NOPE + QK_ROPE) ** -0.5
EXP2_SCALE = SOFTMAX_SCALE * math.log2(math.e)


def _rms(x, g):
    return x * lax.rsqrt(jnp.mean(x * x, axis=-1, keepdims=True) + EPS) * g


def _sigmoid(x):
    return 1.0 / (1.0 + jnp.exp(-x))


def _gelu_tanh(x):
    c = math.sqrt(2.0 / math.pi)
    return 0.5 * x * (1.0 + jnp.tanh(c * (x + 0.044715 * (x * x * x))))


def _dot(a, b):
    return jnp.dot(a, b, preferred_element_type=F32)


def _dot_nt(a, b):
    return lax.dot_general(a, b, (((1,), (1,)), ((), ())), preferred_element_type=F32)


def _pack_bf16_pairs(x):
    c = x.shape[1] // 2
    lo = pltpu.bitcast(x[:, :c].astype(BF16).astype(F32), jnp.uint32)
    hi = pltpu.bitcast(x[:, c:].astype(BF16).astype(F32), jnp.uint32)
    return (hi & jnp.uint32(0xFFFF0000)) | (lo >> 16)


def _unpack_bf16_pairs(w):
    lo = pltpu.bitcast(w << 16, F32)
    hi = pltpu.bitcast(w & jnp.uint32(0xFFFF0000), F32)
    return lo, hi


def _mod_kernel(c_ref, w_ref, b_ref, o_ref):
    c = c_ref[...]
    a = (c * _sigmoid(c)).astype(BF16)
    o_ref[...] = _dot(a, w_ref[...]) + b_ref[...]


def _modulation(c, w_ada_bf, b_ada):
    b = c.shape[0]
    bp = max(16, -(-b // 16) * 16)
    cp = jnp.pad(c, ((0, bp - b), (0, 0)))
    n = w_ada_bf.shape[1]
    tn = D_MODEL
    out = pl.pallas_call(
        _mod_kernel,
        name="mod",
        grid=(n // tn,),
        in_specs=[
            pl.BlockSpec((bp, D_MODEL), lambda j: (0, 0)),
            pl.BlockSpec((D_MODEL, tn), lambda j: (0, j)),
            pl.BlockSpec((1, tn), lambda j: (0, j)),
        ],
        out_specs=pl.BlockSpec((bp, tn), lambda j: (0, j)),
        out_shape=jax.ShapeDtypeStruct((bp, n), F32),
        compiler_params=pltpu.CompilerParams(dimension_semantics=("arbitrary",)),
    )(cp, w_ada_bf, b_ada.reshape(1, n))
    return out[:b].reshape(b, N_MOD, D_MODEL)


def _rope(xh, c, s1, s2):
    return (xh * c + pltpu.roll(xh, LANES - HALF_ROPE, axis=1) * s1
            + pltpu.roll(xh, HALF_ROPE, axis=1) * s2)


def _pre_kernel(x_ref, mod_ref, cos_ref, s1_ref, s2_ref, g1_ref, win_ref, gq_ref, wuq_ref,
                gkv_ref, wuk_ref, wuv_ref, vone_ref, ggv_ref, ws_ref, bs_ref, ggo_ref,
                q_ref, k_ref, v_ref, gm_ref, mix_ref):
    ts = x_ref.shape[1]
    x = x_ref[0]
    mod = mod_ref[0]
    h = _rms(x, g1_ref[...]) * (1.0 + mod[1:2]) + mod[0:1]
    z = _dot(h.astype(BF16), win_ref[...])
    o_kv = Q_LORA
    o_gu = o_kv + KV_LORA
    o_gv = o_gu + GMLP_W
    o_kr = o_gv + GMLP_W
    q_lat = z[:, :o_kv]
    kv_lat = z[:, o_kv:o_gu]
    g_u = z[:, o_gu:o_gv]
    g_v = z[:, o_gv:o_kr]
    kr = z[:, o_kr:o_kr + LANES]

    cos = cos_ref[...]
    s1 = s1_ref[...]
    s2 = s2_ref[...]

    qn = _rms(q_lat, gq_ref[...]).astype(BF16)
    q = _dot(qn, wuq_ref[...])
    kn = _rms(kv_lat, gkv_ref[...]).astype(BF16)
    kf = _dot(kn, wuk_ref[...])
    v_ref[0] = (_dot(kn, wuv_ref[...]) + vone_ref[...]).astype(BF16)
    krr = _rope(kr, cos, s1, s2)
    for hd in range(N_HEADS):
        sl = slice(hd * HEAD_PAD, (hd + 1) * HEAD_PAD)
        q_ref[0, :, sl] = (_rope(q[:, sl], cos, s1, s2) * EXP2_SCALE).astype(BF16)
        k_ref[0, sl, :] = jnp.transpose(kf[:, sl] + krr).astype(BF16)

    u = _gelu_tanh(g_u)
    vn = _rms(_gelu_tanh(g_v), ggv_ref[...]).astype(BF16)
    lane = lax.broadcasted_iota(jnp.int32, (CHUNK, LANES), 1)
    left = lane < (LANES // 2)
    zero = jnp.zeros((CHUNK, LANES), BF16)
    for n in range(ts // CHUNK):
        rs = slice(n * CHUNK, (n + 1) * CHUNK)
        for p in range(GMLP_W // LANES):
            cs = slice(p * LANES, (p + 1) * LANES)
            vp = vn[rs, cs]
            rhs = jnp.concatenate([jnp.where(left, vp, zero), jnp.where(left, zero, vp)], axis=0)
            mix_ref[rs, cs] = _dot(ws_ref[p], rhs) + bs_ref[:, cs]
    gm = u * mix_ref[...]
    gm_ref[0] = _rms(gm, ggo_ref[...]).astype(BF16)


def _pre_call(x, mod, tabs, wts, ts):
    b, s, d = x.shape
    cos, s1, s2 = tabs
    (g1, win, gq, wuq, gkv, wuk, wuv, vone, ggv, wsp, bsp, ggo) = wts
    full = lambda a: pl.BlockSpec(a.shape, lambda bi, i: (0,) * a.ndim)
    tab = pl.BlockSpec((ts, LANES), lambda bi, i: (i, 0))
    out_shapes = (
        jax.ShapeDtypeStruct((b, s, N_HEADS * HEAD_PAD), BF16),
        jax.ShapeDtypeStruct((b, N_HEADS * HEAD_PAD, s), BF16),
        jax.ShapeDtypeStruct((b, s, N_HEADS * HEAD_PAD), BF16),
        jax.ShapeDtypeStruct((b, s, GMLP_W), BF16),
    )
    tok = lambda w: pl.BlockSpec((1, ts, w), lambda bi, i: (bi, i, 0))
    tok_t = pl.BlockSpec((1, N_HEADS * HEAD_PAD, ts), lambda bi, i: (bi, 0, i))
    return pl.pallas_call(
        _pre_kernel,
        name="pre",
        grid=(b, s // ts),
        in_specs=[tok(d), pl.BlockSpec((1, N_MOD, d), lambda bi, i: (bi, 0, 0)), tab, tab, tab,
                  full(g1), full(win), full(gq), full(wuq), full(gkv), full(wuk), full(wuv), full(vone),
                  full(ggv), full(wsp), full(bsp), full(ggo)],
        out_specs=(tok(N_HEADS * HEAD_PAD), tok_t, tok(N_HEADS * HEAD_PAD), tok(GMLP_W)),
        out_shape=out_shapes,
        scratch_shapes=[pltpu.VMEM((ts, GMLP_W), F32)],
        compiler_params=pltpu.CompilerParams(dimension_semantics=("arbitrary", "arbitrary"),
                                             vmem_limit_bytes=VMEM_LIMIT),
    )(x, mod, cos, s1, s2, g1, win, gq, wuq, gkv, wuk, wuv, vone, ggv, wsp, bsp, ggo)


def _attn_kernel(q_ref, k_ref, v_ref, o_ref, *, tkc, cpb):
    tq = q_ref.shape[1]
    n_chunks = v_ref.shape[1] // tkc
    hps = q_ref.shape[2] // HEAD_PAD
    heads = [slice(hh * HEAD_PAD, (hh + 1) * HEAD_PAD) for hh in range(hps)]
    qs = [q_ref[0, :, hs] for hs in heads]

    def step(i, carry):
        carry = list(carry)
        for cc in range(cpb):
            off = pl.multiple_of((i * cpb + cc) * tkc, tkc)
            for hh, hs in enumerate(heads):
                m, acc = carry[2 * hh], carry[2 * hh + 1]
                kc = k_ref[0, hs, pl.ds(off, tkc)]
                vc = v_ref[0, pl.ds(off, tkc), hs]
                sc = _dot(qs[hh], kc)
                m_new = jnp.maximum(m, jnp.max(sc, axis=-1, keepdims=True))
                p = jnp.exp2(sc - m_new).astype(BF16)
                carry[2 * hh + 1] = jnp.exp2(m - m_new) * acc + _dot(p, vc)
                carry[2 * hh] = m_new
        return tuple(carry)

    n_steps = n_chunks // cpb
    init = (jnp.full((tq, 1), -jnp.inf, F32), jnp.zeros((tq, HEAD_PAD), F32)) * hps
    res = step(0, init) if n_steps == 1 else lax.fori_loop(0, n_steps, step, init)
    lane = lax.broadcasted_iota(jnp.int32, (tq, HEAD_PAD), 1)
    for pair in range(hps // 2):
        a0, a1 = res[4 * pair + 1], res[4 * pair + 3]
        o0 = a0 / a0[:, V_DIM:V_DIM + 1]
        o1 = a1 / a1[:, V_DIM:V_DIM + 1]
        o_ref[0, :, pair * HEAD_PAD:(pair + 1) * HEAD_PAD] = jnp.where(
            lane < V_DIM, o0, pltpu.roll(o1, V_DIM, axis=1)).astype(BF16)


def _attn_call(q, k, v, tq, tkc, hps):
    b, s, _ = q.shape
    kv_bytes = 2 * s * hps * HEAD_PAD * 2
    mode = pl.Buffered(2 if 2 * kv_bytes <= VMEM_LIMIT // 3 else 1)
    k_res = pl.BlockSpec((1, hps * HEAD_PAD, s), lambda bi, h, i: (bi, h, 0), pipeline_mode=mode)
    v_res = pl.BlockSpec((1, s, hps * HEAD_PAD), lambda bi, h, i: (bi, 0, h), pipeline_mode=mode)
    return pl.pallas_call(
        functools.partial(_attn_kernel, tkc=tkc, cpb=math.gcd(s // tkc, ATTN_CHUNKS_PER_STEP)),
        name="attn",
        grid=(b, N_HEADS // hps, s // tq),
        in_specs=[pl.BlockSpec((1, tq, hps * HEAD_PAD), lambda bi, h, i: (bi, i, h)), k_res, v_res],
        out_specs=pl.BlockSpec((1, tq, hps * V_DIM), lambda bi, h, i: (bi, i, h)),
        out_shape=jax.ShapeDtypeStruct((b, s, N_HEADS * V_DIM), BF16),
        compiler_params=pltpu.CompilerParams(
            dimension_semantics=("arbitrary", "arbitrary", "arbitrary"),
            vmem_limit_bytes=VMEM_LIMIT),
    )(q, k, v)


def _post_kernel(attn_ref, gm_ref, x_ref, mod_ref, goa_ref, woa_ref, wog_ref, g2_ref, wr_ref,
                 br_ref, wgus_ref, wds_ref,
                 ybase_ref, h2p_ref, e_ref, rank_ref, w_ref, cnt_ref):
    ts = x_ref.shape[1]
    mod = mod_ref[0]
    ga1, sh2, sc2, ga2 = mod[2:3], mod[3:4], mod[4:5], mod[5:6]
    an = _rms(attn_ref[0].astype(F32), goa_ref[...]).astype(BF16)
    y = _dot(an, woa_ref[...]) + _dot(gm_ref[0], wog_ref[...])
    x1 = x_ref[0] + ga1 * y
    h2 = _rms(x1, g2_ref[...]) * (1.0 + sc2) + sh2
    h2b = h2.astype(BF16)
    h2p_ref[...] = _pack_bf16_pairs(h2)

    gu = _dot(h2b, wgus_ref[...])
    g, u = gu[:, :SHARED_DIM], gu[:, SHARED_DIM:]
    a = (g * _sigmoid(g) * u).astype(BF16)
    ybase_ref[0] = x1 + ga2 * _dot(a, wds_ref[...])

    logits = _dot_nt(wr_ref[...], h2b)
    scores = _sigmoid(logits)
    biased = scores + br_ref[...]
    ninf = jnp.float32(-jnp.inf)
    bj = [biased[j * N_GROUPS:(j + 1) * N_GROUPS] for j in range(GROUP_SIZE)]
    sj = [scores[j * N_GROUPS:(j + 1) * N_GROUPS] for j in range(GROUP_SIZE)]
    m1 = bj[0]
    for j in range(1, GROUP_SIZE):
        m1 = jnp.maximum(m1, bj[j])
    found = jnp.zeros_like(m1)
    m2 = jnp.full_like(m1, ninf)
    for j in range(GROUP_SIZE):
        eq = jnp.where(bj[j] == m1, 1.0, 0.0)
        is_first = eq * (1.0 - found)
        found = jnp.maximum(found, eq)
        m2 = jnp.maximum(m2, jnp.where(is_first > 0.0, ninf, bj[j]))
    gs = m1 + m2
    gidx = lax.broadcasted_iota(jnp.int32, gs.shape, 0)
    grank = jnp.zeros_like(gs)
    for kk in range(1, N_GROUPS):
        r = pltpu.roll(gs, kk, axis=0)
        grank = grank + jnp.where(gidx >= kk, jnp.where(r >= gs, 1.0, 0.0), jnp.where(r > gs, 1.0, 0.0))
    gsel = grank < float(TOPK_GROUPS)
    masked = [jnp.where(gsel, bj[j], ninf) for j in range(GROUP_SIZE)]
    eidx = [gidx * GROUP_SIZE + j for j in range(GROUP_SIZE)]

    selm = [jnp.zeros_like(gs) for _ in range(GROUP_SIZE)]
    e_sel = []
    for _k in range(TOP_K):
        m = masked[0]
        for j in range(1, GROUP_SIZE):
            m = jnp.maximum(m, masked[j])
        m = jnp.max(m, axis=0, keepdims=True)
        cand = jnp.where(masked[0] == m, eidx[0], N_EXPERTS)
        for j in range(1, GROUP_SIZE):
            cand = jnp.minimum(cand, jnp.where(masked[j] == m, eidx[j], N_EXPERTS))
        emin = jnp.min(cand, axis=0, keepdims=True)
        e_sel.append(emin)
        for j in range(GROUP_SIZE):
            hit = eidx[j] == emin
            selm[j] = jnp.where(hit, 1.0, selm[j])
            masked[j] = jnp.where(hit, ninf, masked[j])

    wsel = [selm[j] * sj[j] for j in range(GROUP_SIZE)]
    tot = wsel[0]
    for j in range(1, GROUP_SIZE):
        tot = tot + wsel[j]
    tot = jnp.sum(tot, axis=0, keepdims=True)
    wn = [wsel[j] / tot * ROUTED_SCALE for j in range(GROUP_SIZE)]

    tm = e_ref.shape[2]
    sel = jnp.concatenate(selm, axis=0)
    tr = lax.broadcasted_iota(jnp.int32, (ts, ts), 0)
    tc = lax.broadcasted_iota(jnp.int32, (ts, ts), 1)
    sh = tm.bit_length() - 1
    same_tile = lax.shift_right_logical(tr, sh) == lax.shift_right_logical(tc, sh)
    upper = jnp.where(tr < tc, jnp.where(same_tile, 1.0, 0.0), 0.0).astype(BF16)
    rank_full = _dot(sel.astype(BF16), upper)
    rj = [rank_full[j * N_GROUPS:(j + 1) * N_GROUPS] for j in range(GROUP_SIZE)]
    subs = [slice(i * tm, (i + 1) * tm) for i in range(ts // tm)]
    for i, sub in enumerate(subs):
        cnt_ref[i] = jnp.broadcast_to(jnp.sum(sel[:, sub], axis=1, keepdims=True), (N_EXPERTS, LANES))

    for k in range(TOP_K):
        rk = jnp.zeros_like(gs)
        wk = jnp.zeros_like(gs)
        for j in range(GROUP_SIZE):
            hit = eidx[j] == e_sel[k]
            rk = rk + jnp.where(hit, rj[j], 0.0)
            wk = wk + jnp.where(hit, wn[j], 0.0)
        rk = jnp.sum(rk, axis=0, keepdims=True).astype(jnp.int32)
        wk = jnp.sum(wk, axis=0, keepdims=True)
        for i, sub in enumerate(subs):
            e_ref[i, k:k + 1, :] = e_sel[k][:, sub]
            rank_ref[i, k:k + 1, :] = rk[:, sub]
            w_ref[i, k:k + 1, :] = wk[:, sub]


def _post_call(attn, gm, x, mod, wts, ts, tm):
    b, s, d = x.shape
    nt = s // ts
    nsub = ts // tm
    ntm = b * nt * nsub
    (goa, woa, wog, g2, wr, br, wgus, wds) = wts
    full = lambda a: pl.BlockSpec(a.shape, lambda bi, i: (0,) * a.ndim)
    tok = lambda w: pl.BlockSpec((1, ts, w), lambda bi, i: (bi, i, 0))
    rout = pl.BlockSpec((nsub, TOP_K, tm), lambda bi, i: (bi * nt + i, 0, 0))
    out_shapes = (
        jax.ShapeDtypeStruct((b, s, d), F32),
        jax.ShapeDtypeStruct((b * s, d // 2), jnp.uint32),
        jax.ShapeDtypeStruct((ntm, TOP_K, tm), jnp.int32),
        jax.ShapeDtypeStruct((ntm, TOP_K, tm), jnp.int32),
        jax.ShapeDtypeStruct((ntm, TOP_K, tm), F32),
        jax.ShapeDtypeStruct((ntm, N_EXPERTS, LANES), F32),
    )
    return pl.pallas_call(
        _post_kernel,
        name="post",
        grid=(b, nt),
        in_specs=[tok(N_HEADS * V_DIM), tok(GMLP_W), tok(d),
                  pl.BlockSpec((1, N_MOD, d), lambda bi, i: (bi, 0, 0)),
                  full(goa), full(woa), full(wog), full(g2), full(wr), full(br), full(wgus), full(wds)],
        out_specs=(tok(d), pl.BlockSpec((ts, d // 2), lambda bi, i: (bi * nt + i, 0)),
                   rout, rout, rout,
                   pl.BlockSpec((nsub, N_EXPERTS, LANES), lambda bi, i: (bi * nt + i, 0, 0))),
        out_shape=out_shapes,
        compiler_params=pltpu.CompilerParams(dimension_semantics=("arbitrary", "arbitrary"),
                                             vmem_limit_bytes=VMEM_LIMIT),
    )(attn, gm, x, mod, goa, woa, wog, g2, wr, br, wgus, wds)


def _disp_kernel(pends_ref, zfrom_ref, info_ref, lpos_ref, h_ref, xs_ref, sbuf, zbuf, nprev, sems, zsem):
    td = h_ref.shape[0]
    blk = zbuf.shape[0]
    step = pl.program_id(0)
    slot = lax.rem(step, 2)

    @pl.when(step == 0)
    def _():
        nprev[0] = 0
        zbuf[...] = jnp.zeros_like(zbuf)

        def tail_copy(e, j):
            start = pl.multiple_of(zfrom_ref[e] + j * blk, blk)
            return pltpu.make_async_copy(zbuf, xs_ref.at[pl.ds(start, blk), :], zsem)

        def n_tail(e):
            return (pends_ref[e] - zfrom_ref[e]) // blk

        def zstart(e, c):
            def one(j, c2):
                tail_copy(e, j).start()
                return c2
            return lax.fori_loop(0, n_tail(e), one, c)

        def zwait(e, c):
            def one(j, c2):
                tail_copy(e, j).wait()
                return c2
            return lax.fori_loop(0, n_tail(e), one, c)

        lax.fori_loop(0, N_EXPERTS, zstart, 0)
        lax.fori_loop(0, N_EXPERTS, zwait, 0)

    lo, hi = _unpack_bf16_pairs(h_ref[...])
    lo = lo.astype(BF16)
    hi = hi.astype(BF16)
    lpos = lpos_ref[0]
    lpos_b = [jnp.broadcast_to(lpos[k:k + 1, :], (PERM_ROWS, td)).astype(jnp.int16) for k in range(TOP_K)]
    row0 = lax.broadcasted_iota(jnp.int32, (PERM_ROWS, td), 0)
    one = jnp.ones((PERM_ROWS, td), BF16)

    def build(rb, c):
        r0 = pl.multiple_of(rb * PERM_ROWS, PERM_ROWS)
        riota = (row0 + r0).astype(jnp.int16)
        pb = jnp.zeros((PERM_ROWS, td), BF16)
        for k in range(TOP_K):
            pb = jnp.where(lpos_b[k] == riota, one, pb)
        xlo = pltpu.bitcast(_dot(pb, lo), jnp.uint32)
        xhi = pltpu.bitcast(_dot(pb, hi), jnp.uint32)
        sbuf[slot, pl.ds(r0, PERM_ROWS), :] = (xhi & jnp.uint32(0xFFFF0000)) | (xlo >> 16)
        return c

    lax.fori_loop(0, info_ref[0, 2, 1], build, 0)

    def chunk_copy(src, dst, sl):
        return pltpu.make_async_copy(
            sbuf.at[sl, pl.ds(pl.multiple_of(src, ROW_ALIGN), DISP_CHUNK), :],
            xs_ref.at[pl.ds(pl.multiple_of(dst, ROW_ALIGN), DISP_CHUNK), :], sems.at[sl])

    def drain(n, sl):
        def one(c, carry):
            chunk_copy(0, 0, sl).wait()
            return carry
        lax.fori_loop(0, n, one, 0)

    def issue(c, carry):
        chunk_copy(info_ref[0, 0, c], info_ref[0, 1, c], slot).start()
        return carry

    drain(nprev[0], 1 - slot)
    n_chunks = info_ref[0, 2, 0]
    lax.fori_loop(0, n_chunks, issue, 0)
    nprev[0] = n_chunks

    @pl.when(step == pl.num_programs(0) - 1)
    def _():
        drain(n_chunks, slot)


def _sorted_rows(td):
    need = TOP_K * td + N_EXPERTS * (ROW_ALIGN - 1) + DISP_CHUNK
    return -(-need // PERM_ROWS) * PERM_ROWS


def _disp_call(pends, zfrom, info, lpos, h2p, n_rows, td, blk):
    t, w = h2p.shape
    nt = t // td
    return pl.pallas_call(
        _disp_kernel,
        name="disp",
        grid_spec=pltpu.PrefetchScalarGridSpec(
            num_scalar_prefetch=2,
            grid=(nt,),
            in_specs=[pl.BlockSpec((1,) + info.shape[1:], lambda i, pe, pa: (i, 0, 0), memory_space=pltpu.SMEM),
                      pl.BlockSpec((1, TOP_K, td), lambda i, pe, pa: (i, 0, 0)),
                      pl.BlockSpec((td, w), lambda i, pe, pa: (i, 0))],
            out_specs=pl.BlockSpec(memory_space=pl.ANY),
            scratch_shapes=[pltpu.VMEM((2, _sorted_rows(td), w), jnp.uint32),
                            pltpu.VMEM((blk, w), jnp.uint32),
                            pltpu.SMEM((1,), jnp.int32),
                            pltpu.SemaphoreType.DMA((2,)), pltpu.SemaphoreType.DMA(())],
        ),
        out_shape=jax.ShapeDtypeStruct((n_rows, w), jnp.uint32),
        compiler_params=pltpu.CompilerParams(dimension_semantics=("arbitrary",),
                                             vmem_limit_bytes=VMEM_LIMIT),
    )(pends, zfrom, info, lpos, h2p)


def _exp_kernel(bexp_ref, nused_ref, xs_ref, wgu_ref, wd_ref, ys_ref):
    i = pl.program_id(0)

    @pl.when(i < nused_ref[0])
    def _():
        half = D_MODEL // 2
        lo, hi = _unpack_bf16_pairs(xs_ref[...])
        gu = (_dot(lo.astype(BF16), wgu_ref[0, :half, :])
              + _dot(hi.astype(BF16), wgu_ref[0, half:, :]))
        g, u = gu[:, :EXPERT_DIM], gu[:, EXPERT_DIM:]
        a = (g * _sigmoid(g) * u).astype(BF16)
        ys_ref[...] = _dot(a, wd_ref[0]).astype(BF16)


def _exp_call(block_exp, n_used, xs, wgu, wd, blk):
    n_rows, w = xs.shape
    n_blocks = n_rows // blk

    def row_map(i, bexp, nused):
        return (jnp.minimum(i, nused[0] - 1), 0)

    def w_map(i, bexp, nused):
        return (bexp[jnp.minimum(i, nused[0] - 1)], 0, 0)

    return pl.pallas_call(
        _exp_kernel,
        name="exp",
        grid_spec=pltpu.PrefetchScalarGridSpec(
            num_scalar_prefetch=2,
            grid=(n_blocks,),
            in_specs=[pl.BlockSpec((blk, w), row_map),
                      pl.BlockSpec((1, D_MODEL, 2 * EXPERT_DIM), w_map),
                      pl.BlockSpec((1, EXPERT_DIM, D_MODEL), w_map)],
            out_specs=pl.BlockSpec((blk, D_MODEL), row_map),
        ),
        out_shape=jax.ShapeDtypeStruct((n_rows, D_MODEL), BF16),
        compiler_params=pltpu.CompilerParams(dimension_semantics=("arbitrary",),
                                             vmem_limit_bytes=VMEM_LIMIT),
    )(block_exp, n_used, xs, wgu, wd)


def _comb_kernel(info_ref, next_info_ref, ybase_ref, mod_ref, cpos_ref, w_ref, gf_ref, ys_ref, o_ref,
                 ybuf, acc, sems):
    tc = ybase_ref.shape[1]
    step = pl.program_id(0) * pl.num_programs(1) + pl.program_id(1)
    n_steps = pl.num_programs(0) * pl.num_programs(1)
    slot = lax.rem(step, 2)

    def chunk_copy(src, dst, sl, n_tiles):
        return pltpu.make_async_copy(ys_ref.at[pl.ds(src, n_tiles)], ybuf.at[sl, pl.ds(dst, n_tiles)],
                                     sems.at[sl])

    def fetch(iref, sl):
        for i, n_tiles in enumerate(COMB_CHUNK_TILES):
            def issue(c, carry):
                chunk_copy(iref[0, 2 * i, c], iref[0, 2 * i + 1, c], sl, n_tiles).start()
                return carry
            lax.fori_loop(0, iref[0, 2 * len(COMB_CHUNK_TILES), i], issue, 0)

    @pl.when(step == 0)
    def _():
        ybuf[...] = jnp.zeros_like(ybuf)
        fetch(info_ref, 0)

    @pl.when(step + 1 < n_steps)
    def _():
        fetch(next_info_ref, 1 - slot)

    for i, n_tiles in enumerate(COMB_CHUNK_TILES):
        def drain(c, carry):
            chunk_copy(0, 0, slot, n_tiles).wait()
            return carry
        lax.fori_loop(0, info_ref[0, 2 * len(COMB_CHUNK_TILES), i], drain, 0)

    cpos = cpos_ref[...]
    w = w_ref[...]
    cpos_b = [jnp.broadcast_to(cpos[:, k:k + 1], (tc, LANES)).astype(jnp.int16) for k in range(TOP_K)]
    w_b = [jnp.broadcast_to(w[:, k:k + 1], (tc, LANES)).astype(BF16) for k in range(TOP_K)]
    lane = lax.broadcasted_iota(jnp.int32, (tc, LANES), 1)
    acc[...] = jnp.zeros_like(acc)

    def slab(kt, c):
        k0 = pl.multiple_of(kt * COMB_KTILE, COMB_KTILE)
        cols = []
        for j in range(COMB_KTILE // LANES):
            col = (lane + (k0 + j * LANES)).astype(jnp.int16)
            wm = jnp.zeros((tc, LANES), BF16)
            for k in range(TOP_K):
                wm = jnp.where(cpos_b[k] == col, w_b[k], wm)
            cols.append(wm)
        rows = ybuf[slot, pl.ds(kt * (COMB_KTILE // COMB_CHUNK), COMB_KTILE // COMB_CHUNK)]
        acc[...] += _dot(jnp.concatenate(cols, axis=1), rows.reshape(COMB_KTILE, rows.shape[-1]))
        return c

    lax.fori_loop(0, info_ref[0, 2 * len(COMB_CHUNK_TILES), len(COMB_CHUNK_TILES)], slab, 0)
    ga2 = mod_ref[0][5:6]
    o_ref[0] = _rms(ybase_ref[0] + ga2 * acc[...], gf_ref[...])


def _gather_rows(tc):
    need = TOP_K * tc + N_EXPERTS * 2 * (COMB_CHUNK - 1)
    return -(-need // COMB_KTILE) * COMB_KTILE


def _comb_call(info, ybase, mod, cpos_tok, w_tok, g_final, ys, tc):
    b, s, d = ybase.shape
    nt = s // tc
    return pl.pallas_call(
        _comb_kernel,
        name="comb",
        grid=(b, nt),
        in_specs=[pl.BlockSpec((1,) + info.shape[1:], lambda bi, i: (bi * nt + i, 0, 0), memory_space=pltpu.SMEM),
                  pl.BlockSpec((1,) + info.shape[1:], lambda bi, i: (jnp.minimum(bi * nt + i + 1, b * nt - 1), 0, 0),
                               memory_space=pltpu.SMEM),
                  pl.BlockSpec((1, tc, d), lambda bi, i: (bi, i, 0)),
                  pl.BlockSpec((1, N_MOD, d), lambda bi, i: (bi, 0, 0)),
                  pl.BlockSpec((tc, TOP_K), lambda bi, i: (bi * nt + i, 0)),
                  pl.BlockSpec((tc, TOP_K), lambda bi, i: (bi * nt + i, 0)),
                  pl.BlockSpec((1, d), lambda bi, i: (0, 0)),
                  pl.BlockSpec(memory_space=pl.ANY)],
        out_specs=pl.BlockSpec((1, tc, d), lambda bi, i: (bi, i, 0)),
        out_shape=jax.ShapeDtypeStruct((b, s, d), F32),
        scratch_shapes=[pltpu.VMEM((2, _gather_rows(tc) // COMB_CHUNK, COMB_CHUNK, d), BF16),
                        pltpu.VMEM((tc, d), F32), pltpu.SemaphoreType.DMA((2,))],
        compiler_params=pltpu.CompilerParams(dimension_semantics=("arbitrary", "arbitrary"),
                                             vmem_limit_bytes=VMEM_LIMIT),
    )(info, info, ybase, mod, cpos_tok, w_tok, g_final, ys)


def _prep_weights(w_ada, b_ada, g_norm1, w_in, g_q_lat, w_uq, g_kv_lat, w_ukv, g_gmlp_v, w_spatial,
                  b_spatial, g_out_attn, g_out_gmlp, w_out, g_norm2, w_router, b_router, w_gate_e,
                  w_up_e, w_down_e, w_gate_s, w_up_s, w_down_s, g_final):
    row = lambda g: g.reshape(1, -1).astype(F32)
    o1 = Q_LORA
    o2 = o1 + KV_LORA
    o3 = o2 + QK_ROPE
    o4 = o3 + GMLP_W
    kr_cols = jnp.pad(w_in[:, o2:o3], ((0, 0), (QK_NOPE, LANES - QK_NOPE - QK_ROPE)))
    win = jnp.concatenate([w_in[:, :o2], w_in[:, o3:o4], w_in[:, o4:], kr_cols], axis=1).astype(BF16)
    qd = QK_NOPE + QK_ROPE
    wuq = jnp.pad(w_uq.reshape(Q_LORA, N_HEADS, qd), ((0, 0), (0, 0), (0, HEAD_PAD - qd)))
    wuq = wuq.reshape(Q_LORA, N_HEADS * HEAD_PAD).astype(BF16)
    wkv = w_ukv.reshape(KV_LORA, N_HEADS, QK_NOPE + V_DIM)
    wuk = jnp.pad(wkv[:, :, :QK_NOPE], ((0, 0), (0, 0), (0, HEAD_PAD - QK_NOPE)))
    wuk = wuk.reshape(KV_LORA, N_HEADS * HEAD_PAD).astype(BF16)
    wuv = jnp.pad(wkv[:, :, QK_NOPE:], ((0, 0), (0, 0), (0, HEAD_PAD - V_DIM)))
    wuv = wuv.reshape(KV_LORA, N_HEADS * HEAD_PAD).astype(BF16)
    vone = jnp.tile((jnp.arange(HEAD_PAD) == V_DIM).astype(F32), N_HEADS).reshape(1, -1)
    wsp = w_spatial.reshape(N_HEADS // 2, 2, CHUNK, CHUNK).transpose(0, 2, 1, 3)
    wsp = wsp.reshape(N_HEADS // 2, CHUNK, 2 * CHUNK).astype(BF16)
    bsp = jnp.repeat(jnp.transpose(b_spatial), GMLP_W // N_HEADS, axis=1).astype(F32)
    pre = (row(g_norm1), win, row(g_q_lat), wuq, row(g_kv_lat), wuk, wuv, vone, row(g_gmlp_v), wsp,
           bsp, row(g_out_gmlp))
    perm = (jnp.arange(N_GROUPS)[None, :] * GROUP_SIZE + jnp.arange(GROUP_SIZE)[:, None]).reshape(-1)
    wr = jnp.transpose(w_router)[perm].astype(BF16)
    br = b_router.astype(F32)[perm].reshape(N_EXPERTS, 1)
    mla_w = N_HEADS * V_DIM
    wgus = jnp.concatenate([w_gate_s, w_up_s], axis=1).astype(BF16)
    post = (row(g_out_attn), w_out[:mla_w].astype(BF16), w_out[mla_w:].astype(BF16), row(g_norm2),
            wr, br, wgus, w_down_s.astype(BF16))
    wgu_e = jnp.concatenate([w_gate_e, w_up_e], axis=2).astype(BF16)
    wd_e = w_down_e.astype(BF16)
    return w_ada.astype(BF16), b_ada, pre, post, (wgu_e, wd_e), row(g_final)


def _rope_tables(s):
    inv = 1.0 / (ROPE_THETA ** (jnp.arange(0, QK_ROPE, 2, dtype=F32) / QK_ROPE))
    ang = jnp.arange(s, dtype=F32)[:, None] * inv[None, :]
    cos, sin = jnp.cos(ang), jnp.sin(ang)
    z = lambda n: jnp.zeros((s, n), F32)
    tail = LANES - QK_NOPE - QK_ROPE
    c = jnp.concatenate([jnp.ones((s, QK_NOPE), F32), cos, cos, z(tail)], axis=1)
    s1 = jnp.concatenate([z(QK_NOPE), -sin, z(HALF_ROPE), z(tail)], axis=1)
    s2 = jnp.concatenate([z(QK_NOPE), z(HALF_ROPE), sin, z(tail)], axis=1)
    return c, s1, s2


def _tiles(s):
    ts = min(1024, s)
    tq = min(512, s)
    tkc = min(1024, s)
    blk = 1024
    tm = min(256, s)
    hps = 8 if s * 8 * HEAD_PAD * 2 * 2 <= VMEM_LIMIT // 4 else 2
    return ts, tq, tkc, blk, tm, hps


def _ceil_to(x, m):
    return (x + m - 1) // m * m


def _trunk(x, c, prep, tiles=None):
    w_ada, b_ada, pre_w, post_w, exp_w, g_final = prep
    b, s, d = x.shape
    ts, tq, tkc, blk, tm, hps = tiles or _tiles(s)
    t = b * s
    nt = t // tm
    mod = _modulation(c, w_ada, b_ada)
    q, k, v, gm = _pre_call(x, mod, _rope_tables(s), pre_w, ts)
    attn = _attn_call(q, k, v, tq, tkc, hps)
    ybase, h2p, e_arr, lrank, w_arr, cnt = _post_call(attn, gm, x, mod, post_w, min(ts, 512), tm)

    i32 = jnp.int32
    cnt = cnt[:, :, 0].astype(i32).reshape(nt, GROUP_SIZE, N_GROUPS).transpose(0, 2, 1).reshape(nt, N_EXPERTS)
    cnt8 = _ceil_to(cnt, ROW_ALIGN)
    base8 = jnp.cumsum(cnt8, axis=0) - cnt8
    total8 = jnp.sum(cnt8, axis=0)
    padded = _ceil_to(total8 + EXPERT_SLACK, blk)
    pends = jnp.cumsum(padded).astype(i32)
    zfrom = ((pends - padded + total8) // blk * blk).astype(i32)
    dstbase = (pends - padded)[None, :] + base8
    toff8 = jnp.cumsum(cnt8, axis=1) - cnt8
    eids = jnp.arange(N_EXPERTS, dtype=i32)

    def chunk_table(nch, chunk, n_max):
        cend = jnp.cumsum(nch, axis=1)
        cidx = jnp.arange(n_max, dtype=i32)
        e_of_c = jnp.minimum(jnp.sum((cend[:, None, :] <= cidx[None, :, None]).astype(i32), axis=-1),
                             N_EXPERTS - 1)
        pick = lambda tbl: jnp.sum(jnp.where(e_of_c[..., None] == eids, tbl[:, None, :], 0), axis=-1)
        rel = lambda first_row: pick(first_row - (cend - nch) * chunk) + cidx[None, :] * chunk
        return rel, cend[:, -1]

    n_dmax = N_EXPERTS + TOP_K * tm // DISP_CHUNK
    nch_d = (cnt + DISP_CHUNK - 1) // DISP_CHUNK
    rel_d, n_dch = chunk_table(nch_d, DISP_CHUNK, n_dmax)
    n_rb = (jnp.sum(cnt8, axis=1) + DISP_CHUNK + PERM_ROWS - 1) // PERM_ROWS
    tail = lambda a, b2, n: jnp.concatenate([a[:, None], b2[:, None], jnp.zeros((nt, n - 2), i32)], axis=1)
    dinfo = jnp.stack([rel_d(toff8), rel_d(dstbase), tail(n_dch, n_rb, n_dmax)], axis=1).astype(i32)

    shift = dstbase % COMB_CHUNK
    nch_c = jnp.where(cnt > 0, (cnt + shift + COMB_CHUNK - 1) // COMB_CHUNK, 0)
    boff = (jnp.cumsum(nch_c, axis=1) - nch_c) * COMB_CHUNK
    big = COMB_CHUNK_TILES[0]
    n_cmax = max(_gather_rows(tm) // COMB_CHUNK // big, N_EXPERTS)
    src_t = (dstbase - shift) // COMB_CHUNK
    dst_t = boff // COMB_CHUNK
    done = nch_c // big * big
    rows, totals = [], []
    for size in COMB_CHUNK_TILES:
        n_size = nch_c // big if size == big else (nch_c - done == size).astype(i32)
        first = 0 if size == big else done
        rel, total = chunk_table(n_size, size, n_cmax)
        rows += [rel(src_t + first), rel(dst_t + first)]
        totals.append(total[:, None])
    n_kt = (jnp.sum(nch_c, axis=1) * COMB_CHUNK + COMB_KTILE - 1) // COMB_KTILE
    counts = jnp.concatenate(totals + [n_kt[:, None], jnp.zeros((nt, n_cmax - len(totals) - 1), i32)], axis=1)
    cinfo = jnp.stack(rows + [counts], axis=1).astype(i32)

    onehot = e_arr[..., None] == eids
    tbl = jnp.stack([toff8, boff + shift], axis=-1)
    base = jnp.sum(jnp.where(onehot[..., None], tbl[:, None, None, :, :], 0), axis=-2)
    lpos = base[..., 0] + lrank
    cpos = base[..., 1] + lrank
    tok_major = lambda a: jnp.transpose(a, (0, 2, 1)).reshape(t, TOP_K)

    n_rows = _ceil_to(t * TOP_K + nt * N_EXPERTS * (ROW_ALIGN - 1) + N_EXPERTS * (EXPERT_SLACK + blk - 1), blk)
    n_blocks = n_rows // blk
    n_used = (pends[-1] // blk).astype(i32).reshape(1)
    block_start = jnp.arange(n_blocks, dtype=i32) * blk
    block_exp = jnp.minimum(jnp.sum((pends[None, :] <= block_start[:, None]).astype(i32), axis=1),
                            N_EXPERTS - 1)

    xs = _disp_call(pends, zfrom, dinfo, lpos.astype(i32), h2p, n_rows, tm, blk)
    ys = _exp_call(block_exp, n_used, xs, exp_w[0], exp_w[1], blk)
    ys3 = ys.reshape(n_rows // COMB_CHUNK, COMB_CHUNK, d)
    return _comb_call(cinfo, ybase, mod, tok_major(cpos).astype(i32), tok_major(w_arr), g_final, ys3, tm)


def kernel(x_prompt, x_sample, c_prompt, c_sample, w_ada, b_ada, g_norm1, w_in, g_q_lat, w_uq, g_kv_lat, w_ukv, g_gmlp_v, w_spatial, b_spatial, g_out_attn, g_out_gmlp, w_out, g_norm2, w_router, b_router, w_gate_e, w_up_e, w_down_e, w_gate_s, w_up_s, w_down_s, g_final):
    prep = _prep_weights(w_ada[0], b_ada[0], g_norm1[0], w_in[0], g_q_lat[0], w_uq[0], g_kv_lat[0],
                         w_ukv[0], g_gmlp_v[0], w_spatial[0], b_spatial[0], g_out_attn[0],
                         g_out_gmlp[0], w_out[0], g_norm2[0], w_router[0], b_router[0], w_gate_e[0],
                         w_up_e[0], w_down_e[0], w_gate_s[0], w_up_s[0], w_down_s[0], g_final)
    return (_trunk(x_prompt, c_prompt, prep), _trunk(x_sample, c_sample, prep))
```

```python
import functools
import math

import jax
import jax.numpy as jnp
from jax import lax
from jax.experimental import pallas as pl
from jax.experimental.pallas import tpu as pltpu

F32 = jnp.float32
BF16 = jnp.bfloat16

D_MODEL = 1024
N_HEADS = 8
QK_NOPE = 64
QK_ROPE = 32
V_DIM = 64
Q_LORA = 256
KV_LORA = 128
GMLP_W = 512
CHUNK = 128
N_EXPERTS = 64
TOP_K = 8
N_GROUPS = 8
TOPK_GROUPS = 4
GROUP_SIZE = N_EXPERTS // N_GROUPS
EXPERT_DIM = 256
SHARED_DIM = 256
ROUTED_SCALE = 2.5
ROPE_THETA = 10000.0
N_MOD = 6
EPS = 1e-6

LANES = 128
HEAD_PAD = 128
HALF_ROPE = QK_ROPE // 2
VMEM_LIMIT = 52 * 1024 * 1024
ROW_ALIGN = 8
DISP_CHUNK = 48
COMB_CHUNK = 16
COMB_CHUNK_TILES = (3, 2, 1)
ATTN_CHUNKS_PER_STEP = 8
PERM_ROWS = 512
COMB_KTILE = 1024
EXPERT_SLACK = max(DISP_CHUNK, 2 * (COMB_CHUNK - 1))

SOFTMAX_SCALE = (QK_NOPE + QK_ROPE) ** -0.5
EXP2_SCALE = SOFTMAX_SCALE * math.log2(math.e)


def _rms(x, g):
    return x * lax.rsqrt(jnp.mean(x * x, axis=-1, keepdims=True) + EPS) * g


def _sigmoid(x):
    return 1.0 / (1.0 + jnp.exp(-x))


def _gelu_tanh(x):
    c = math.sqrt(2.0 / math.pi)
    return 0.5 * x * (1.0 + jnp.tanh(c * (x + 0.044715 * (x * x * x))))


def _dot(a, b):
    return jnp.dot(a, b, preferred_element_type=F32)


def _dot_nt(a, b):
    return lax.dot_general(a, b, (((1,), (1,)), ((), ())), preferred_element_type=F32)


def _pack_bf16_pairs(x):
    c = x.shape[1] // 2
    lo = pltpu.bitcast(x[:, :c].astype(BF16).astype(F32), jnp.uint32)
    hi = pltpu.bitcast(x[:, c:].astype(BF16).astype(F32), jnp.uint32)
    return (hi & jnp.uint32(0xFFFF0000)) | (lo >> 16)


def _unpack_bf16_pairs(w):
    lo = pltpu.bitcast(w << 16, F32)
    hi = pltpu.bitcast(w & jnp.uint32(0xFFFF0000), F32)
    return lo, hi


def _mod_kernel(c_ref, w_ref, b_ref, o_ref):
    c = c_ref[...]
    a = (c * _sigmoid(c)).astype(BF16)
    o_ref[...] = _dot(a, w_ref[...]) + b_ref[...]


def _modulation(c, w_ada_bf, b_ada):
    b = c.shape[0]
    bp = max(16, -(-b // 16) * 16)
    cp = jnp.pad(c, ((0, bp - b), (0, 0)))
    n = w_ada_bf.shape[1]
    tn = D_MODEL
    out = pl.pallas_call(
        _mod_kernel,
        name="mod",
        grid=(n // tn,),
        in_specs=[
            pl.BlockSpec((bp, D_MODEL), lambda j: (0, 0)),
            pl.BlockSpec((D_MODEL, tn), lambda j: (0, j)),
            pl.BlockSpec((1, tn), lambda j: (0, j)),
        ],
        out_specs=pl.BlockSpec((bp, tn), lambda j: (0, j)),
        out_shape=jax.ShapeDtypeStruct((bp, n), F32),
        compiler_params=pltpu.CompilerParams(dimension_semantics=("arbitrary",)),
    )(cp, w_ada_bf, b_ada.reshape(1, n))
    return out[:b].reshape(b, N_MOD, D_MODEL)


def _rope(xh, c, s1, s2):
    return (xh * c + pltpu.roll(xh, LANES - HALF_ROPE, axis=1) * s1
            + pltpu.roll(xh, HALF_ROPE, axis=1) * s2)


def _pre_kernel(x_ref, mod_ref, cos_ref, s1_ref, s2_ref, g1_ref, win_ref, gq_ref, wuq_ref,
                gkv_ref, wuk_ref, wuv_ref, vone_ref, ggv_ref, ws_ref, bs_ref, ggo_ref,
                q_ref, k_ref, v_ref, gm_ref, mix_ref):
    ts = x_ref.shape[1]
    x = x_ref[0]
    mod = mod_ref[0]
    h = _rms(x, g1_ref[...]) * (1.0 + mod[1:2]) + mod[0:1]
    z = _dot(h.astype(BF16), win_ref[...])
    o_kv = Q_LORA
    o_gu = o_kv + KV_LORA
    o_gv = o_gu + GMLP_W
    o_kr = o_gv + GMLP_W
    q_lat = z[:, :o_kv]
    kv_lat = z[:, o_kv:o_gu]
    g_u = z[:, o_gu:o_gv]
    g_v = z[:, o_gv:o_kr]
    kr = z[:, o_kr:o_kr + LANES]

    cos = cos_ref[...]
    s1 = s1_ref[...]
    s2 = s2_ref[...]

    qn = _rms(q_lat, gq_ref[...]).astype(BF16)
    q = _dot(qn, wuq_ref[...])
    kn = _rms(kv_lat, gkv_ref[...]).astype(BF16)
    kf = _dot(kn, wuk_ref[...])
    v_ref[0] = (_dot(kn, wuv_ref[...]) + vone_ref[...]).astype(BF16)
    krr = _rope(kr, cos, s1, s2)
    for hd in range(N_HEADS):
        sl = slice(hd * HEAD_PAD, (hd + 1) * HEAD_PAD)
        q_ref[0, :, sl] = (_rope(q[:, sl], cos, s1, s2) * EXP2_SCALE).astype(BF16)
        k_ref[0, sl, :] = jnp.transpose(kf[:, sl] + krr).astype(BF16)

    u = _gelu_tanh(g_u)
    vn = _rms(_gelu_tanh(g_v), ggv_ref[...]).astype(BF16)
    lane = lax.broadcasted_iota(jnp.int32, (CHUNK, LANES), 1)
    left = lane < (LANES // 2)
    zero = jnp.zeros((CHUNK, LANES), BF16)
    for n in range(ts // CHUNK):
        rs = slice(n * CHUNK, (n + 1) * CHUNK)
        for p in range(GMLP_W // LANES):
            cs = slice(p * LANES, (p + 1) * LANES)
            vp = vn[rs, cs]
            rhs = jnp.concatenate([jnp.where(left, vp, zero), jnp.where(left, zero, vp)], axis=0)
            mix_ref[rs, cs] = _dot(ws_ref[p], rhs) + bs_ref[:, cs]
    gm = u * mix_ref[...]
    gm_ref[0] = _rms(gm, ggo_ref[...]).astype(BF16)


def _pre_call(x, mod, tabs, wts, ts):
    b, s, d = x.shape
    cos, s1, s2 = tabs
    (g1, win, gq, wuq, gkv, wuk, wuv, vone, ggv, wsp, bsp, ggo) = wts
    full = lambda a: pl.BlockSpec(a.shape, lambda bi, i: (0,) * a.ndim)
    tab = pl.BlockSpec((ts, LANES), lambda bi, i: (i, 0))
    out_shapes = (
        jax.ShapeDtypeStruct((b, s, N_HEADS * HEAD_PAD), BF16),
        jax.ShapeDtypeStruct((b, N_HEADS * HEAD_PAD, s), BF16),
        jax.ShapeDtypeStruct((b, s, N_HEADS * HEAD_PAD), BF16),
        jax.ShapeDtypeStruct((b, s, GMLP_W), BF16),
    )
    tok = lambda w: pl.BlockSpec((1, ts, w), lambda bi, i: (bi, i, 0))
    tok_t = pl.BlockSpec((1, N_HEADS * HEAD_PAD, ts), lambda bi, i: (bi, 0, i))
    return pl.pallas_call(
        _pre_kernel,
        name="pre",
        grid=(b, s // ts),
        in_specs=[tok(d), pl.BlockSpec((1, N_MOD, d), lambda bi, i: (bi, 0, 0)), tab, tab, tab,
                  full(g1), full(win), full(gq), full(wuq), full(gkv), full(wuk), full(wuv), full(vone),
                  full(ggv), full(wsp), full(bsp), full(ggo)],
        out_specs=(tok(N_HEADS * HEAD_PAD), tok_t, tok(N_HEADS * HEAD_PAD), tok(GMLP_W)),
        out_shape=out_shapes,
        scratch_shapes=[pltpu.VMEM((ts, GMLP_W), F32)],
        compiler_params=pltpu.CompilerParams(dimension_semantics=("arbitrary", "arbitrary"),
                                             vmem_limit_bytes=VMEM_LIMIT),
    )(x, mod, cos, s1, s2, g1, win, gq, wuq, gkv, wuk, wuv, vone, ggv, wsp, bsp, ggo)


def _attn_kernel(q_ref, k_ref, v_ref, o_ref, *, tkc, cpb):
    tq = q_ref.shape[1]
    n_chunks = v_ref.shape[1] // tkc
    hps = q_ref.shape[2] // HEAD_PAD
    heads = [slice(hh * HEAD_PAD, (hh + 1) * HEAD_PAD) for hh in range(hps)]
    qs = [q_ref[0, :, hs] for hs in heads]

    def step(i, carry):
        carry = list(carry)
        for cc in range(cpb):
            off = pl.multiple_of((i * cpb + cc) * tkc, tkc)
            for hh, hs in enumerate(heads):
                m, acc = carry[2 * hh], carry[2 * hh + 1]
                kc = k_ref[0, hs, pl.ds(off, tkc)]
                vc = v_ref[0, pl.ds(off, tkc), hs]
                sc = _dot(qs[hh], kc)
                m_new = jnp.maximum(m, jnp.max(sc, axis=-1, keepdims=True))
                p = jnp.exp2(sc - m_new).astype(BF16)
                carry[2 * hh + 1] = jnp.exp2(m - m_new) * acc + _dot(p, vc)
                carry[2 * hh] = m_new
        return tuple(carry)

    n_steps = n_chunks // cpb
    init = (jnp.full((tq, 1), -jnp.inf, F32), jnp.zeros((tq, HEAD_PAD), F32)) * hps
    res = step(0, init) if n_steps == 1 else lax.fori_loop(0, n_steps, step, init)
    lane = lax.broadcasted_iota(jnp.int32, (tq, HEAD_PAD), 1)
    for pair in range(hps // 2):
        a0, a1 = res[4 * pair + 1], res[4 * pair + 3]
        o0 = a0 / a0[:, V_DIM:V_DIM + 1]
        o1 = a1 / a1[:, V_DIM:V_DIM + 1]
        o_ref[0, :, pair * HEAD_PAD:(pair + 1) * HEAD_PAD] = jnp.where(
            lane < V_DIM, o0, pltpu.roll(o1, V_DIM, axis=1)).astype(BF16)


def _attn_call(q, k, v, tq, tkc, hps):
    b, s, _ = q.shape
    kv_bytes = 2 * s * hps * HEAD_PAD * 2
    mode = pl.Buffered(2 if 2 * kv_bytes <= VMEM_LIMIT // 3 else 1)
    k_res = pl.BlockSpec((1, hps * HEAD_PAD, s), lambda bi, h, i: (bi, h, 0), pipeline_mode=mode)
    v_res = pl.BlockSpec((1, s, hps * HEAD_PAD), lambda bi, h, i: (bi, 0, h), pipeline_mode=mode)
    return pl.pallas_call(
        functools.partial(_attn_kernel, tkc=tkc, cpb=math.gcd(s // tkc, ATTN_CHUNKS_PER_STEP)),
        name="attn",
        grid=(b, N_HEADS // hps, s // tq),
        in_specs=[pl.BlockSpec((1, tq, hps * HEAD_PAD), lambda bi, h, i: (bi, i, h)), k_res, v_res],
        out_specs=pl.BlockSpec((1, tq, hps * V_DIM), lambda bi, h, i: (bi, i, h)),
        out_shape=jax.ShapeDtypeStruct((b, s, N_HEADS * V_DIM), BF16),
        compiler_params=pltpu.CompilerParams(
            dimension_semantics=("arbitrary", "arbitrary", "arbitrary"),
            vmem_limit_bytes=VMEM_LIMIT),
    )(q, k, v)


def _post_kernel(attn_ref, gm_ref, x_ref, mod_ref, goa_ref, woa_ref, wog_ref, g2_ref, wr_ref,
                 br_ref, wgus_ref, wds_ref,
                 ybase_ref, h2p_ref, e_ref, rank_ref, w_ref, cnt_ref):
    ts = x_ref.shape[1]
    mod = mod_ref[0]
    ga1, sh2, sc2, ga2 = mod[2:3], mod[3:4], mod[4:5], mod[5:6]
    an = _rms(attn_ref[0].astype(F32), goa_ref[...]).astype(BF16)
    y = _dot(an, woa_ref[...]) + _dot(gm_ref[0], wog_ref[...])
    x1 = x_ref[0] + ga1 * y
    h2 = _rms(x1, g2_ref[...]) * (1.0 + sc2) + sh2
    h2b = h2.astype(BF16)
    h2p_ref[...] = _pack_bf16_pairs(h2)

    gu = _dot(h2b, wgus_ref[...])
    g, u = gu[:, :SHARED_DIM], gu[:, SHARED_DIM:]
    a = (g * _sigmoid(g) * u).astype(BF16)
    ybase_ref[0] = x1 + ga2 * _dot(a, wds_ref[...])

    logits = _dot_nt(wr_ref[...], h2b)
    scores = _sigmoid(logits)
    biased = scores + br_ref[...]
    ninf = jnp.float32(-jnp.inf)
    bj = [biased[j * N_GROUPS:(j + 1) * N_GROUPS] for j in range(GROUP_SIZE)]
    sj = [scores[j * N_GROUPS:(j + 1) * N_GROUPS] for j in range(GROUP_SIZE)]
    m1 = bj[0]
    for j in range(1, GROUP_SIZE):
        m1 = jnp.maximum(m1, bj[j])
    found = jnp.zeros_like(m1)
    m2 = jnp.full_like(m1, ninf)
    for j in range(GROUP_SIZE):
        eq = jnp.where(bj[j] == m1, 1.0, 0.0)
        is_first = eq * (1.0 - found)
        found = jnp.maximum(found, eq)
        m2 = jnp.maximum(m2, jnp.where(is_first > 0.0, ninf, bj[j]))
    gs = m1 + m2
    gidx = lax.broadcasted_iota(jnp.int32, gs.shape, 0)
    grank = jnp.zeros_like(gs)
    for kk in range(1, N_GROUPS):
        r = pltpu.roll(gs, kk, axis=0)
        grank = grank + jnp.where(gidx >= kk, jnp.where(r >= gs, 1.0, 0.0), jnp.where(r > gs, 1.0, 0.0))
    gsel = grank < float(TOPK_GROUPS)
    masked = [jnp.where(gsel, bj[j], ninf) for j in range(GROUP_SIZE)]
    eidx = [gidx * GROUP_SIZE + j for j in range(GROUP_SIZE)]

    selm = [jnp.zeros_like(gs) for _ in range(GROUP_SIZE)]
    e_sel = []
    for _k in range(TOP_K):
        m = masked[0]
        for j in range(1, GROUP_SIZE):
            m = jnp.maximum(m, masked[j])
        m = jnp.max(m, axis=0, keepdims=True)
        cand = jnp.where(masked[0] == m, eidx[0], N_EXPERTS)
        for j in range(1, GROUP_SIZE):
            cand = jnp.minimum(cand, jnp.where(masked[j] == m, eidx[j], N_EXPERTS))
        emin = jnp.min(cand, axis=0, keepdims=True)
        e_sel.append(emin)
        for j in range(GROUP_SIZE):
            hit = eidx[j] == emin
            selm[j] = jnp.where(hit, 1.0, selm[j])
            masked[j] = jnp.where(hit, ninf, masked[j])

    wsel = [selm[j] * sj[j] for j in range(GROUP_SIZE)]
    tot = wsel[0]
    for j in range(1, GROUP_SIZE):
        tot = tot + wsel[j]
    tot = jnp.sum(tot, axis=0, keepdims=True)
    wn = [wsel[j] / tot * ROUTED_SCALE for j in range(GROUP_SIZE)]

    tm = e_ref.shape[2]
    sel = jnp.concatenate(selm, axis=0)
    tr = lax.broadcasted_iota(jnp.int32, (ts, ts), 0)
    tc = lax.broadcasted_iota(jnp.int32, (ts, ts), 1)
    sh = tm.bit_length() - 1
    same_tile = lax.shift_right_logical(tr, sh) == lax.shift_right_logical(tc, sh)
    upper = jnp.where(tr < tc, jnp.where(same_tile, 1.0, 0.0), 0.0).astype(BF16)
    rank_full = _dot(sel.astype(BF16), upper)
    rj = [rank_full[j * N_GROUPS:(j + 1) * N_GROUPS] for j in range(GROUP_SIZE)]
    subs = [slice(i * tm, (i + 1) * tm) for i in range(ts // tm)]
    for i, sub in enumerate(subs):
        cnt_ref[i] = jnp.broadcast_to(jnp.sum(sel[:, sub], axis=1, keepdims=True), (N_EXPERTS, LANES))

    for k in range(TOP_K):
        rk = jnp.zeros_like(gs)
        wk = jnp.zeros_like(gs)
        for j in range(GROUP_SIZE):
            hit = eidx[j] == e_sel[k]
            rk = rk + jnp.where(hit, rj[j], 0.0)
            wk = wk + jnp.where(hit, wn[j], 0.0)
        rk = jnp.sum(rk, axis=0, keepdims=True).astype(jnp.int32)
        wk = jnp.sum(wk, axis=0, keepdims=True)
        for i, sub in enumerate(subs):
            e_ref[i, k:k + 1, :] = e_sel[k][:, sub]
            rank_ref[i, k:k + 1, :] = rk[:, sub]
            w_ref[i, k:k + 1, :] = wk[:, sub]


def _post_call(attn, gm, x, mod, wts, ts, tm):
    b, s, d = x.shape
    nt = s // ts
    nsub = ts // tm
    ntm = b * nt * nsub
    (goa, woa, wog, g2, wr, br, wgus, wds) = wts
    full = lambda a: pl.BlockSpec(a.shape, lambda bi, i: (0,) * a.ndim)
    tok = lambda w: pl.BlockSpec((1, ts, w), lambda bi, i: (bi, i, 0))
    rout = pl.BlockSpec((nsub, TOP_K, tm), lambda bi, i: (bi * nt + i, 0, 0))
    out_shapes = (
        jax.ShapeDtypeStruct((b, s, d), F32),
        jax.ShapeDtypeStruct((b * s, d // 2), jnp.uint32),
        jax.ShapeDtypeStruct((ntm, TOP_K, tm), jnp.int32),
        jax.ShapeDtypeStruct((ntm, TOP_K, tm), jnp.int32),
        jax.ShapeDtypeStruct((ntm, TOP_K, tm), F32),
        jax.ShapeDtypeStruct((ntm, N_EXPERTS, LANES), F32),
    )
    return pl.pallas_call(
        _post_kernel,
        name="post",
        grid=(b, nt),
        in_specs=[tok(N_HEADS * V_DIM), tok(GMLP_W), tok(d),
                  pl.BlockSpec((1, N_MOD, d), lambda bi, i: (bi, 0, 0)),
                  full(goa), full(woa), full(wog), full(g2), full(wr), full(br), full(wgus), full(wds)],
        out_specs=(tok(d), pl.BlockSpec((ts, d // 2), lambda bi, i: (bi * nt + i, 0)),
                   rout, rout, rout,
                   pl.BlockSpec((nsub, N_EXPERTS, LANES), lambda bi, i: (bi * nt + i, 0, 0))),
        out_shape=out_shapes,
        compiler_params=pltpu.CompilerParams(dimension_semantics=("arbitrary", "arbitrary"),
                                             vmem_limit_bytes=VMEM_LIMIT),
    )(attn, gm, x, mod, goa, woa, wog, g2, wr, br, wgus, wds)


def _disp_kernel(pends_ref, zfrom_ref, info_ref, lpos_ref, h_ref, xs_ref, sbuf, zbuf, nprev, sems, zsem):
    td = h_ref.shape[0]
    blk = zbuf.shape[0]
    step = pl.program_id(0)
    slot = lax.rem(step, 2)

    @pl.when(step == 0)
    def _():
        nprev[0] = 0
        zbuf[...] = jnp.zeros_like(zbuf)

        def tail_copy(e, j):
            start = pl.multiple_of(zfrom_ref[e] + j * blk, blk)
            return pltpu.make_async_copy(zbuf, xs_ref.at[pl.ds(start, blk), :], zsem)

        def n_tail(e):
            return (pends_ref[e] - zfrom_ref[e]) // blk

        def zstart(e, c):
            def one(j, c2):
                tail_copy(e, j).start()
                return c2
            return lax.fori_loop(0, n_tail(e), one, c)

        def zwait(e, c):
            def one(j, c2):
                tail_copy(e, j).wait()
                return c2
            return lax.fori_loop(0, n_tail(e), one, c)

        lax.fori_loop(0, N_EXPERTS, zstart, 0)
        lax.fori_loop(0, N_EXPERTS, zwait, 0)

    lo, hi = _unpack_bf16_pairs(h_ref[...])
    lo = lo.astype(BF16)
    hi = hi.astype(BF16)
    lpos = lpos_ref[0]
    lpos_b = [jnp.broadcast_to(lpos[k:k + 1, :], (PERM_ROWS, td)).astype(jnp.int16) for k in range(TOP_K)]
    row0 = lax.broadcasted_iota(jnp.int32, (PERM_ROWS, td), 0)
    one = jnp.ones((PERM_ROWS, td), BF16)

    def build(rb, c):
        r0 = pl.multiple_of(rb * PERM_ROWS, PERM_ROWS)
        riota = (row0 + r0).astype(jnp.int16)
        pb = jnp.zeros((PERM_ROWS, td), BF16)
        for k in range(TOP_K):
            pb = jnp.where(lpos_b[k] == riota, one, pb)
        xlo = pltpu.bitcast(_dot(pb, lo), jnp.uint32)
        xhi = pltpu.bitcast(_dot(pb, hi), jnp.uint32)
        sbuf[slot, pl.ds(r0, PERM_ROWS), :] = (xhi & jnp.uint32(0xFFFF0000)) | (xlo >> 16)
        return c

    lax.fori_loop(0, info_ref[0, 2, 1], build, 0)

    def chunk_copy(src, dst, sl):
        return pltpu.make_async_copy(
            sbuf.at[sl, pl.ds(pl.multiple_of(src, ROW_ALIGN), DISP_CHUNK), :],
            xs_ref.at[pl.ds(pl.multiple_of(dst, ROW_ALIGN), DISP_CHUNK), :], sems.at[sl])

    def drain(n, sl):
        def one(c, carry):
            chunk_copy(0, 0, sl).wait()
            return carry
        lax.fori_loop(0, n, one, 0)

    def issue(c, carry):
        chunk_copy(info_ref[0, 0, c], info_ref[0, 1, c], slot).start()
        return carry

    drain(nprev[0], 1 - slot)
    n_chunks = info_ref[0, 2, 0]
    lax.fori_loop(0, n_chunks, issue, 0)
    nprev[0] = n_chunks

    @pl.when(step == pl.num_programs(0) - 1)
    def _():
        drain(n_chunks, slot)


def _sorted_rows(td):
    need = TOP_K * td + N_EXPERTS * (ROW_ALIGN - 1) + DISP_CHUNK
    return -(-need // PERM_ROWS) * PERM_ROWS


def _disp_call(pends, zfrom, info, lpos, h2p, n_rows, td, blk):
    t, w = h2p.shape
    nt = t // td
    return pl.pallas_call(
        _disp_kernel,
        name="disp",
        grid_spec=pltpu.PrefetchScalarGridSpec(
            num_scalar_prefetch=2,
            grid=(nt,),
            in_specs=[pl.BlockSpec((1,) + info.shape[1:], lambda i, pe, pa: (i, 0, 0), memory_space=pltpu.SMEM),
                      pl.BlockSpec((1, TOP_K, td), lambda i, pe, pa: (i, 0, 0)),
                      pl.BlockSpec((td, w), lambda i, pe, pa: (i, 0))],
            out_specs=pl.BlockSpec(memory_space=pl.ANY),
            scratch_shapes=[pltpu.VMEM((2, _sorted_rows(td), w), jnp.uint32),
                            pltpu.VMEM((blk, w), jnp.uint32),
                            pltpu.SMEM((1,), jnp.int32),
                            pltpu.SemaphoreType.DMA((2,)), pltpu.SemaphoreType.DMA(())],
        ),
        out_shape=jax.ShapeDtypeStruct((n_rows, w), jnp.uint32),
        compiler_params=pltpu.CompilerParams(dimension_semantics=("arbitrary",),
                                             vmem_limit_bytes=VMEM_LIMIT),
    )(pends, zfrom, info, lpos, h2p)


def _exp_kernel(bexp_ref, nused_ref, xs_ref, wgu_ref, wd_ref, ys_ref):
    i = pl.program_id(0)

    @pl.when(i < nused_ref[0])
    def _():
        half = D_MODEL // 2
        lo, hi = _unpack_bf16_pairs(xs_ref[...])
        gu = (_dot(lo.astype(BF16), wgu_ref[0, :half, :])
              + _dot(hi.astype(BF16), wgu_ref[0, half:, :]))
        g, u = gu[:, :EXPERT_DIM], gu[:, EXPERT_DIM:]
        a = (g * _sigmoid(g) * u).astype(BF16)
        ys_ref[...] = _dot(a, wd_ref[0]).astype(BF16)


def _exp_call(block_exp, n_used, xs, wgu, wd, blk):
    n_rows, w = xs.shape
    n_blocks = n_rows // blk

    def row_map(i, bexp, nused):
        return (jnp.minimum(i, nused[0] - 1), 0)

    def w_map(i, bexp, nused):
        return (bexp[jnp.minimum(i, nused[0] - 1)], 0, 0)

    return pl.pallas_call(
        _exp_kernel,
        name="exp",
        grid_spec=pltpu.PrefetchScalarGridSpec(
            num_scalar_prefetch=2,
            grid=(n_blocks,),
            in_specs=[pl.BlockSpec((blk, w), row_map),
                      pl.BlockSpec((1, D_MODEL, 2 * EXPERT_DIM), w_map),
                      pl.BlockSpec((1, EXPERT_DIM, D_MODEL), w_map)],
            out_specs=pl.BlockSpec((blk, D_MODEL), row_map),
        ),
        out_shape=jax.ShapeDtypeStruct((n_rows, D_MODEL), BF16),
        compiler_params=pltpu.CompilerParams(dimension_semantics=("arbitrary",),
                                             vmem_limit_bytes=VMEM_LIMIT),
    )(block_exp, n_used, xs, wgu, wd)


def _comb_kernel(info_ref, next_info_ref, ybase_ref, mod_ref, cpos_ref, w_ref, gf_ref, ys_ref, o_ref,
                 ybuf, acc, sems):
    tc = ybase_ref.shape[1]
    step = pl.program_id(0) * pl.num_programs(1) + pl.program_id(1)
    n_steps = pl.num_programs(0) * pl.num_programs(1)
    slot = lax.rem(step, 2)

    def chunk_copy(src, dst, sl, n_tiles):
        return pltpu.make_async_copy(ys_ref.at[pl.ds(src, n_tiles)], ybuf.at[sl, pl.ds(dst, n_tiles)],
                                     sems.at[sl])

    def fetch(iref, sl):
        for i, n_tiles in enumerate(COMB_CHUNK_TILES):
            def issue(c, carry):
                chunk_copy(iref[0, 2 * i, c], iref[0, 2 * i + 1, c], sl, n_tiles).start()
                return carry
            lax.fori_loop(0, iref[0, 2 * len(COMB_CHUNK_TILES), i], issue, 0)

    @pl.when(step == 0)
    def _():
        ybuf[...] = jnp.zeros_like(ybuf)
        fetch(info_ref, 0)

    @pl.when(step + 1 < n_steps)
    def _():
        fetch(next_info_ref, 1 - slot)

    for i, n_tiles in enumerate(COMB_CHUNK_TILES):
        def drain(c, carry):
            chunk_copy(0, 0, slot, n_tiles).wait()
            return carry
        lax.fori_loop(0, info_ref[0, 2 * len(COMB_CHUNK_TILES), i], drain, 0)

    cpos = cpos_ref[...]
    w = w_ref[...]
    cpos_b = [jnp.broadcast_to(cpos[:, k:k + 1], (tc, LANES)).astype(jnp.int16) for k in range(TOP_K)]
    w_b = [jnp.broadcast_to(w[:, k:k + 1], (tc, LANES)).astype(BF16) for k in range(TOP_K)]
    lane = lax.broadcasted_iota(jnp.int32, (tc, LANES), 1)
    acc[...] = jnp.zeros_like(acc)

    def slab(kt, c):
        k0 = pl.multiple_of(kt * COMB_KTILE, COMB_KTILE)
        cols = []
        for j in range(COMB_KTILE // LANES):
            col = (lane + (k0 + j * LANES)).astype(jnp.int16)
            wm = jnp.zeros((tc, LANES), BF16)
            for k in range(TOP_K):
                wm = jnp.where(cpos_b[k] == col, w_b[k], wm)
            cols.append(wm)
        rows = ybuf[slot, pl.ds(kt * (COMB_KTILE // COMB_CHUNK), COMB_KTILE // COMB_CHUNK)]
        acc[...] += _dot(jnp.concatenate(cols, axis=1), rows.reshape(COMB_KTILE, rows.shape[-1]))
        return c

    lax.fori_loop(0, info_ref[0, 2 * len(COMB_CHUNK_TILES), len(COMB_CHUNK_TILES)], slab, 0)
    ga2 = mod_ref[0][5:6]
    o_ref[0] = _rms(ybase_ref[0] + ga2 * acc[...], gf_ref[...])


def _gather_rows(tc):
    need = TOP_K * tc + N_EXPERTS * 2 * (COMB_CHUNK - 1)
    return -(-need // COMB_KTILE) * COMB_KTILE


def _comb_call(info, ybase, mod, cpos_tok, w_tok, g_final, ys, tc):
    b, s, d = ybase.shape
    nt = s // tc
    return pl.pallas_call(
        _comb_kernel,
        name="comb",
        grid=(b, nt),
        in_specs=[pl.BlockSpec((1,) + info.shape[1:], lambda bi, i: (bi * nt + i, 0, 0), memory_space=pltpu.SMEM),
                  pl.BlockSpec((1,) + info.shape[1:], lambda bi, i: (jnp.minimum(bi * nt + i + 1, b * nt - 1), 0, 0),
                               memory_space=pltpu.SMEM),
                  pl.BlockSpec((1, tc, d), lambda bi, i: (bi, i, 0)),
                  pl.BlockSpec((1, N_MOD, d), lambda bi, i: (bi, 0, 0)),
                  pl.BlockSpec((tc, TOP_K), lambda bi, i: (bi * nt + i, 0)),
                  pl.BlockSpec((tc, TOP_K), lambda bi, i: (bi * nt + i, 0)),
                  pl.BlockSpec((1, d), lambda bi, i: (0, 0)),
                  pl.BlockSpec(memory_space=pl.ANY)],
        out_specs=pl.BlockSpec((1, tc, d), lambda bi, i: (bi, i, 0)),
        out_shape=jax.ShapeDtypeStruct((b, s, d), F32),
        scratch_shapes=[pltpu.VMEM((2, _gather_rows(tc) // COMB_CHUNK, COMB_CHUNK, d), BF16),
                        pltpu.VMEM((tc, d), F32), pltpu.SemaphoreType.DMA((2,))],
        compiler_params=pltpu.CompilerParams(dimension_semantics=("arbitrary", "arbitrary"),
                                             vmem_limit_bytes=VMEM_LIMIT),
    )(info, info, ybase, mod, cpos_tok, w_tok, g_final, ys)


def _prep_weights(w_ada, b_ada, g_norm1, w_in, g_q_lat, w_uq, g_kv_lat, w_ukv, g_gmlp_v, w_spatial,
                  b_spatial, g_out_attn, g_out_gmlp, w_out, g_norm2, w_router, b_router, w_gate_e,
                  w_up_e, w_down_e, w_gate_s, w_up_s, w_down_s, g_final):
    row = lambda g: g.reshape(1, -1).astype(F32)
    o1 = Q_LORA
    o2 = o1 + KV_LORA
    o3 = o2 + QK_ROPE
    o4 = o3 + GMLP_W
    kr_cols = jnp.pad(w_in[:, o2:o3], ((0, 0), (QK_NOPE, LANES - QK_NOPE - QK_ROPE)))
    win = jnp.concatenate([w_in[:, :o2], w_in[:, o3:o4], w_in[:, o4:], kr_cols], axis=1).astype(BF16)
    qd = QK_NOPE + QK_ROPE
    wuq = jnp.pad(w_uq.reshape(Q_LORA, N_HEADS, qd), ((0, 0), (0, 0), (0, HEAD_PAD - qd)))
    wuq = wuq.reshape(Q_LORA, N_HEADS * HEAD_PAD).astype(BF16)
    wkv = w_ukv.reshape(KV_LORA, N_HEADS, QK_NOPE + V_DIM)
    wuk = jnp.pad(wkv[:, :, :QK_NOPE], ((0, 0), (0, 0), (0, HEAD_PAD - QK_NOPE)))
    wuk = wuk.reshape(KV_LORA, N_HEADS * HEAD_PAD).astype(BF16)
    wuv = jnp.pad(wkv[:, :, QK_NOPE:], ((0, 0), (0, 0), (0, HEAD_PAD - V_DIM)))
    wuv = wuv.reshape(KV_LORA, N_HEADS * HEAD_PAD).astype(BF16)
    vone = jnp.tile((jnp.arange(HEAD_PAD) == V_DIM).astype(F32), N_HEADS).reshape(1, -1)
    wsp = w_spatial.reshape(N_HEADS // 2, 2, CHUNK, CHUNK).transpose(0, 2, 1, 3)
    wsp = wsp.reshape(N_HEADS // 2, CHUNK, 2 * CHUNK).astype(BF16)
    bsp = jnp.repeat(jnp.transpose(b_spatial), GMLP_W // N_HEADS, axis=1).astype(F32)
    pre = (row(g_norm1), win, row(g_q_lat), wuq, row(g_kv_lat), wuk, wuv, vone, row(g_gmlp_v), wsp,
           bsp, row(g_out_gmlp))
    perm = (jnp.arange(N_GROUPS)[None, :] * GROUP_SIZE + jnp.arange(GROUP_SIZE)[:, None]).reshape(-1)
    wr = jnp.transpose(w_router)[perm].astype(BF16)
    br = b_router.astype(F32)[perm].reshape(N_EXPERTS, 1)
    mla_w = N_HEADS * V_DIM
    wgus = jnp.concatenate([w_gate_s, w_up_s], axis=1).astype(BF16)
    post = (row(g_out_attn), w_out[:mla_w].astype(BF16), w_out[mla_w:].astype(BF16), row(g_norm2),
            wr, br, wgus, w_down_s.astype(BF16))
    wgu_e = jnp.concatenate([w_gate_e, w_up_e], axis=2).astype(BF16)
    wd_e = w_down_e.astype(BF16)
    return w_ada.astype(BF16), b_ada, pre, post, (wgu_e, wd_e), row(g_final)


def _rope_tables(s):
    inv = 1.0 / (ROPE_THETA ** (jnp.arange(0, QK_ROPE, 2, dtype=F32) / QK_ROPE))
    ang = jnp.arange(s, dtype=F32)[:, None] * inv[None, :]
    cos, sin = jnp.cos(ang), jnp.sin(ang)
    z = lambda n: jnp.zeros((s, n), F32)
    tail = LANES - QK_NOPE - QK_ROPE
    c = jnp.concatenate([jnp.ones((s, QK_NOPE), F32), cos, cos, z(tail)], axis=1)
    s1 = jnp.concatenate([z(QK_NOPE), -sin, z(HALF_ROPE), z(tail)], axis=1)
    s2 = jnp.concatenate([z(QK_NOPE), z(HALF_ROPE), sin, z(tail)], axis=1)
    return c, s1, s2


def _tiles(s):
    ts = min(1024, s)
    tq = min(512, s)
    tkc = min(1024, s)
    blk = 1024
    tm = min(256, s)
    hps = 8 if s * 8 * HEAD_PAD * 2 * 2 <= VMEM_LIMIT // 4 else 2
    return ts, tq, tkc, blk, tm, hps


def _ceil_to(x, m):
    return (x + m - 1) // m * m


def _trunk(x, c, prep, tiles=None):
    w_ada, b_ada, pre_w, post_w, exp_w, g_final = prep
    b, s, d = x.shape
    ts, tq, tkc, blk, tm, hps = tiles or _tiles(s)
    t = b * s
    nt = t // tm
    mod = _modulation(c, w_ada, b_ada)
    q, k, v, gm = _pre_call(x, mod, _rope_tables(s), pre_w, ts)
    attn = _attn_call(q, k, v, tq, tkc, hps)
    ybase, h2p, e_arr, lrank, w_arr, cnt = _post_call(attn, gm, x, mod, post_w, min(ts, 512), tm)

    i32 = jnp.int32
    cnt = cnt[:, :, 0].astype(i32).reshape(nt, GROUP_SIZE, N_GROUPS).transpose(0, 2, 1).reshape(nt, N_EXPERTS)
    cnt8 = _ceil_to(cnt, ROW_ALIGN)
    base8 = jnp.cumsum(cnt8, axis=0) - cnt8
    total8 = jnp.sum(cnt8, axis=0)
    padded = _ceil_to(total8 + EXPERT_SLACK, blk)
    pends = jnp.cumsum(padded).astype(i32)
    zfrom = ((pends - padded + total8) // blk * blk).astype(i32)
    dstbase = (pends - padded)[None, :] + base8
    toff8 = jnp.cumsum(cnt8, axis=1) - cnt8
    eids = jnp.arange(N_EXPERTS, dtype=i32)

    def chunk_table(nch, chunk, n_max):
        cend = jnp.cumsum(nch, axis=1)
        cidx = jnp.arange(n_max, dtype=i32)
        e_of_c = jnp.minimum(jnp.sum((cend[:, None, :] <= cidx[None, :, None]).astype(i32), axis=-1),
                             N_EXPERTS - 1)
        pick = lambda tbl: jnp.sum(jnp.where(e_of_c[..., None] == eids, tbl[:, None, :], 0), axis=-1)
        rel = lambda first_row: pick(first_row - (cend - nch) * chunk) + cidx[None, :] * chunk
        return rel, cend[:, -1]

    n_dmax = N_EXPERTS + TOP_K * tm // DISP_CHUNK
    nch_d = (cnt + DISP_CHUNK - 1) // DISP_CHUNK
    rel_d, n_dch = chunk_table(nch_d, DISP_CHUNK, n_dmax)
    n_rb = (jnp.sum(cnt8, axis=1) + DISP_CHUNK + PERM_ROWS - 1) // PERM_ROWS
    tail = lambda a, b2, n: jnp.concatenate([a[:, None], b2[:, None], jnp.zeros((nt, n - 2), i32)], axis=1)
    dinfo = jnp.stack([rel_d(toff8), rel_d(dstbase), tail(n_dch, n_rb, n_dmax)], axis=1).astype(i32)

    shift = dstbase % COMB_CHUNK
    nch_c = jnp.where(cnt > 0, (cnt + shift + COMB_CHUNK - 1) // COMB_CHUNK, 0)
    boff = (jnp.cumsum(nch_c, axis=1) - nch_c) * COMB_CHUNK
    big = COMB_CHUNK_TILES[0]
    n_cmax = max(_gather_rows(tm) // COMB_CHUNK // big, N_EXPERTS)
    src_t = (dstbase - shift) // COMB_CHUNK
    dst_t = boff // COMB_CHUNK
    done = nch_c // big * big
    rows, totals = [], []
    for size in COMB_CHUNK_TILES:
        n_size = nch_c // big if size == big else (nch_c - done == size).astype(i32)
        first = 0 if size == big else done
        rel, total = chunk_table(n_size, size, n_cmax)
        rows += [rel(src_t + first), rel(dst_t + first)]
        totals.append(total[:, None])
    n_kt = (jnp.sum(nch_c, axis=1) * COMB_CHUNK + COMB_KTILE - 1) // COMB_KTILE
    counts = jnp.concatenate(totals + [n_kt[:, None], jnp.zeros((nt, n_cmax - len(totals) - 1), i32)], axis=1)
    cinfo = jnp.stack(rows + [counts], axis=1).astype(i32)

    onehot = e_arr[..., None] == eids
    lookup = lambda tbl: jnp.sum(jnp.where(onehot, tbl[:, None, None, :], 0), axis=-1)
    lpos = lookup(toff8) + lrank
    cpos = lookup(boff + shift) + lrank
    tok_major = lambda a: jnp.transpose(a, (0, 2, 1)).reshape(t, TOP_K)

    n_rows = _ceil_to(t * TOP_K + nt * N_EXPERTS * (ROW_ALIGN - 1) + N_EXPERTS * (EXPERT_SLACK + blk - 1), blk)
    n_blocks = n_rows // blk
    n_used = (pends[-1] // blk).astype(i32).reshape(1)
    block_start = jnp.arange(n_blocks, dtype=i32) * blk
    block_exp = jnp.minimum(jnp.sum((pends[None, :] <= block_start[:, None]).astype(i32), axis=1),
                            N_EXPERTS - 1)

    xs = _disp_call(pends, zfrom, dinfo, lpos.astype(i32), h2p, n_rows, tm, blk)
    ys = _exp_call(block_exp, n_used, xs, exp_w[0], exp_w[1], blk)
    ys3 = ys.reshape(n_rows // COMB_CHUNK, COMB_CHUNK, d)
    return _comb_call(cinfo, ybase, mod, tok_major(cpos).astype(i32), tok_major(w_arr), g_final, ys3, tm)


def kernel(x_prompt, x_sample, c_prompt, c_sample, w_ada, b_ada, g_norm1, w_in, g_q_lat, w_uq, g_kv_lat, w_ukv, g_gmlp_v, w_spatial, b_spatial, g_out_attn, g_out_gmlp, w_out, g_norm2, w_router, b_router, w_gate_e, w_up_e, w_down_e, w_gate_s, w_up_s, w_down_s, g_final):
    prep = _prep_weights(w_ada[0], b_ada[0], g_norm1[0], w_in[0], g_q_lat[0], w_uq[0], g_kv_lat[0],
                         w_ukv[0], g_gmlp_v[0], w_spatial[0], b_spatial[0], g_out_attn[0],
                         g_out_gmlp[0], w_out[0], g_norm2[0], w_router[0], b_router[0], w_gate_e[0],
                         w_up_e[0], w_down_e[0], w_gate_s[0], w_up_s[0], w_down_s[0], g_final)
    return (_trunk(x_prompt, c_prompt, prep), _trunk(x_sample, c_sample, prep))
```

```python
import functools
import math

import jax
import jax.numpy as jnp
from jax import lax
from jax.experimental import pallas as pl
from jax.experimental.pallas import tpu as pltpu

F32 = jnp.float32
BF16 = jnp.bfloat16

D_MODEL = 1024
N_HEADS = 8
QK_NOPE = 64
QK_ROPE = 32
V_DIM = 64
Q_LORA = 256
KV_LORA = 128
GMLP_W = 512
CHUNK = 128
N_EXPERTS = 64
TOP_K = 8
N_GROUPS = 8
TOPK_GROUPS = 4
GROUP_SIZE = N_EXPERTS // N_GROUPS
EXPERT_DIM = 256
SHARED_DIM = 256
ROUTED_SCALE = 2.5
ROPE_THETA = 10000.0
N_MOD = 6
EPS = 1e-6

LANES = 128
HEAD_PAD = 128
HALF_ROPE = QK_ROPE // 2
VMEM_LIMIT = 52 * 1024 * 1024
ROW_ALIGN = 8
DISP_CHUNK = 48
COMB_CHUNK = 16
COMB_CHUNK_TILES = (3, 2, 1)
ATTN_CHUNKS_PER_STEP = 8
PERM_ROWS = 512
COMB_KTILE = 1024
EXPERT_SLACK = max(DISP_CHUNK, 2 * (COMB_CHUNK - 1))

SOFTMAX_SCALE = (QK_NOPE + QK_ROPE) ** -0.5
EXP2_SCALE = SOFTMAX_SCALE * math.log2(math.e)


def _rms(x, g):
    return x * lax.rsqrt(jnp.mean(x * x, axis=-1, keepdims=True) + EPS) * g


def _sigmoid(x):
    return 1.0 / (1.0 + jnp.exp(-x))


def _gelu_tanh(x):
    c = math.sqrt(2.0 / math.pi)
    return 0.5 * x * (1.0 + jnp.tanh(c * (x + 0.044715 * (x * x * x))))


def _dot(a, b):
    return jnp.dot(a, b, preferred_element_type=F32)


def _dot_nt(a, b):
    return lax.dot_general(a, b, (((1,), (1,)), ((), ())), preferred_element_type=F32)


def _pack_bf16_pairs(x):
    c = x.shape[1] // 2
    lo = pltpu.bitcast(x[:, :c].astype(BF16).astype(F32), jnp.uint32)
    hi = pltpu.bitcast(x[:, c:].astype(BF16).astype(F32), jnp.uint32)
    return (hi & jnp.uint32(0xFFFF0000)) | (lo >> 16)


def _unpack_bf16_pairs(w):
    lo = pltpu.bitcast(w << 16, F32)
    hi = pltpu.bitcast(w & jnp.uint32(0xFFFF0000), F32)
    return lo, hi


def _mod_kernel(c_ref, w_ref, b_ref, o_ref):
    c = c_ref[...]
    a = (c * _sigmoid(c)).astype(BF16)
    o_ref[...] = _dot(a, w_ref[...]) + b_ref[...]


def _modulation(c, w_ada_bf, b_ada):
    b = c.shape[0]
    bp = max(16, -(-b // 16) * 16)
    cp = jnp.pad(c, ((0, bp - b), (0, 0)))
    n = w_ada_bf.shape[1]
    tn = D_MODEL
    out = pl.pallas_call(
        _mod_kernel,
        name="mod",
        grid=(n // tn,),
        in_specs=[
            pl.BlockSpec((bp, D_MODEL), lambda j: (0, 0)),
            pl.BlockSpec((D_MODEL, tn), lambda j: (0, j)),
            pl.BlockSpec((1, tn), lambda j: (0, j)),
        ],
        out_specs=pl.BlockSpec((bp, tn), lambda j: (0, j)),
        out_shape=jax.ShapeDtypeStruct((bp, n), F32),
        compiler_params=pltpu.CompilerParams(dimension_semantics=("arbitrary",)),
    )(cp, w_ada_bf, b_ada.reshape(1, n))
    return out[:b].reshape(b, N_MOD, D_MODEL)


def _rope(xh, c, s1, s2):
    return (xh * c + pltpu.roll(xh, LANES - HALF_ROPE, axis=1) * s1
            + pltpu.roll(xh, HALF_ROPE, axis=1) * s2)


def _pre_kernel(x_ref, mod_ref, cos_ref, s1_ref, s2_ref, g1_ref, win_ref, gq_ref, wuq_ref,
                gkv_ref, wuk_ref, wuv_ref, vone_ref, ggv_ref, ws_ref, bs_ref, ggo_ref,
                q_ref, k_ref, v_ref, gm_ref, mix_ref):
    ts = x_ref.shape[1]
    x = x_ref[0]
    mod = mod_ref[0]
    h = _rms(x, g1_ref[...]) * (1.0 + mod[1:2]) + mod[0:1]
    z = _dot(h.astype(BF16), win_ref[...])
    o_kv = Q_LORA
    o_gu = o_kv + KV_LORA
    o_gv = o_gu + GMLP_W
    o_kr = o_gv + GMLP_W
    q_lat = z[:, :o_kv]
    kv_lat = z[:, o_kv:o_gu]
    g_u = z[:, o_gu:o_gv]
    g_v = z[:, o_gv:o_kr]
    kr = z[:, o_kr:o_kr + LANES]

    cos = cos_ref[...]
    s1 = s1_ref[...]
    s2 = s2_ref[...]

    qn = _rms(q_lat, gq_ref[...]).astype(BF16)
    q = _dot(qn, wuq_ref[...])
    kn = _rms(kv_lat, gkv_ref[...]).astype(BF16)
    kf = _dot(kn, wuk_ref[...])
    v_ref[0] = (_dot(kn, wuv_ref[...]) + vone_ref[...]).astype(BF16)
    krr = _rope(kr, cos, s1, s2)
    for hd in range(N_HEADS):
        sl = slice(hd * HEAD_PAD, (hd + 1) * HEAD_PAD)
        q_ref[0, :, sl] = (_rope(q[:, sl], cos, s1, s2) * EXP2_SCALE).astype(BF16)
        k_ref[0, sl, :] = jnp.transpose(kf[:, sl] + krr).astype(BF16)

    u = _gelu_tanh(g_u)
    vn = _rms(_gelu_tanh(g_v), ggv_ref[...]).astype(BF16)
    lane = lax.broadcasted_iota(jnp.int32, (CHUNK, LANES), 1)
    left = lane < (LANES // 2)
    zero = jnp.zeros((CHUNK, LANES), BF16)
    for n in range(ts // CHUNK):
        rs = slice(n * CHUNK, (n + 1) * CHUNK)
        for p in range(GMLP_W // LANES):
            cs = slice(p * LANES, (p + 1) * LANES)
            vp = vn[rs, cs]
            rhs = jnp.concatenate([jnp.where(left, vp, zero), jnp.where(left, zero, vp)], axis=0)
            mix_ref[rs, cs] = _dot(ws_ref[p], rhs) + bs_ref[:, cs]
    gm = u * mix_ref[...]
    gm_ref[0] = _rms(gm, ggo_ref[...]).astype(BF16)


def _pre_call(x, mod, tabs, wts, ts):
    b, s, d = x.shape
    cos, s1, s2 = tabs
    (g1, win, gq, wuq, gkv, wuk, wuv, vone, ggv, wsp, bsp, ggo) = wts
    full = lambda a: pl.BlockSpec(a.shape, lambda bi, i: (0,) * a.ndim)
    tab = pl.BlockSpec((ts, LANES), lambda bi, i: (i, 0))
    out_shapes = (
        jax.ShapeDtypeStruct((b, s, N_HEADS * HEAD_PAD), BF16),
        jax.ShapeDtypeStruct((b, N_HEADS * HEAD_PAD, s), BF16),
        jax.ShapeDtypeStruct((b, s, N_HEADS * HEAD_PAD), BF16),
        jax.ShapeDtypeStruct((b, s, GMLP_W), BF16),
    )
    tok = lambda w: pl.BlockSpec((1, ts, w), lambda bi, i: (bi, i, 0))
    tok_t = pl.BlockSpec((1, N_HEADS * HEAD_PAD, ts), lambda bi, i: (bi, 0, i))
    return pl.pallas_call(
        _pre_kernel,
        name="pre",
        grid=(b, s // ts),
        in_specs=[tok(d), pl.BlockSpec((1, N_MOD, d), lambda bi, i: (bi, 0, 0)), tab, tab, tab,
                  full(g1), full(win), full(gq), full(wuq), full(gkv), full(wuk), full(wuv), full(vone),
                  full(ggv), full(wsp), full(bsp), full(ggo)],
        out_specs=(tok(N_HEADS * HEAD_PAD), tok_t, tok(N_HEADS * HEAD_PAD), tok(GMLP_W)),
        out_shape=out_shapes,
        scratch_shapes=[pltpu.VMEM((ts, GMLP_W), F32)],
        compiler_params=pltpu.CompilerParams(dimension_semantics=("arbitrary", "arbitrary"),
                                             vmem_limit_bytes=VMEM_LIMIT),
    )(x, mod, cos, s1, s2, g1, win, gq, wuq, gkv, wuk, wuv, vone, ggv, wsp, bsp, ggo)


def _attn_kernel(q_ref, k_ref, v_ref, o_ref, *, tkc, cpb):
    tq = q_ref.shape[1]
    n_chunks = v_ref.shape[1] // tkc
    hps = q_ref.shape[2] // HEAD_PAD
    heads = [slice(hh * HEAD_PAD, (hh + 1) * HEAD_PAD) for hh in range(hps)]
    qs = [q_ref[0, :, hs] for hs in heads]

    def step(i, carry):
        carry = list(carry)
        for cc in range(cpb):
            off = pl.multiple_of((i * cpb + cc) * tkc, tkc)
            for hh, hs in enumerate(heads):
                m, acc = carry[2 * hh], carry[2 * hh + 1]
                kc = k_ref[0, hs, pl.ds(off, tkc)]
                vc = v_ref[0, pl.ds(off, tkc), hs]
                sc = _dot(qs[hh], kc)
                m_new = jnp.maximum(m, jnp.max(sc, axis=-1, keepdims=True))
                p = jnp.exp2(sc - m_new).astype(BF16)
                carry[2 * hh + 1] = jnp.exp2(m - m_new) * acc + _dot(p, vc)
                carry[2 * hh] = m_new
        return tuple(carry)

    n_steps = n_chunks // cpb
    init = (jnp.full((tq, 1), -jnp.inf, F32), jnp.zeros((tq, HEAD_PAD), F32)) * hps
    res = step(0, init) if n_steps == 1 else lax.fori_loop(0, n_steps, step, init)
    lane = lax.broadcasted_iota(jnp.int32, (tq, HEAD_PAD), 1)
    for pair in range(hps // 2):
        a0, a1 = res[4 * pair + 1], res[4 * pair + 3]
        o0 = a0 / a0[:, V_DIM:V_DIM + 1]
        o1 = a1 / a1[:, V_DIM:V_DIM + 1]
        o_ref[0, :, pair * HEAD_PAD:(pair + 1) * HEAD_PAD] = jnp.where(
            lane < V_DIM, o0, pltpu.roll(o1, V_DIM, axis=1)).astype(BF16)


def _attn_call(q, k, v, tq, tkc, hps):
    b, s, _ = q.shape
    kv_bytes = 2 * s * hps * HEAD_PAD * 2
    mode = pl.Buffered(2 if 2 * kv_bytes <= VMEM_LIMIT // 3 else 1)
    k_res = pl.BlockSpec((1, hps * HEAD_PAD, s), lambda bi, h, i: (bi, h, 0), pipeline_mode=mode)
    v_res = pl.BlockSpec((1, s, hps * HEAD_PAD), lambda bi, h, i: (bi, 0, h), pipeline_mode=mode)
    return pl.pallas_call(
        functools.partial(_attn_kernel, tkc=tkc, cpb=math.gcd(s // tkc, ATTN_CHUNKS_PER_STEP)),
        name="attn",
        grid=(b, N_HEADS // hps, s // tq),
        in_specs=[pl.BlockSpec((1, tq, hps * HEAD_PAD), lambda bi, h, i: (bi, i, h)), k_res, v_res],
        out_specs=pl.BlockSpec((1, tq, hps * V_DIM), lambda bi, h, i: (bi, i, h)),
        out_shape=jax.ShapeDtypeStruct((b, s, N_HEADS * V_DIM), BF16),
        compiler_params=pltpu.CompilerParams(
            dimension_semantics=("arbitrary", "arbitrary", "arbitrary"),
            vmem_limit_bytes=VMEM_LIMIT),
    )(q, k, v)


def _post_kernel(attn_ref, gm_ref, x_ref, mod_ref, goa_ref, woa_ref, wog_ref, g2_ref, wr_ref,
                 br_ref, wgus_ref, wds_ref,
                 ybase_ref, h2p_ref, e_ref, rank_ref, w_ref, cnt_ref):
    ts = x_ref.shape[1]
    mod = mod_ref[0]
    ga1, sh2, sc2, ga2 = mod[2:3], mod[3:4], mod[4:5], mod[5:6]
    an = _rms(attn_ref[0].astype(F32), goa_ref[...]).astype(BF16)
    y = _dot(an, woa_ref[...]) + _dot(gm_ref[0], wog_ref[...])
    x1 = x_ref[0] + ga1 * y
    h2 = _rms(x1, g2_ref[...]) * (1.0 + sc2) + sh2
    h2b = h2.astype(BF16)
    h2p_ref[...] = _pack_bf16_pairs(h2)

    gu = _dot(h2b, wgus_ref[...])
    g, u = gu[:, :SHARED_DIM], gu[:, SHARED_DIM:]
    a = (g * _sigmoid(g) * u).astype(BF16)
    ybase_ref[0] = x1 + ga2 * _dot(a, wds_ref[...])

    logits = _dot_nt(wr_ref[...], h2b)
    scores = _sigmoid(logits)
    biased = scores + br_ref[...]
    ninf = jnp.float32(-jnp.inf)
    bj = [biased[j * N_GROUPS:(j + 1) * N_GROUPS] for j in range(GROUP_SIZE)]
    sj = [scores[j * N_GROUPS:(j + 1) * N_GROUPS] for j in range(GROUP_SIZE)]
    m1 = bj[0]
    for j in range(1, GROUP_SIZE):
        m1 = jnp.maximum(m1, bj[j])
    found = jnp.zeros_like(m1)
    m2 = jnp.full_like(m1, ninf)
    for j in range(GROUP_SIZE):
        eq = jnp.where(bj[j] == m1, 1.0, 0.0)
        is_first = eq * (1.0 - found)
        found = jnp.maximum(found, eq)
        m2 = jnp.maximum(m2, jnp.where(is_first > 0.0, ninf, bj[j]))
    gs = m1 + m2
    gidx = lax.broadcasted_iota(jnp.int32, gs.shape, 0)
    grank = jnp.zeros_like(gs)
    for kk in range(1, N_GROUPS):
        r = pltpu.roll(gs, kk, axis=0)
        grank = grank + jnp.where(gidx >= kk, jnp.where(r >= gs, 1.0, 0.0), jnp.where(r > gs, 1.0, 0.0))
    gsel = grank < float(TOPK_GROUPS)
    masked = [jnp.where(gsel, bj[j], ninf) for j in range(GROUP_SIZE)]
    eidx = [gidx * GROUP_SIZE + j for j in range(GROUP_SIZE)]

    selm = [jnp.zeros_like(gs) for _ in range(GROUP_SIZE)]
    e_sel = []
    for _k in range(TOP_K):
        m = masked[0]
        for j in range(1, GROUP_SIZE):
            m = jnp.maximum(m, masked[j])
        m = jnp.max(m, axis=0, keepdims=True)
        cand = jnp.where(masked[0] == m, eidx[0], N_EXPERTS)
        for j in range(1, GROUP_SIZE):
            cand = jnp.minimum(cand, jnp.where(masked[j] == m, eidx[j], N_EXPERTS))
        emin = jnp.min(cand, axis=0, keepdims=True)
        e_sel.append(emin)
        for j in range(GROUP_SIZE):
            hit = eidx[j] == emin
            selm[j] = jnp.where(hit, 1.0, selm[j])
            masked[j] = jnp.where(hit, ninf, masked[j])

    wsel = [selm[j] * sj[j] for j in range(GROUP_SIZE)]
    tot = wsel[0]
    for j in range(1, GROUP_SIZE):
        tot = tot + wsel[j]
    tot = jnp.sum(tot, axis=0, keepdims=True)
    wn = [wsel[j] / tot * ROUTED_SCALE for j in range(GROUP_SIZE)]

    tm = e_ref.shape[2]
    sel = jnp.concatenate(selm, axis=0)
    tr = lax.broadcasted_iota(jnp.int32, (ts, ts), 0)
    tc = lax.broadcasted_iota(jnp.int32, (ts, ts), 1)
    sh = tm.bit_length() - 1
    same_tile = lax.shift_right_logical(tr, sh) == lax.shift_right_logical(tc, sh)
    upper = jnp.where(tr < tc, jnp.where(same_tile, 1.0, 0.0), 0.0).astype(BF16)
    rank_full = _dot(sel.astype(BF16), upper)
    rj = [rank_full[j * N_GROUPS:(j + 1) * N_GROUPS] for j in range(GROUP_SIZE)]
    subs = [slice(i * tm, (i + 1) * tm) for i in range(ts // tm)]
    for i, sub in enumerate(subs):
        cnt_ref[i] = jnp.broadcast_to(jnp.sum(sel[:, sub], axis=1, keepdims=True), (N_EXPERTS, LANES))

    for k in range(TOP_K):
        rk = jnp.zeros_like(gs)
        wk = jnp.zeros_like(gs)
        for j in range(GROUP_SIZE):
            hit = eidx[j] == e_sel[k]
            rk = rk + jnp.where(hit, rj[j], 0.0)
            wk = wk + jnp.where(hit, wn[j], 0.0)
        rk = jnp.sum(rk, axis=0, keepdims=True).astype(jnp.int32)
        wk = jnp.sum(wk, axis=0, keepdims=True)
        for i, sub in enumerate(subs):
            e_ref[i, k:k + 1, :] = e_sel[k][:, sub]
            rank_ref[i, k:k + 1, :] = rk[:, sub]
            w_ref[i, k:k + 1, :] = wk[:, sub]


def _post_call(attn, gm, x, mod, wts, ts, tm):
    b, s, d = x.shape
    nt = s // ts
    nsub = ts // tm
    ntm = b * nt * nsub
    (goa, woa, wog, g2, wr, br, wgus, wds) = wts
    full = lambda a: pl.BlockSpec(a.shape, lambda bi, i: (0,) * a.ndim)
    tok = lambda w: pl.BlockSpec((1, ts, w), lambda bi, i: (bi, i, 0))
    rout = pl.BlockSpec((nsub, TOP_K, tm), lambda bi, i: (bi * nt + i, 0, 0))
    out_shapes = (
        jax.ShapeDtypeStruct((b, s, d), F32),
        jax.ShapeDtypeStruct((b * s, d // 2), jnp.uint32),
        jax.ShapeDtypeStruct((ntm, TOP_K, tm), jnp.int32),
        jax.ShapeDtypeStruct((ntm, TOP_K, tm), jnp.int32),
        jax.ShapeDtypeStruct((ntm, TOP_K, tm), F32),
        jax.ShapeDtypeStruct((ntm, N_EXPERTS, LANES), F32),
    )
    return pl.pallas_call(
        _post_kernel,
        name="post",
        grid=(b, nt),
        in_specs=[tok(N_HEADS * V_DIM), tok(GMLP_W), tok(d),
                  pl.BlockSpec((1, N_MOD, d), lambda bi, i: (bi, 0, 0)),
                  full(goa), full(woa), full(wog), full(g2), full(wr), full(br), full(wgus), full(wds)],
        out_specs=(tok(d), pl.BlockSpec((ts, d // 2), lambda bi, i: (bi * nt + i, 0)),
                   rout, rout, rout,
                   pl.BlockSpec((nsub, N_EXPERTS, LANES), lambda bi, i: (bi * nt + i, 0, 0))),
        out_shape=out_shapes,
        compiler_params=pltpu.CompilerParams(dimension_semantics=("arbitrary", "arbitrary"),
                                             vmem_limit_bytes=VMEM_LIMIT),
    )(attn, gm, x, mod, goa, woa, wog, g2, wr, br, wgus, wds)


def _disp_kernel(pends_ref, zfrom_ref, info_ref, lpos_ref, h_ref, xs_ref, sbuf, zbuf, nprev, sems, zsem):
    td = h_ref.shape[0]
    blk = zbuf.shape[0]
    step = pl.program_id(0)
    slot = lax.rem(step, 2)

    @pl.when(step == 0)
    def _():
        nprev[0] = 0
        zbuf[...] = jnp.zeros_like(zbuf)

        def tail_copy(e, j):
            start = pl.multiple_of(zfrom_ref[e] + j * blk, blk)
            return pltpu.make_async_copy(zbuf, xs_ref.at[pl.ds(start, blk), :], zsem)

        def n_tail(e):
            return (pends_ref[e] - zfrom_ref[e]) // blk

        def zstart(e, c):
            def one(j, c2):
                tail_copy(e, j).start()
                return c2
            return lax.fori_loop(0, n_tail(e), one, c)

        def zwait(e, c):
            def one(j, c2):
                tail_copy(e, j).wait()
                return c2
            return lax.fori_loop(0, n_tail(e), one, c)

        lax.fori_loop(0, N_EXPERTS, zstart, 0)
        lax.fori_loop(0, N_EXPERTS, zwait, 0)

    lo, hi = _unpack_bf16_pairs(h_ref[...])
    lo = lo.astype(BF16)
    hi = hi.astype(BF16)
    lpos = lpos_ref[0]
    lpos_b = [jnp.broadcast_to(lpos[k:k + 1, :], (PERM_ROWS, td)).astype(jnp.int16) for k in range(TOP_K)]
    row0 = lax.broadcasted_iota(jnp.int32, (PERM_ROWS, td), 0)
    one = jnp.ones((PERM_ROWS, td), BF16)

    def build(rb, c):
        r0 = pl.multiple_of(rb * PERM_ROWS, PERM_ROWS)
        riota = (row0 + r0).astype(jnp.int16)
        pb = jnp.zeros((PERM_ROWS, td), BF16)
        for k in range(TOP_K):
            pb = jnp.where(lpos_b[k] == riota, one, pb)
        xlo = pltpu.bitcast(_dot(pb, lo), jnp.uint32)
        xhi = pltpu.bitcast(_dot(pb, hi), jnp.uint32)
        sbuf[slot, pl.ds(r0, PERM_ROWS), :] = (xhi & jnp.uint32(0xFFFF0000)) | (xlo >> 16)
        return c

    lax.fori_loop(0, info_ref[0, 2, 1], build, 0)

    def chunk_copy(src, dst, sl):
        return pltpu.make_async_copy(
            sbuf.at[sl, pl.ds(pl.multiple_of(src, ROW_ALIGN), DISP_CHUNK), :],
            xs_ref.at[pl.ds(pl.multiple_of(dst, ROW_ALIGN), DISP_CHUNK), :], sems.at[sl])

    def drain(n, sl):
        def one(c, carry):
            chunk_copy(0, 0, sl).wait()
            return carry
        lax.fori_loop(0, n, one, 0)

    def issue(c, carry):
        chunk_copy(info_ref[0, 0, c], info_ref[0, 1, c], slot).start()
        return carry

    drain(nprev[0], 1 - slot)
    n_chunks = info_ref[0, 2, 0]
    lax.fori_loop(0, n_chunks, issue, 0)
    nprev[0] = n_chunks

    @pl.when(step == pl.num_programs(0) - 1)
    def _():
        drain(n_chunks, slot)


def _sorted_rows(td):
    need = TOP_K * td + N_EXPERTS * (ROW_ALIGN - 1) + DISP_CHUNK
    return -(-need // PERM_ROWS) * PERM_ROWS


def _disp_call(pends, zfrom, info, lpos, h2p, n_rows, td, blk):
    t, w = h2p.shape
    nt = t // td
    return pl.pallas_call(
        _disp_kernel,
        name="disp",
        grid_spec=pltpu.PrefetchScalarGridSpec(
            num_scalar_prefetch=2,
            grid=(nt,),
            in_specs=[pl.BlockSpec((1,) + info.shape[1:], lambda i, pe, pa: (i, 0, 0), memory_space=pltpu.SMEM),
                      pl.BlockSpec((1, TOP_K, td), lambda i, pe, pa: (i, 0, 0)),
                      pl.BlockSpec((td, w), lambda i, pe, pa: (i, 0))],
            out_specs=pl.BlockSpec(memory_space=pl.ANY),
            scratch_shapes=[pltpu.VMEM((2, _sorted_rows(td), w), jnp.uint32),
                            pltpu.VMEM((blk, w), jnp.uint32),
                            pltpu.SMEM((1,), jnp.int32),
                            pltpu.SemaphoreType.DMA((2,)), pltpu.SemaphoreType.DMA(())],
        ),
        out_shape=jax.ShapeDtypeStruct((n_rows, w), jnp.uint32),
        compiler_params=pltpu.CompilerParams(dimension_semantics=("arbitrary",),
                                             vmem_limit_bytes=VMEM_LIMIT),
    )(pends, zfrom, info, lpos, h2p)


def _exp_kernel(bexp_ref, nused_ref, xs_ref, wg_ref, wu_ref, wd_ref, ys_ref):
    i = pl.program_id(0)

    @pl.when(i < nused_ref[0])
    def _():
        half = D_MODEL // 2
        lo, hi = _unpack_bf16_pairs(xs_ref[...])
        lo = lo.astype(BF16)
        hi = hi.astype(BF16)
        wg = wg_ref[0].astype(BF16)
        wu = wu_ref[0].astype(BF16)
        g = _dot(lo, wg[:half]) + _dot(hi, wg[half:])
        u = _dot(lo, wu[:half]) + _dot(hi, wu[half:])
        a = (g * _sigmoid(g) * u).astype(BF16)
        ys_ref[...] = _dot(a, wd_ref[0].astype(BF16)).astype(BF16)


def _exp_call(block_exp, n_used, xs, wg, wu, wd, blk):
    n_rows, w = xs.shape
    n_blocks = n_rows // blk

    def row_map(i, bexp, nused):
        return (jnp.minimum(i, nused[0] - 1), 0)

    def w_map(i, bexp, nused):
        return (bexp[jnp.minimum(i, nused[0] - 1)], 0, 0)

    return pl.pallas_call(
        _exp_kernel,
        name="exp",
        grid_spec=pltpu.PrefetchScalarGridSpec(
            num_scalar_prefetch=2,
            grid=(n_blocks,),
            in_specs=[pl.BlockSpec((blk, w), row_map),
                      pl.BlockSpec((1, D_MODEL, EXPERT_DIM), w_map),
                      pl.BlockSpec((1, D_MODEL, EXPERT_DIM), w_map),
                      pl.BlockSpec((1, EXPERT_DIM, D_MODEL), w_map)],
            out_specs=pl.BlockSpec((blk, D_MODEL), row_map),
        ),
        out_shape=jax.ShapeDtypeStruct((n_rows, D_MODEL), BF16),
        compiler_params=pltpu.CompilerParams(dimension_semantics=("arbitrary",),
                                             vmem_limit_bytes=VMEM_LIMIT),
    )(block_exp, n_used, xs, wg, wu, wd)


def _comb_kernel(info_ref, next_info_ref, ybase_ref, mod_ref, cpos_ref, w_ref, gf_ref, ys_ref, o_ref,
                 ybuf, acc, sems):
    tc = ybase_ref.shape[1]
    step = pl.program_id(0) * pl.num_programs(1) + pl.program_id(1)
    n_steps = pl.num_programs(0) * pl.num_programs(1)
    slot = lax.rem(step, 2)

    def chunk_copy(src, dst, sl, n_tiles):
        return pltpu.make_async_copy(ys_ref.at[pl.ds(src, n_tiles)], ybuf.at[sl, pl.ds(dst, n_tiles)],
                                     sems.at[sl])

    def fetch(iref, sl):
        for i, n_tiles in enumerate(COMB_CHUNK_TILES):
            def issue(c, carry):
                chunk_copy(iref[0, 2 * i, c], iref[0, 2 * i + 1, c], sl, n_tiles).start()
                return carry
            lax.fori_loop(0, iref[0, 2 * len(COMB_CHUNK_TILES), i], issue, 0)

    @pl.when(step == 0)
    def _():
        ybuf[...] = jnp.zeros_like(ybuf)
        fetch(info_ref, 0)

    @pl.when(step + 1 < n_steps)
    def _():
        fetch(next_info_ref, 1 - slot)

    for i, n_tiles in enumerate(COMB_CHUNK_TILES):
        def drain(c, carry):
            chunk_copy(0, 0, slot, n_tiles).wait()
            return carry
        lax.fori_loop(0, info_ref[0, 2 * len(COMB_CHUNK_TILES), i], drain, 0)

    cpos = cpos_ref[...]
    w = w_ref[...]
    cpos_b = [jnp.broadcast_to(cpos[:, k:k + 1], (tc, LANES)).astype(jnp.int16) for k in range(TOP_K)]
    w_b = [jnp.broadcast_to(w[:, k:k + 1], (tc, LANES)).astype(BF16) for k in range(TOP_K)]
    lane = lax.broadcasted_iota(jnp.int32, (tc, LANES), 1)
    acc[...] = jnp.zeros_like(acc)

    def slab(kt, c):
        k0 = pl.multiple_of(kt * COMB_KTILE, COMB_KTILE)
        cols = []
        for j in range(COMB_KTILE // LANES):
            col = (lane + (k0 + j * LANES)).astype(jnp.int16)
            wm = jnp.zeros((tc, LANES), BF16)
            for k in range(TOP_K):
                wm = jnp.where(cpos_b[k] == col, w_b[k], wm)
            cols.append(wm)
        rows = ybuf[slot, pl.ds(kt * (COMB_KTILE // COMB_CHUNK), COMB_KTILE // COMB_CHUNK)]
        acc[...] += _dot(jnp.concatenate(cols, axis=1), rows.reshape(COMB_KTILE, rows.shape[-1]))
        return c

    lax.fori_loop(0, info_ref[0, 2 * len(COMB_CHUNK_TILES), len(COMB_CHUNK_TILES)], slab, 0)
    ga2 = mod_ref[0][5:6]
    o_ref[0] = _rms(ybase_ref[0] + ga2 * acc[...], gf_ref[...])


def _gather_rows(tc):
    need = TOP_K * tc + N_EXPERTS * 2 * (COMB_CHUNK - 1)
    return -(-need // COMB_KTILE) * COMB_KTILE


def _comb_call(info, ybase, mod, cpos_tok, w_tok, g_final, ys, tc):
    b, s, d = ybase.shape
    nt = s // tc
    return pl.pallas_call(
        _comb_kernel,
        name="comb",
        grid=(b, nt),
        in_specs=[pl.BlockSpec((1,) + info.shape[1:], lambda bi, i: (bi * nt + i, 0, 0), memory_space=pltpu.SMEM),
                  pl.BlockSpec((1,) + info.shape[1:], lambda bi, i: (jnp.minimum(bi * nt + i + 1, b * nt - 1), 0, 0),
                               memory_space=pltpu.SMEM),
                  pl.BlockSpec((1, tc, d), lambda bi, i: (bi, i, 0)),
                  pl.BlockSpec((1, N_MOD, d), lambda bi, i: (bi, 0, 0)),
                  pl.BlockSpec((tc, TOP_K), lambda bi, i: (bi * nt + i, 0)),
                  pl.BlockSpec((tc, TOP_K), lambda bi, i: (bi * nt + i, 0)),
                  pl.BlockSpec((1, d), lambda bi, i: (0, 0)),
                  pl.BlockSpec(memory_space=pl.ANY)],
        out_specs=pl.BlockSpec((1, tc, d), lambda bi, i: (bi, i, 0)),
        out_shape=jax.ShapeDtypeStruct((b, s, d), F32),
        scratch_shapes=[pltpu.VMEM((2, _gather_rows(tc) // COMB_CHUNK, COMB_CHUNK, d), BF16),
                        pltpu.VMEM((tc, d), F32), pltpu.SemaphoreType.DMA((2,))],
        compiler_params=pltpu.CompilerParams(dimension_semantics=("arbitrary", "arbitrary"),
                                             vmem_limit_bytes=VMEM_LIMIT),
    )(info, info, ybase, mod, cpos_tok, w_tok, g_final, ys)


def _prep_weights(w_ada, b_ada, g_norm1, w_in, g_q_lat, w_uq, g_kv_lat, w_ukv, g_gmlp_v, w_spatial,
                  b_spatial, g_out_attn, g_out_gmlp, w_out, g_norm2, w_router, b_router, w_gate_e,
                  w_up_e, w_down_e, w_gate_s, w_up_s, w_down_s, g_final):
    row = lambda g: g.reshape(1, -1).astype(F32)
    o1 = Q_LORA
    o2 = o1 + KV_LORA
    o3 = o2 + QK_ROPE
    o4 = o3 + GMLP_W
    kr_cols = jnp.pad(w_in[:, o2:o3], ((0, 0), (QK_NOPE, LANES - QK_NOPE - QK_ROPE)))
    win = jnp.concatenate([w_in[:, :o2], w_in[:, o3:o4], w_in[:, o4:], kr_cols], axis=1).astype(BF16)
    qd = QK_NOPE + QK_ROPE
    wuq = jnp.pad(w_uq.reshape(Q_LORA, N_HEADS, qd), ((0, 0), (0, 0), (0, HEAD_PAD - qd)))
    wuq = wuq.reshape(Q_LORA, N_HEADS * HEAD_PAD).astype(BF16)
    wkv = w_ukv.reshape(KV_LORA, N_HEADS, QK_NOPE + V_DIM)
    wuk = jnp.pad(wkv[:, :, :QK_NOPE], ((0, 0), (0, 0), (0, HEAD_PAD - QK_NOPE)))
    wuk = wuk.reshape(KV_LORA, N_HEADS * HEAD_PAD).astype(BF16)
    wuv = jnp.pad(wkv[:, :, QK_NOPE:], ((0, 0), (0, 0), (0, HEAD_PAD - V_DIM)))
    wuv = wuv.reshape(KV_LORA, N_HEADS * HEAD_PAD).astype(BF16)
    vone = jnp.tile((jnp.arange(HEAD_PAD) == V_DIM).astype(F32), N_HEADS).reshape(1, -1)
    wsp = w_spatial.reshape(N_HEADS // 2, 2, CHUNK, CHUNK).transpose(0, 2, 1, 3)
    wsp = wsp.reshape(N_HEADS // 2, CHUNK, 2 * CHUNK).astype(BF16)
    bsp = jnp.repeat(jnp.transpose(b_spatial), GMLP_W // N_HEADS, axis=1).astype(F32)
    pre = (row(g_norm1), win, row(g_q_lat), wuq, row(g_kv_lat), wuk, wuv, vone, row(g_gmlp_v), wsp,
           bsp, row(g_out_gmlp))
    perm = (jnp.arange(N_GROUPS)[None, :] * GROUP_SIZE + jnp.arange(GROUP_SIZE)[:, None]).reshape(-1)
    wr = jnp.transpose(w_router)[perm].astype(BF16)
    br = b_router.astype(F32)[perm].reshape(N_EXPERTS, 1)
    mla_w = N_HEADS * V_DIM
    wgus = jnp.concatenate([w_gate_s, w_up_s], axis=1).astype(BF16)
    post = (row(g_out_attn), w_out[:mla_w].astype(BF16), w_out[mla_w:].astype(BF16), row(g_norm2),
            wr, br, wgus, w_down_s.astype(BF16))
    return w_ada.astype(BF16), b_ada, pre, post, (w_gate_e, w_up_e, w_down_e), row(g_final)


def _rope_tables(s):
    inv = 1.0 / (ROPE_THETA ** (jnp.arange(0, QK_ROPE, 2, dtype=F32) / QK_ROPE))
    ang = jnp.arange(s, dtype=F32)[:, None] * inv[None, :]
    cos, sin = jnp.cos(ang), jnp.sin(ang)
    z = lambda n: jnp.zeros((s, n), F32)
    tail = LANES - QK_NOPE - QK_ROPE
    c = jnp.concatenate([jnp.ones((s, QK_NOPE), F32), cos, cos, z(tail)], axis=1)
    s1 = jnp.concatenate([z(QK_NOPE), -sin, z(HALF_ROPE), z(tail)], axis=1)
    s2 = jnp.concatenate([z(QK_NOPE), z(HALF_ROPE), sin, z(tail)], axis=1)
    return c, s1, s2


def _tiles(s):
    ts = min(1024, s)
    tq = min(512, s)
    tkc = min(1024, s)
    blk = 1024
    tm = min(256, s)
    hps = 8 if s * 8 * HEAD_PAD * 2 * 2 <= VMEM_LIMIT // 4 else 2
    return ts, tq, tkc, blk, tm, hps


def _ceil_to(x, m):
    return (x + m - 1) // m * m


def _trunk(x, c, prep, tiles=None, rope=None):
    w_ada, b_ada, pre_w, post_w, exp_w, g_final = prep
    b, s, d = x.shape
    ts, tq, tkc, blk, tm, hps = tiles or _tiles(s)
    t = b * s
    nt = t // tm
    mod = _modulation(c, w_ada, b_ada)
    q, k, v, gm = _pre_call(x, mod, rope or _rope_tables(s), pre_w, ts)
    attn = _attn_call(q, k, v, tq, tkc, hps)
    ybase, h2p, e_arr, lrank, w_arr, cnt = _post_call(attn, gm, x, mod, post_w, min(ts, 512), tm)

    i32 = jnp.int32
    cnt = cnt[:, :, 0].astype(i32).reshape(nt, GROUP_SIZE, N_GROUPS).transpose(0, 2, 1).reshape(nt, N_EXPERTS)
    cnt8 = _ceil_to(cnt, ROW_ALIGN)
    base8 = jnp.cumsum(cnt8, axis=0) - cnt8
    total8 = jnp.sum(cnt8, axis=0)
    padded = _ceil_to(total8 + EXPERT_SLACK, blk)
    pends = jnp.cumsum(padded).astype(i32)
    zfrom = ((pends - padded + total8) // blk * blk).astype(i32)
    dstbase = (pends - padded)[None, :] + base8
    toff8 = jnp.cumsum(cnt8, axis=1) - cnt8
    eids = jnp.arange(N_EXPERTS, dtype=i32)

    def chunk_table(nch, chunk, n_max):
        cend = jnp.cumsum(nch, axis=1)
        cidx = jnp.arange(n_max, dtype=i32)
        e_of_c = jnp.minimum(jnp.sum((cend[:, None, :] <= cidx[None, :, None]).astype(i32), axis=-1),
                             N_EXPERTS - 1)
        pick = lambda tbl: jnp.sum(jnp.where(e_of_c[..., None] == eids, tbl[:, None, :], 0), axis=-1)
        rel = lambda first_row: pick(first_row - (cend - nch) * chunk) + cidx[None, :] * chunk
        return rel, cend[:, -1]

    n_dmax = N_EXPERTS + TOP_K * tm // DISP_CHUNK
    nch_d = (cnt + DISP_CHUNK - 1) // DISP_CHUNK
    rel_d, n_dch = chunk_table(nch_d, DISP_CHUNK, n_dmax)
    n_rb = (jnp.sum(cnt8, axis=1) + DISP_CHUNK + PERM_ROWS - 1) // PERM_ROWS
    tail = lambda a, b2, n: jnp.concatenate([a[:, None], b2[:, None], jnp.zeros((nt, n - 2), i32)], axis=1)
    dinfo = jnp.stack([rel_d(toff8), rel_d(dstbase), tail(n_dch, n_rb, n_dmax)], axis=1).astype(i32)

    shift = dstbase % COMB_CHUNK
    nch_c = jnp.where(cnt > 0, (cnt + shift + COMB_CHUNK - 1) // COMB_CHUNK, 0)
    boff = (jnp.cumsum(nch_c, axis=1) - nch_c) * COMB_CHUNK
    big = COMB_CHUNK_TILES[0]
    n_cmax = max(_gather_rows(tm) // COMB_CHUNK // big, N_EXPERTS)
    src_t = (dstbase - shift) // COMB_CHUNK
    dst_t = boff // COMB_CHUNK
    done = nch_c // big * big
    rows, totals = [], []
    for size in COMB_CHUNK_TILES:
        n_size = nch_c // big if size == big else (nch_c - done == size).astype(i32)
        first = 0 if size == big else done
        rel, total = chunk_table(n_size, size, n_cmax)
        rows += [rel(src_t + first), rel(dst_t + first)]
        totals.append(total[:, None])
    n_kt = (jnp.sum(nch_c, axis=1) * COMB_CHUNK + COMB_KTILE - 1) // COMB_KTILE
    counts = jnp.concatenate(totals + [n_kt[:, None], jnp.zeros((nt, n_cmax - len(totals) - 1), i32)], axis=1)
    cinfo = jnp.stack(rows + [counts], axis=1).astype(i32)

    onehot = e_arr[..., None] == eids
    lookup = lambda tbl: jnp.sum(jnp.where(onehot, tbl[:, None, None, :], 0), axis=-1)
    lpos = lookup(toff8) + lrank
    cpos = lookup(boff + shift) + lrank
    tok_major = lambda a: jnp.transpose(a, (0, 2, 1)).reshape(t, TOP_K)

    n_rows = _ceil_to(t * TOP_K + nt * N_EXPERTS * (ROW_ALIGN - 1) + N_EXPERTS * (EXPERT_SLACK + blk - 1), blk)
    n_blocks = n_rows // blk
    n_used = (pends[-1] // blk).astype(i32).reshape(1)
    block_start = jnp.arange(n_blocks, dtype=i32) * blk
    block_exp = jnp.minimum(jnp.sum((pends[None, :] <= block_start[:, None]).astype(i32), axis=1),
                            N_EXPERTS - 1)

    xs = _disp_call(pends, zfrom, dinfo, lpos.astype(i32), h2p, n_rows, tm, blk)
    ys = _exp_call(block_exp, n_used, xs, *exp_w, blk)
    ys3 = ys.reshape(n_rows // COMB_CHUNK, COMB_CHUNK, d)
    return _comb_call(cinfo, ybase, mod, tok_major(cpos).astype(i32), tok_major(w_arr), g_final, ys3, tm)


def kernel(x_prompt, x_sample, c_prompt, c_sample, w_ada, b_ada, g_norm1, w_in, g_q_lat, w_uq, g_kv_lat, w_ukv, g_gmlp_v, w_spatial, b_spatial, g_out_attn, g_out_gmlp, w_out, g_norm2, w_router, b_router, w_gate_e, w_up_e, w_down_e, w_gate_s, w_up_s, w_down_s, g_final):
    prep = _prep_weights(w_ada[0], b_ada[0], g_norm1[0], w_in[0], g_q_lat[0], w_uq[0], g_kv_lat[0],
                         w_ukv[0], g_gmlp_v[0], w_spatial[0], b_spatial[0], g_out_attn[0],
                         g_out_gmlp[0], w_out[0], g_norm2[0], w_router[0], b_router[0], w_gate_e[0],
                         w_up_e[0], w_down_e[0], w_gate_s[0], w_up_s[0], w_down_s[0], g_final)
    rope = _rope_tables(max(x_prompt.shape[1], x_sample.shape[1]))
    return (_trunk(x_prompt, c_prompt, prep, rope=rope), _trunk(x_sample, c_sample, prep, rope=rope))
```

```python
import functools
import math

import jax
import jax.numpy as jnp
from jax import lax
from jax.experimental import pallas as pl
from jax.experimental.pallas import tpu as pltpu

F32 = jnp.float32
BF16 = jnp.bfloat16

D_MODEL = 1024
N_HEADS = 8
QK_NOPE = 64
QK_ROPE = 32
V_DIM = 64
Q_LORA = 256
KV_LORA = 128
GMLP_W = 512
CHUNK = 128
N_EXPERTS = 64
TOP_K = 8
N_GROUPS = 8
TOPK_GROUPS = 4
GROUP_SIZE = N_EXPERTS // N_GROUPS
EXPERT_DIM = 256
SHARED_DIM = 256
ROUTED_SCALE = 2.5
ROPE_THETA = 10000.0
N_MOD = 6
EPS = 1e-6

LANES = 128
HEAD_PAD = 128
HALF_ROPE = QK_ROPE // 2
VMEM_LIMIT = 52 * 1024 * 1024
ROW_ALIGN = 8
DISP_CHUNK = 48
COMB_CHUNK = 16
COMB_CHUNK_TILES = (3, 2, 1)
ATTN_CHUNKS_PER_STEP = 8
PERM_ROWS = 512
COMB_KTILE = 1024
EXPERT_SLACK = max(DISP_CHUNK, 2 * (COMB_CHUNK - 1))

SOFTMAX_SCALE = (QK_NOPE + QK_ROPE) ** -0.5
EXP2_SCALE = SOFTMAX_SCALE * math.log2(math.e)


def _rms(x, g):
    return x * lax.rsqrt(jnp.mean(x * x, axis=-1, keepdims=True) + EPS) * g


def _sigmoid(x):
    return 1.0 / (1.0 + jnp.exp(-x))


def _gelu_tanh(x):
    c = math.sqrt(2.0 / math.pi)
    return 0.5 * x * (1.0 + jnp.tanh(c * (x + 0.044715 * (x * x * x))))


def _dot(a, b):
    return jnp.dot(a, b, preferred_element_type=F32)


def _dot_nt(a, b):
    return lax.dot_general(a, b, (((1,), (1,)), ((), ())), preferred_element_type=F32)


def _pack_bf16_pairs(x):
    c = x.shape[1] // 2
    lo = pltpu.bitcast(x[:, :c].astype(BF16).astype(F32), jnp.uint32)
    hi = pltpu.bitcast(x[:, c:].astype(BF16).astype(F32), jnp.uint32)
    return (hi & jnp.uint32(0xFFFF0000)) | (lo >> 16)


def _unpack_bf16_pairs(w):
    lo = pltpu.bitcast(w << 16, F32)
    hi = pltpu.bitcast(w & jnp.uint32(0xFFFF0000), F32)
    return lo, hi


def _mod_kernel(c_ref, w_ref, b_ref, o_ref):
    c = c_ref[...]
    a = (c * _sigmoid(c)).astype(BF16)
    o_ref[...] = _dot(a, w_ref[...]) + b_ref[...]


def _modulation(c, w_ada_bf, b_ada):
    b = c.shape[0]
    bp = max(16, -(-b // 16) * 16)
    cp = jnp.pad(c, ((0, bp - b), (0, 0)))
    n = w_ada_bf.shape[1]
    tn = D_MODEL
    out = pl.pallas_call(
        _mod_kernel,
        name="mod",
        grid=(n // tn,),
        in_specs=[
            pl.BlockSpec((bp, D_MODEL), lambda j: (0, 0)),
            pl.BlockSpec((D_MODEL, tn), lambda j: (0, j)),
            pl.BlockSpec((1, tn), lambda j: (0, j)),
        ],
        out_specs=pl.BlockSpec((bp, tn), lambda j: (0, j)),
        out_shape=jax.ShapeDtypeStruct((bp, n), F32),
        compiler_params=pltpu.CompilerParams(dimension_semantics=("arbitrary",)),
    )(cp, w_ada_bf, b_ada.reshape(1, n))
    return out[:b].reshape(b, N_MOD, D_MODEL)


def _rope(xh, c, s1, s2):
    return (xh * c + pltpu.roll(xh, LANES - HALF_ROPE, axis=1) * s1
            + pltpu.roll(xh, HALF_ROPE, axis=1) * s2)


def _pre_kernel(x_ref, mod_ref, cos_ref, s1_ref, s2_ref, g1_ref, win_ref, gq_ref, wuq_ref,
                gkv_ref, wuk_ref, wuv_ref, vone_ref, ggv_ref, ws_ref, bs_ref, ggo_ref,
                q_ref, k_ref, v_ref, gm_ref, mix_ref):
    ts = x_ref.shape[1]
    x = x_ref[0]
    mod = mod_ref[0]
    h = _rms(x, g1_ref[...]) * (1.0 + mod[1:2]) + mod[0:1]
    z = _dot(h.astype(BF16), win_ref[...])
    o_kv = Q_LORA
    o_gu = o_kv + KV_LORA
    o_gv = o_gu + GMLP_W
    o_kr = o_gv + GMLP_W
    q_lat = z[:, :o_kv]
    kv_lat = z[:, o_kv:o_gu]
    g_u = z[:, o_gu:o_gv]
    g_v = z[:, o_gv:o_kr]
    kr = z[:, o_kr:o_kr + LANES]

    cos = cos_ref[...]
    s1 = s1_ref[...]
    s2 = s2_ref[...]

    qn = _rms(q_lat, gq_ref[...]).astype(BF16)
    q = _dot(qn, wuq_ref[...])
    kn = _rms(kv_lat, gkv_ref[...]).astype(BF16)
    kf = _dot(kn, wuk_ref[...])
    v_ref[0] = (_dot(kn, wuv_ref[...]) + vone_ref[...]).astype(BF16)
    krr = _rope(kr, cos, s1, s2)
    for hd in range(N_HEADS):
        sl = slice(hd * HEAD_PAD, (hd + 1) * HEAD_PAD)
        q_ref[0, :, sl] = (_rope(q[:, sl], cos, s1, s2) * EXP2_SCALE).astype(BF16)
        k_ref[0, sl, :] = jnp.transpose(kf[:, sl] + krr).astype(BF16)

    u = _gelu_tanh(g_u)
    vn = _rms(_gelu_tanh(g_v), ggv_ref[...]).astype(BF16)
    lane = lax.broadcasted_iota(jnp.int32, (CHUNK, LANES), 1)
    left = lane < (LANES // 2)
    zero = jnp.zeros((CHUNK, LANES), BF16)
    for n in range(ts // CHUNK):
        rs = slice(n * CHUNK, (n + 1) * CHUNK)
        for p in range(GMLP_W // LANES):
            cs = slice(p * LANES, (p + 1) * LANES)
            vp = vn[rs, cs]
            rhs = jnp.concatenate([jnp.where(left, vp, zero), jnp.where(left, zero, vp)], axis=0)
            mix_ref[rs, cs] = _dot(ws_ref[p], rhs) + bs_ref[:, cs]
    gm = u * mix_ref[...]
    gm_ref[0] = _rms(gm, ggo_ref[...]).astype(BF16)


def _pre_call(x, mod, tabs, wts, ts):
    b, s, d = x.shape
    cos, s1, s2 = tabs
    (g1, win, gq, wuq, gkv, wuk, wuv, vone, ggv, wsp, bsp, ggo) = wts
    full = lambda a: pl.BlockSpec(a.shape, lambda bi, i: (0,) * a.ndim)
    tab = pl.BlockSpec((ts, LANES), lambda bi, i: (i, 0))
    out_shapes = (
        jax.ShapeDtypeStruct((b, s, N_HEADS * HEAD_PAD), BF16),
        jax.ShapeDtypeStruct((b, N_HEADS * HEAD_PAD, s), BF16),
        jax.ShapeDtypeStruct((b, s, N_HEADS * HEAD_PAD), BF16),
        jax.ShapeDtypeStruct((b, s, GMLP_W), BF16),
    )
    tok = lambda w: pl.BlockSpec((1, ts, w), lambda bi, i: (bi, i, 0))
    tok_t = pl.BlockSpec((1, N_HEADS * HEAD_PAD, ts), lambda bi, i: (bi, 0, i))
    return pl.pallas_call(
        _pre_kernel,
        name="pre",
        grid=(b, s // ts),
        in_specs=[tok(d), pl.BlockSpec((1, N_MOD, d), lambda bi, i: (bi, 0, 0)), tab, tab, tab,
                  full(g1), full(win), full(gq), full(wuq), full(gkv), full(wuk), full(wuv), full(vone),
                  full(ggv), full(wsp), full(bsp), full(ggo)],
        out_specs=(tok(N_HEADS * HEAD_PAD), tok_t, tok(N_HEADS * HEAD_PAD), tok(GMLP_W)),
        out_shape=out_shapes,
        scratch_shapes=[pltpu.VMEM((ts, GMLP_W), F32)],
        compiler_params=pltpu.CompilerParams(dimension_semantics=("arbitrary", "arbitrary"),
                                             vmem_limit_bytes=VMEM_LIMIT),
    )(x, mod, cos, s1, s2, g1, win, gq, wuq, gkv, wuk, wuv, vone, ggv, wsp, bsp, ggo)


def _attn_kernel(q_ref, k_ref, v_ref, o_ref, *, tkc, cpb):
    tq = q_ref.shape[1]
    n_chunks = v_ref.shape[1] // tkc
    hps = q_ref.shape[2] // HEAD_PAD
    heads = [slice(hh * HEAD_PAD, (hh + 1) * HEAD_PAD) for hh in range(hps)]
    qs = [q_ref[0, :, hs] for hs in heads]

    def step(i, carry):
        carry = list(carry)
        for cc in range(cpb):
            off = pl.multiple_of((i * cpb + cc) * tkc, tkc)
            for hh, hs in enumerate(heads):
                m, acc = carry[2 * hh], carry[2 * hh + 1]
                kc = k_ref[0, hs, pl.ds(off, tkc)]
                vc = v_ref[0, pl.ds(off, tkc), hs]
                sc = _dot(qs[hh], kc)
                m_new = jnp.maximum(m, jnp.max(sc, axis=-1, keepdims=True))
                p = jnp.exp2(sc - m_new).astype(BF16)
                carry[2 * hh + 1] = jnp.exp2(m - m_new) * acc + _dot(p, vc)
                carry[2 * hh] = m_new
        return tuple(carry)

    n_steps = n_chunks // cpb
    init = (jnp.full((tq, 1), -jnp.inf, F32), jnp.zeros((tq, HEAD_PAD), F32)) * hps
    res = step(0, init) if n_steps == 1 else lax.fori_loop(0, n_steps, step, init)
    lane = lax.broadcasted_iota(jnp.int32, (tq, HEAD_PAD), 1)
    for pair in range(hps // 2):
        a0, a1 = res[4 * pair + 1], res[4 * pair + 3]
        o0 = a0 / a0[:, V_DIM:V_DIM + 1]
        o1 = a1 / a1[:, V_DIM:V_DIM + 1]
        o_ref[0, :, pair * HEAD_PAD:(pair + 1) * HEAD_PAD] = jnp.where(
            lane < V_DIM, o0, pltpu.roll(o1, V_DIM, axis=1)).astype(BF16)


def _attn_call(q, k, v, tq, tkc, hps):
    b, s, _ = q.shape
    kv_bytes = 2 * s * hps * HEAD_PAD * 2
    mode = pl.Buffered(2 if 2 * kv_bytes <= VMEM_LIMIT // 3 else 1)
    k_res = pl.BlockSpec((1, hps * HEAD_PAD, s), lambda bi, h, i: (bi, h, 0), pipeline_mode=mode)
    v_res = pl.BlockSpec((1, s, hps * HEAD_PAD), lambda bi, h, i: (bi, 0, h), pipeline_mode=mode)
    return pl.pallas_call(
        functools.partial(_attn_kernel, tkc=tkc, cpb=math.gcd(s // tkc, ATTN_CHUNKS_PER_STEP)),
        name="attn",
        grid=(b, N_HEADS // hps, s // tq),
        in_specs=[pl.BlockSpec((1, tq, hps * HEAD_PAD), lambda bi, h, i: (bi, i, h)), k_res, v_res],
        out_specs=pl.BlockSpec((1, tq, hps * V_DIM), lambda bi, h, i: (bi, i, h)),
        out_shape=jax.ShapeDtypeStruct((b, s, N_HEADS * V_DIM), BF16),
        compiler_params=pltpu.CompilerParams(
            dimension_semantics=("arbitrary", "arbitrary", "arbitrary"),
            vmem_limit_bytes=VMEM_LIMIT),
    )(q, k, v)


def _post_kernel(attn_ref, gm_ref, x_ref, mod_ref, goa_ref, woa_ref, wog_ref, g2_ref, wr_ref,
                 br_ref, wgus_ref, wds_ref,
                 ybase_ref, h2p_ref, e_ref, rank_ref, w_ref, cnt_ref):
    ts = x_ref.shape[1]
    mod = mod_ref[0]
    ga1, sh2, sc2, ga2 = mod[2:3], mod[3:4], mod[4:5], mod[5:6]
    an = _rms(attn_ref[0].astype(F32), goa_ref[...]).astype(BF16)
    y = _dot(an, woa_ref[...]) + _dot(gm_ref[0], wog_ref[...])
    x1 = x_ref[0] + ga1 * y
    h2 = _rms(x1, g2_ref[...]) * (1.0 + sc2) + sh2
    h2b = h2.astype(BF16)
    h2p_ref[...] = _pack_bf16_pairs(h2)

    gu = _dot(h2b, wgus_ref[...])
    g, u = gu[:, :SHARED_DIM], gu[:, SHARED_DIM:]
    a = (g * _sigmoid(g) * u).astype(BF16)
    ybase_ref[0] = x1 + ga2 * _dot(a, wds_ref[...])

    logits = _dot_nt(wr_ref[...], h2b)
    scores = _sigmoid(logits)
    biased = scores + br_ref[...]
    ninf = jnp.float32(-jnp.inf)
    bj = [biased[j * N_GROUPS:(j + 1) * N_GROUPS] for j in range(GROUP_SIZE)]
    sj = [scores[j * N_GROUPS:(j + 1) * N_GROUPS] for j in range(GROUP_SIZE)]
    m1 = bj[0]
    for j in range(1, GROUP_SIZE):
        m1 = jnp.maximum(m1, bj[j])
    found = jnp.zeros_like(m1)
    m2 = jnp.full_like(m1, ninf)
    for j in range(GROUP_SIZE):
        eq = jnp.where(bj[j] == m1, 1.0, 0.0)
        is_first = eq * (1.0 - found)
        found = jnp.maximum(found, eq)
        m2 = jnp.maximum(m2, jnp.where(is_first > 0.0, ninf, bj[j]))
    gs = m1 + m2
    gidx = lax.broadcasted_iota(jnp.int32, gs.shape, 0)
    grank = jnp.zeros_like(gs)
    for kk in range(1, N_GROUPS):
        r = pltpu.roll(gs, kk, axis=0)
        grank = grank + jnp.where(gidx >= kk, jnp.where(r >= gs, 1.0, 0.0), jnp.where(r > gs, 1.0, 0.0))
    gsel = grank < float(TOPK_GROUPS)
    masked = [jnp.where(gsel, bj[j], ninf) for j in range(GROUP_SIZE)]
    eidx = [gidx * GROUP_SIZE + j for j in range(GROUP_SIZE)]

    selm = [jnp.zeros_like(gs) for _ in range(GROUP_SIZE)]
    e_sel = []
    for _k in range(TOP_K):
        m = masked[0]
        for j in range(1, GROUP_SIZE):
            m = jnp.maximum(m, masked[j])
        m = jnp.max(m, axis=0, keepdims=True)
        cand = jnp.where(masked[0] == m, eidx[0], N_EXPERTS)
        for j in range(1, GROUP_SIZE):
            cand = jnp.minimum(cand, jnp.where(masked[j] == m, eidx[j], N_EXPERTS))
        emin = jnp.min(cand, axis=0, keepdims=True)
        e_sel.append(emin)
        for j in range(GROUP_SIZE):
            hit = eidx[j] == emin
            selm[j] = jnp.where(hit, 1.0, selm[j])
            masked[j] = jnp.where(hit, ninf, masked[j])

    wsel = [selm[j] * sj[j] for j in range(GROUP_SIZE)]
    tot = wsel[0]
    for j in range(1, GROUP_SIZE):
        tot = tot + wsel[j]
    tot = jnp.sum(tot, axis=0, keepdims=True)
    wn = [wsel[j] / tot * ROUTED_SCALE for j in range(GROUP_SIZE)]

    tm = e_ref.shape[2]
    sel = jnp.concatenate(selm, axis=0)
    tr = lax.broadcasted_iota(jnp.int32, (ts, ts), 0)
    tc = lax.broadcasted_iota(jnp.int32, (ts, ts), 1)
    sh = tm.bit_length() - 1
    same_tile = lax.shift_right_logical(tr, sh) == lax.shift_right_logical(tc, sh)
    upper = jnp.where(tr < tc, jnp.where(same_tile, 1.0, 0.0), 0.0).astype(BF16)
    rank_full = _dot(sel.astype(BF16), upper)
    rj = [rank_full[j * N_GROUPS:(j + 1) * N_GROUPS] for j in range(GROUP_SIZE)]
    subs = [slice(i * tm, (i + 1) * tm) for i in range(ts // tm)]
    for i, sub in enumerate(subs):
        cnt_ref[i] = jnp.broadcast_to(jnp.sum(sel[:, sub], axis=1, keepdims=True), (N_EXPERTS, LANES))

    for k in range(TOP_K):
        rk = jnp.zeros_like(gs)
        wk = jnp.zeros_like(gs)
        for j in range(GROUP_SIZE):
            hit = eidx[j] == e_sel[k]
            rk = rk + jnp.where(hit, rj[j], 0.0)
            wk = wk + jnp.where(hit, wn[j], 0.0)
        rk = jnp.sum(rk, axis=0, keepdims=True).astype(jnp.int32)
        wk = jnp.sum(wk, axis=0, keepdims=True)
        for i, sub in enumerate(subs):
            e_ref[i, k:k + 1, :] = e_sel[k][:, sub]
            rank_ref[i, k:k + 1, :] = rk[:, sub]
            w_ref[i, k:k + 1, :] = wk[:, sub]


def _post_call(attn, gm, x, mod, wts, ts, tm):
    b, s, d = x.shape
    nt = s // ts
    nsub = ts // tm
    ntm = b * nt * nsub
    (goa, woa, wog, g2, wr, br, wgus, wds) = wts
    full = lambda a: pl.BlockSpec(a.shape, lambda bi, i: (0,) * a.ndim)
    tok = lambda w: pl.BlockSpec((1, ts, w), lambda bi, i: (bi, i, 0))
    rout = pl.BlockSpec((nsub, TOP_K, tm), lambda bi, i: (bi * nt + i, 0, 0))
    out_shapes = (
        jax.ShapeDtypeStruct((b, s, d), F32),
        jax.ShapeDtypeStruct((b * s, d // 2), jnp.uint32),
        jax.ShapeDtypeStruct((ntm, TOP_K, tm), jnp.int32),
        jax.ShapeDtypeStruct((ntm, TOP_K, tm), jnp.int32),
        jax.ShapeDtypeStruct((ntm, TOP_K, tm), F32),
        jax.ShapeDtypeStruct((ntm, N_EXPERTS, LANES), F32),
    )
    return pl.pallas_call(
        _post_kernel,
        name="post",
        grid=(b, nt),
        in_specs=[tok(N_HEADS * V_DIM), tok(GMLP_W), tok(d),
                  pl.BlockSpec((1, N_MOD, d), lambda bi, i: (bi, 0, 0)),
                  full(goa), full(woa), full(wog), full(g2), full(wr), full(br), full(wgus), full(wds)],
        out_specs=(tok(d), pl.BlockSpec((ts, d // 2), lambda bi, i: (bi * nt + i, 0)),
                   rout, rout, rout,
                   pl.BlockSpec((nsub, N_EXPERTS, LANES), lambda bi, i: (bi * nt + i, 0, 0))),
        out_shape=out_shapes,
        compiler_params=pltpu.CompilerParams(dimension_semantics=("arbitrary", "arbitrary"),
                                             vmem_limit_bytes=VMEM_LIMIT),
    )(attn, gm, x, mod, goa, woa, wog, g2, wr, br, wgus, wds)


def _disp_kernel(pends_ref, zfrom_ref, info_ref, lpos_ref, h_ref, xs_ref, sbuf, zbuf, nprev, sems, zsem):
    td = h_ref.shape[0]
    blk = zbuf.shape[0]
    step = pl.program_id(0)
    slot = lax.rem(step, 2)

    @pl.when(step == 0)
    def _():
        nprev[0] = 0
        zbuf[...] = jnp.zeros_like(zbuf)

        def tail_copy(e, j):
            start = pl.multiple_of(zfrom_ref[e] + j * blk, blk)
            return pltpu.make_async_copy(zbuf, xs_ref.at[pl.ds(start, blk), :], zsem)

        def n_tail(e):
            return (pends_ref[e] - zfrom_ref[e]) // blk

        def zstart(e, c):
            def one(j, c2):
                tail_copy(e, j).start()
                return c2
            return lax.fori_loop(0, n_tail(e), one, c)

        def zwait(e, c):
            def one(j, c2):
                tail_copy(e, j).wait()
                return c2
            return lax.fori_loop(0, n_tail(e), one, c)

        lax.fori_loop(0, N_EXPERTS, zstart, 0)
        lax.fori_loop(0, N_EXPERTS, zwait, 0)

    lo, hi = _unpack_bf16_pairs(h_ref[...])
    lo = lo.astype(BF16)
    hi = hi.astype(BF16)
    lpos = lpos_ref[0]
    lpos_b = [jnp.broadcast_to(lpos[k:k + 1, :], (PERM_ROWS, td)).astype(jnp.int16) for k in range(TOP_K)]
    row0 = lax.broadcasted_iota(jnp.int32, (PERM_ROWS, td), 0)
    one = jnp.ones((PERM_ROWS, td), BF16)

    def build(rb, c):
        r0 = pl.multiple_of(rb * PERM_ROWS, PERM_ROWS)
        riota = (row0 + r0).astype(jnp.int16)
        pb = jnp.zeros((PERM_ROWS, td), BF16)
        for k in range(TOP_K):
            pb = jnp.where(lpos_b[k] == riota, one, pb)
        xlo = pltpu.bitcast(_dot(pb, lo), jnp.uint32)
        xhi = pltpu.bitcast(_dot(pb, hi), jnp.uint32)
        sbuf[slot, pl.ds(r0, PERM_ROWS), :] = (xhi & jnp.uint32(0xFFFF0000)) | (xlo >> 16)
        return c

    lax.fori_loop(0, info_ref[0, 2, 1], build, 0)

    def chunk_copy(src, dst, sl):
        return pltpu.make_async_copy(
            sbuf.at[sl, pl.ds(pl.multiple_of(src, ROW_ALIGN), DISP_CHUNK), :],
            xs_ref.at[pl.ds(pl.multiple_of(dst, ROW_ALIGN), DISP_CHUNK), :], sems.at[sl])

    def drain(n, sl):
        def one(c, carry):
            chunk_copy(0, 0, sl).wait()
            return carry
        lax.fori_loop(0, n, one, 0)

    def issue(c, carry):
        chunk_copy(info_ref[0, 0, c], info_ref[0, 1, c], slot).start()
        return carry

    drain(nprev[0], 1 - slot)
    n_chunks = info_ref[0, 2, 0]
    lax.fori_loop(0, n_chunks, issue, 0)
    nprev[0] = n_chunks

    @pl.when(step == pl.num_programs(0) - 1)
    def _():
        drain(n_chunks, slot)


def _sorted_rows(td):
    need = TOP_K * td + N_EXPERTS * (ROW_ALIGN - 1) + DISP_CHUNK
    return -(-need // PERM_ROWS) * PERM_ROWS


def _disp_call(pends, zfrom, info, lpos, h2p, n_rows, td, blk):
    t, w = h2p.shape
    nt = t // td
    return pl.pallas_call(
        _disp_kernel,
        name="disp",
        grid_spec=pltpu.PrefetchScalarGridSpec(
            num_scalar_prefetch=2,
            grid=(nt,),
            in_specs=[pl.BlockSpec((1,) + info.shape[1:], lambda i, pe, pa: (i, 0, 0), memory_space=pltpu.SMEM),
                      pl.BlockSpec((1, TOP_K, td), lambda i, pe, pa: (i, 0, 0)),
                      pl.BlockSpec((td, w), lambda i, pe, pa: (i, 0))],
            out_specs=pl.BlockSpec(memory_space=pl.ANY),
            scratch_shapes=[pltpu.VMEM((2, _sorted_rows(td), w), jnp.uint32),
                            pltpu.VMEM((blk, w), jnp.uint32),
                            pltpu.SMEM((1,), jnp.int32),
                            pltpu.SemaphoreType.DMA((2,)), pltpu.SemaphoreType.DMA(())],
        ),
        out_shape=jax.ShapeDtypeStruct((n_rows, w), jnp.uint32),
        compiler_params=pltpu.CompilerParams(dimension_semantics=("arbitrary",),
                                             vmem_limit_bytes=VMEM_LIMIT),
    )(pends, zfrom, info, lpos, h2p)


def _exp_kernel(bexp_ref, nused_ref, xs_ref, wg_ref, wu_ref, wd_ref, ys_ref):
    i = pl.program_id(0)

    @pl.when(i < nused_ref[0])
    def _():
        half = D_MODEL // 2
        lo, hi = _unpack_bf16_pairs(xs_ref[...])
        lo = lo.astype(BF16)
        hi = hi.astype(BF16)
        wg = wg_ref[0].astype(BF16)
        wu = wu_ref[0].astype(BF16)
        g = _dot(lo, wg[:half]) + _dot(hi, wg[half:])
        u = _dot(lo, wu[:half]) + _dot(hi, wu[half:])
        a = (g * _sigmoid(g) * u).astype(BF16)
        ys_ref[...] = _dot(a, wd_ref[0].astype(BF16)).astype(BF16)


def _exp_call(block_exp, n_used, xs, wg, wu, wd, blk):
    n_rows, w = xs.shape
    n_blocks = n_rows // blk

    def row_map(i, bexp, nused):
        return (jnp.minimum(i, nused[0] - 1), 0)

    def w_map(i, bexp, nused):
        return (bexp[jnp.minimum(i, nused[0] - 1)], 0, 0)

    return pl.pallas_call(
        _exp_kernel,
        name="exp",
        grid_spec=pltpu.PrefetchScalarGridSpec(
            num_scalar_prefetch=2,
            grid=(n_blocks,),
            in_specs=[pl.BlockSpec((blk, w), row_map),
                      pl.BlockSpec((1, D_MODEL, EXPERT_DIM), w_map),
                      pl.BlockSpec((1, D_MODEL, EXPERT_DIM), w_map),
                      pl.BlockSpec((1, EXPERT_DIM, D_MODEL), w_map)],
            out_specs=pl.BlockSpec((blk, D_MODEL), row_map),
        ),
        out_shape=jax.ShapeDtypeStruct((n_rows, D_MODEL), BF16),
        compiler_params=pltpu.CompilerParams(dimension_semantics=("arbitrary",),
                                             vmem_limit_bytes=VMEM_LIMIT),
    )(block_exp, n_used, xs, wg, wu, wd)


def _comb_kernel(info_ref, next_info_ref, ybase_ref, mod_ref, cpos_ref, w_ref, gf_ref, ys_ref, o_ref,
                 ybuf, acc, sems):
    tc = ybase_ref.shape[1]
    step = pl.program_id(0) * pl.num_programs(1) + pl.program_id(1)
    n_steps = pl.num_programs(0) * pl.num_programs(1)
    slot = lax.rem(step, 2)

    def chunk_copy(src, dst, sl, n_tiles):
        return pltpu.make_async_copy(ys_ref.at[pl.ds(src, n_tiles)], ybuf.at[sl, pl.ds(dst, n_tiles)],
                                     sems.at[sl])

    def fetch(iref, sl):
        for i, n_tiles in enumerate(COMB_CHUNK_TILES):
            def issue(c, carry):
                chunk_copy(iref[0, 2 * i, c], iref[0, 2 * i + 1, c], sl, n_tiles).start()
                return carry
            lax.fori_loop(0, iref[0, 2 * len(COMB_CHUNK_TILES), i], issue, 0)

    @pl.when(step == 0)
    def _():
        ybuf[...] = jnp.zeros_like(ybuf)
        fetch(info_ref, 0)

    @pl.when(step + 1 < n_steps)
    def _():
        fetch(next_info_ref, 1 - slot)

    for i, n_tiles in enumerate(COMB_CHUNK_TILES):
        def drain(c, carry):
            chunk_copy(0, 0, slot, n_tiles).wait()
            return carry
        lax.fori_loop(0, info_ref[0, 2 * len(COMB_CHUNK_TILES), i], drain, 0)

    cpos = cpos_ref[...]
    w = w_ref[...]
    cpos_b = [jnp.broadcast_to(cpos[:, k:k + 1], (tc, LANES)).astype(jnp.int16) for k in range(TOP_K)]
    w_b = [jnp.broadcast_to(w[:, k:k + 1], (tc, LANES)).astype(BF16) for k in range(TOP_K)]
    lane = lax.broadcasted_iota(jnp.int32, (tc, LANES), 1)
    acc[...] = jnp.zeros_like(acc)

    def slab(kt, c):
        k0 = pl.multiple_of(kt * COMB_KTILE, COMB_KTILE)
        cols = []
        for j in range(COMB_KTILE // LANES):
            col = (lane + (k0 + j * LANES)).astype(jnp.int16)
            wm = jnp.zeros((tc, LANES), BF16)
            for k in range(TOP_K):
                wm = jnp.where(cpos_b[k] == col, w_b[k], wm)
            cols.append(wm)
        rows = ybuf[slot, pl.ds(kt * (COMB_KTILE // COMB_CHUNK), COMB_KTILE // COMB_CHUNK)]
        acc[...] += _dot(jnp.concatenate(cols, axis=1), rows.reshape(COMB_KTILE, rows.shape[-1]))
        return c

    lax.fori_loop(0, info_ref[0, 2 * len(COMB_CHUNK_TILES), len(COMB_CHUNK_TILES)], slab, 0)
    ga2 = mod_ref[0][5:6]
    o_ref[0] = _rms(ybase_ref[0] + ga2 * acc[...], gf_ref[...])


def _gather_rows(tc):
    need = TOP_K * tc + N_EXPERTS * 2 * (COMB_CHUNK - 1)
    return -(-need // COMB_KTILE) * COMB_KTILE


def _comb_call(info, ybase, mod, cpos_tok, w_tok, g_final, ys, tc):
    b, s, d = ybase.shape
    nt = s // tc
    return pl.pallas_call(
        _comb_kernel,
        name="comb",
        grid=(b, nt),
        in_specs=[pl.BlockSpec((1,) + info.shape[1:], lambda bi, i: (bi * nt + i, 0, 0), memory_space=pltpu.SMEM),
                  pl.BlockSpec((1,) + info.shape[1:], lambda bi, i: (jnp.minimum(bi * nt + i + 1, b * nt - 1), 0, 0),
                               memory_space=pltpu.SMEM),
                  pl.BlockSpec((1, tc, d), lambda bi, i: (bi, i, 0)),
                  pl.BlockSpec((1, N_MOD, d), lambda bi, i: (bi, 0, 0)),
                  pl.BlockSpec((tc, TOP_K), lambda bi, i: (bi * nt + i, 0)),
                  pl.BlockSpec((tc, TOP_K), lambda bi, i: (bi * nt + i, 0)),
                  pl.BlockSpec((1, d), lambda bi, i: (0, 0)),
                  pl.BlockSpec(memory_space=pl.ANY)],
        out_specs=pl.BlockSpec((1, tc, d), lambda bi, i: (bi, i, 0)),
        out_shape=jax.ShapeDtypeStruct((b, s, d), F32),
        scratch_shapes=[pltpu.VMEM((2, _gather_rows(tc) // COMB_CHUNK, COMB_CHUNK, d), BF16),
                        pltpu.VMEM((tc, d), F32), pltpu.SemaphoreType.DMA((2,))],
        compiler_params=pltpu.CompilerParams(dimension_semantics=("arbitrary", "arbitrary"),
                                             vmem_limit_bytes=VMEM_LIMIT),
    )(info, info, ybase, mod, cpos_tok, w_tok, g_final, ys)


def _prep_weights(w_ada, b_ada, g_norm1, w_in, g_q_lat, w_uq, g_kv_lat, w_ukv, g_gmlp_v, w_spatial,
                  b_spatial, g_out_attn, g_out_gmlp, w_out, g_norm2, w_router, b_router, w_gate_e,
                  w_up_e, w_down_e, w_gate_s, w_up_s, w_down_s, g_final):
    row = lambda g: g.reshape(1, -1).astype(F32)
    o1 = Q_LORA
    o2 = o1 + KV_LORA
    o3 = o2 + QK_ROPE
    o4 = o3 + GMLP_W
    kr_cols = jnp.pad(w_in[:, o2:o3], ((0, 0), (QK_NOPE, LANES - QK_NOPE - QK_ROPE)))
    win = jnp.concatenate([w_in[:, :o2], w_in[:, o3:o4], w_in[:, o4:], kr_cols], axis=1).astype(BF16)
    qd = QK_NOPE + QK_ROPE
    wuq = jnp.pad(w_uq.reshape(Q_LORA, N_HEADS, qd), ((0, 0), (0, 0), (0, HEAD_PAD - qd)))
    wuq = wuq.reshape(Q_LORA, N_HEADS * HEAD_PAD).astype(BF16)
    wkv = w_ukv.reshape(KV_LORA, N_HEADS, QK_NOPE + V_DIM)
    wuk = jnp.pad(wkv[:, :, :QK_NOPE], ((0, 0), (0, 0), (0, HEAD_PAD - QK_NOPE)))
    wuk = wuk.reshape(KV_LORA, N_HEADS * HEAD_PAD).astype(BF16)
    wuv = jnp.pad(wkv[:, :, QK_NOPE:], ((0, 0), (0, 0), (0, HEAD_PAD - V_DIM)))
    wuv = wuv.reshape(KV_LORA, N_HEADS * HEAD_PAD).astype(BF16)
    vone = jnp.tile((jnp.arange(HEAD_PAD) == V_DIM).astype(F32), N_HEADS).reshape(1, -1)
    wsp = w_spatial.reshape(N_HEADS // 2, 2, CHUNK, CHUNK).transpose(0, 2, 1, 3)
    wsp = wsp.reshape(N_HEADS // 2, CHUNK, 2 * CHUNK).astype(BF16)
    bsp = jnp.repeat(jnp.transpose(b_spatial), GMLP_W // N_HEADS, axis=1).astype(F32)
    pre = (row(g_norm1), win, row(g_q_lat), wuq, row(g_kv_lat), wuk, wuv, vone, row(g_gmlp_v), wsp,
           bsp, row(g_out_gmlp))
    perm = (jnp.arange(N_GROUPS)[None, :] * GROUP_SIZE + jnp.arange(GROUP_SIZE)[:, None]).reshape(-1)
    wr = jnp.transpose(w_router)[perm].astype(BF16)
    br = b_router.astype(F32)[perm].reshape(N_EXPERTS, 1)
    mla_w = N_HEADS * V_DIM
    wgus = jnp.concatenate([w_gate_s, w_up_s], axis=1).astype(BF16)
    post = (row(g_out_attn), w_out[:mla_w].astype(BF16), w_out[mla_w:].astype(BF16), row(g_norm2),
            wr, br, wgus, w_down_s.astype(BF16))
    return w_ada.astype(BF16), b_ada, pre, post, (w_gate_e, w_up_e, w_down_e), row(g_final)


def _rope_tables(s):
    inv = 1.0 / (ROPE_THETA ** (jnp.arange(0, QK_ROPE, 2, dtype=F32) / QK_ROPE))
    ang_a = (jnp.arange(s // CHUNK, dtype=F32) * CHUNK)[:, None] * inv[None, :]
    ang_b = jnp.arange(CHUNK, dtype=F32)[:, None] * inv[None, :]
    ca, sa = jnp.cos(ang_a)[:, None, :], jnp.sin(ang_a)[:, None, :]
    cb, sb = jnp.cos(ang_b)[None], jnp.sin(ang_b)[None]
    cos = (ca * cb - sa * sb).reshape(s, HALF_ROPE)
    sin = (sa * cb + ca * sb).reshape(s, HALF_ROPE)
    z = lambda n: jnp.zeros((s, n), F32)
    tail = LANES - QK_NOPE - QK_ROPE
    c = jnp.concatenate([jnp.ones((s, QK_NOPE), F32), cos, cos, z(tail)], axis=1)
    s1 = jnp.concatenate([z(QK_NOPE), -sin, z(HALF_ROPE), z(tail)], axis=1)
    s2 = jnp.concatenate([z(QK_NOPE), z(HALF_ROPE), sin, z(tail)], axis=1)
    return c, s1, s2


def _tiles(s):
    ts = min(1024, s)
    tq = min(512, s)
    tkc = min(1024, s)
    blk = 1024
    tm = min(256, s)
    hps = 8 if s * 8 * HEAD_PAD * 2 * 2 <= VMEM_LIMIT // 4 else 2
    return ts, tq, tkc, blk, tm, hps


def _ceil_to(x, m):
    return (x + m - 1) // m * m


def _trunk(x, c, prep, tiles=None, rope=None):
    w_ada, b_ada, pre_w, post_w, exp_w, g_final = prep
    b, s, d = x.shape
    ts, tq, tkc, blk, tm, hps = tiles or _tiles(s)
    t = b * s
    nt = t // tm
    mod = _modulation(c, w_ada, b_ada)
    q, k, v, gm = _pre_call(x, mod, rope or _rope_tables(s), pre_w, ts)
    attn = _attn_call(q, k, v, tq, tkc, hps)
    ybase, h2p, e_arr, lrank, w_arr, cnt = _post_call(attn, gm, x, mod, post_w, min(ts, 1024), tm)

    i32 = jnp.int32
    cnt = cnt[:, :, 0].astype(i32).reshape(nt, GROUP_SIZE, N_GROUPS).transpose(0, 2, 1).reshape(nt, N_EXPERTS)
    cnt8 = _ceil_to(cnt, ROW_ALIGN)
    base8 = jnp.cumsum(cnt8, axis=0) - cnt8
    total8 = jnp.sum(cnt8, axis=0)
    padded = _ceil_to(total8 + EXPERT_SLACK, blk)
    pends = jnp.cumsum(padded).astype(i32)
    zfrom = ((pends - padded + total8) // blk * blk).astype(i32)
    dstbase = (pends - padded)[None, :] + base8
    toff8 = jnp.cumsum(cnt8, axis=1) - cnt8
    eids = jnp.arange(N_EXPERTS, dtype=i32)

    def chunk_table(nch, chunk, n_max):
        cend = jnp.cumsum(nch, axis=1)
        cidx = jnp.arange(n_max, dtype=i32)
        e_of_c = jnp.minimum(jnp.sum((cend[:, None, :] <= cidx[None, :, None]).astype(i32), axis=-1),
                             N_EXPERTS - 1)
        pick = lambda tbl: jnp.sum(jnp.where(e_of_c[..., None] == eids, tbl[:, None, :], 0), axis=-1)
        rel = lambda first_row: pick(first_row - (cend - nch) * chunk) + cidx[None, :] * chunk
        return rel, cend[:, -1]

    n_dmax = N_EXPERTS + TOP_K * tm // DISP_CHUNK
    nch_d = (cnt + DISP_CHUNK - 1) // DISP_CHUNK
    rel_d, n_dch = chunk_table(nch_d, DISP_CHUNK, n_dmax)
    n_rb = (jnp.sum(cnt8, axis=1) + DISP_CHUNK + PERM_ROWS - 1) // PERM_ROWS
    tail = lambda a, b2, n: jnp.concatenate([a[:, None], b2[:, None], jnp.zeros((nt, n - 2), i32)], axis=1)
    dinfo = jnp.stack([rel_d(toff8), rel_d(dstbase), tail(n_dch, n_rb, n_dmax)], axis=1).astype(i32)

    shift = dstbase % COMB_CHUNK
    nch_c = jnp.where(cnt > 0, (cnt + shift + COMB_CHUNK - 1) // COMB_CHUNK, 0)
    boff = (jnp.cumsum(nch_c, axis=1) - nch_c) * COMB_CHUNK
    big = COMB_CHUNK_TILES[0]
    n_cmax = max(_gather_rows(tm) // COMB_CHUNK // big, N_EXPERTS)
    src_t = (dstbase - shift) // COMB_CHUNK
    dst_t = boff // COMB_CHUNK
    done = nch_c // big * big
    rows, totals = [], []
    for size in COMB_CHUNK_TILES:
        n_size = nch_c // big if size == big else (nch_c - done == size).astype(i32)
        first = 0 if size == big else done
        rel, total = chunk_table(n_size, size, n_cmax)
        rows += [rel(src_t + first), rel(dst_t + first)]
        totals.append(total[:, None])
    n_kt = (jnp.sum(nch_c, axis=1) * COMB_CHUNK + COMB_KTILE - 1) // COMB_KTILE
    counts = jnp.concatenate(totals + [n_kt[:, None], jnp.zeros((nt, n_cmax - len(totals) - 1), i32)], axis=1)
    cinfo = jnp.stack(rows + [counts], axis=1).astype(i32)

    onehot = e_arr[..., None] == eids
    lookup = lambda tbl: jnp.sum(jnp.where(onehot, tbl[:, None, None, :], 0), axis=-1)
    lpos = lookup(toff8) + lrank
    cpos = lookup(boff + shift) + lrank
    tok_major = lambda a: jnp.transpose(a, (0, 2, 1)).reshape(t, TOP_K)

    n_rows = _ceil_to(t * TOP_K + nt * N_EXPERTS * (ROW_ALIGN - 1) + N_EXPERTS * (EXPERT_SLACK + blk - 1), blk)
    n_blocks = n_rows // blk
    n_used = (pends[-1] // blk).astype(i32).reshape(1)
    block_start = jnp.arange(n_blocks, dtype=i32) * blk
    block_exp = jnp.minimum(jnp.sum((pends[None, :] <= block_start[:, None]).astype(i32), axis=1),
                            N_EXPERTS - 1)

    xs = _disp_call(pends, zfrom, dinfo, lpos.astype(i32), h2p, n_rows, tm, blk)
    ys = _exp_call(block_exp, n_used, xs, *exp_w, blk)
    ys3 = ys.reshape(n_rows // COMB_CHUNK, COMB_CHUNK, d)
    return _comb_call(cinfo, ybase, mod, tok_major(cpos).astype(i32), tok_major(w_arr), g_final, ys3, tm)


def kernel(x_prompt, x_sample, c_prompt, c_sample, w_ada, b_ada, g_norm1, w_in, g_q_lat, w_uq, g_kv_lat, w_ukv, g_gmlp_v, w_spatial, b_spatial, g_out_attn, g_out_gmlp, w_out, g_norm2, w_router, b_router, w_gate_e, w_up_e, w_down_e, w_gate_s, w_up_s, w_down_s, g_final):
    prep = _prep_weights(w_ada[0], b_ada[0], g_norm1[0], w_in[0], g_q_lat[0], w_uq[0], g_kv_lat[0],
                         w_ukv[0], g_gmlp_v[0], w_spatial[0], b_spatial[0], g_out_attn[0],
                         g_out_gmlp[0], w_out[0], g_norm2[0], w_router[0], b_router[0], w_gate_e[0],
                         w_up_e[0], w_down_e[0], w_gate_s[0], w_up_s[0], w_down_s[0], g_final)
    rope = _rope_tables(max(x_prompt.shape[1], x_sample.shape[1]))
    return (_trunk(x_prompt, c_prompt, prep, rope=rope), _trunk(x_sample, c_sample, prep, rope=rope))
```

```python
import functools
import math

import jax
import jax.numpy as jnp
from jax import lax
from jax.experimental import pallas as pl
from jax.experimental.pallas import tpu as pltpu

F32 = jnp.float32
BF16 = jnp.bfloat16

D_MODEL = 1024
N_HEADS = 8
QK_NOPE = 64
QK_ROPE = 32
V_DIM = 64
Q_LORA = 256
KV_LORA = 128
GMLP_W = 512
CHUNK = 128
N_EXPERTS = 64
TOP_K = 8
N_GROUPS = 8
TOPK_GROUPS = 4
GROUP_SIZE = N_EXPERTS // N_GROUPS
EXPERT_DIM = 256
SHARED_DIM = 256
ROUTED_SCALE = 2.5
ROPE_THETA = 10000.0
N_MOD = 6
EPS = 1e-6

LANES = 128
HEAD_PAD = 128
HALF_ROPE = QK_ROPE // 2
VMEM_LIMIT = 52 * 1024 * 1024
ROW_ALIGN = 8
DISP_CHUNK = 48
COMB_CHUNK = 16
COMB_CHUNK_TILES = (3, 2, 1)
ATTN_CHUNKS_PER_STEP = 8
PERM_ROWS = 512
COMB_KTILE = 1024
EXPERT_SLACK = max(DISP_CHUNK, 2 * (COMB_CHUNK - 1))

SOFTMAX_SCALE = (QK_NOPE + QK_ROPE) ** -0.5
EXP2_SCALE = SOFTMAX_SCALE * math.log2(math.e)


def _rms(x, g):
    return x * lax.rsqrt(jnp.mean(x * x, axis=-1, keepdims=True) + EPS) * g


def _sigmoid(x):
    return 1.0 / (1.0 + jnp.exp(-x))


def _gelu_tanh(x):
    c = math.sqrt(2.0 / math.pi)
    return 0.5 * x * (1.0 + jnp.tanh(c * (x + 0.044715 * (x * x * x))))


def _dot(a, b):
    return jnp.dot(a, b, preferred_element_type=F32)


def _dot_nt(a, b):
    return lax.dot_general(a, b, (((1,), (1,)), ((), ())), preferred_element_type=F32)


def _pack_bf16_pairs(x):
    c = x.shape[1] // 2
    lo = pltpu.bitcast(x[:, :c].astype(BF16).astype(F32), jnp.uint32)
    hi = pltpu.bitcast(x[:, c:].astype(BF16).astype(F32), jnp.uint32)
    return (hi & jnp.uint32(0xFFFF0000)) | (lo >> 16)


def _unpack_bf16_pairs(w):
    lo = pltpu.bitcast(w << 16, F32)
    hi = pltpu.bitcast(w & jnp.uint32(0xFFFF0000), F32)
    return lo, hi


def _mod_kernel(c_ref, w_ref, b_ref, o_ref):
    c = c_ref[...]
    a = (c * _sigmoid(c)).astype(BF16)
    o_ref[...] = _dot(a, w_ref[...]) + b_ref[...]


def _modulation(c, w_ada_bf, b_ada):
    b = c.shape[0]
    bp = max(16, -(-b // 16) * 16)
    cp = jnp.pad(c, ((0, bp - b), (0, 0)))
    n = w_ada_bf.shape[1]
    tn = D_MODEL
    out = pl.pallas_call(
        _mod_kernel,
        name="mod",
        grid=(n // tn,),
        in_specs=[
            pl.BlockSpec((bp, D_MODEL), lambda j: (0, 0)),
            pl.BlockSpec((D_MODEL, tn), lambda j: (0, j)),
            pl.BlockSpec((1, tn), lambda j: (0, j)),
        ],
        out_specs=pl.BlockSpec((bp, tn), lambda j: (0, j)),
        out_shape=jax.ShapeDtypeStruct((bp, n), F32),
        compiler_params=pltpu.CompilerParams(dimension_semantics=("arbitrary",)),
    )(cp, w_ada_bf, b_ada.reshape(1, n))
    return out[:b].reshape(b, N_MOD, D_MODEL)


def _rope(xh, c, s1, s2):
    return (xh * c + pltpu.roll(xh, LANES - HALF_ROPE, axis=1) * s1
            + pltpu.roll(xh, HALF_ROPE, axis=1) * s2)


def _pre_kernel(x_ref, mod_ref, cos_ref, s1_ref, s2_ref, g1_ref, win_ref, gq_ref, wuq_ref,
                gkv_ref, wuk_ref, wuv_ref, vone_ref, ggv_ref, ws_ref, bs_ref, ggo_ref,
                q_ref, k_ref, v_ref, gm_ref, mix_ref):
    ts = x_ref.shape[1]
    x = x_ref[0]
    mod = mod_ref[0]
    h = _rms(x, g1_ref[...]) * (1.0 + mod[1:2]) + mod[0:1]
    z = _dot(h.astype(BF16), win_ref[...])
    o_kv = Q_LORA
    o_gu = o_kv + KV_LORA
    o_gv = o_gu + GMLP_W
    o_kr = o_gv + GMLP_W
    q_lat = z[:, :o_kv]
    kv_lat = z[:, o_kv:o_gu]
    g_u = z[:, o_gu:o_gv]
    g_v = z[:, o_gv:o_kr]
    kr = z[:, o_kr:o_kr + LANES]

    cos = cos_ref[...]
    s1 = s1_ref[...]
    s2 = s2_ref[...]

    qn = _rms(q_lat, gq_ref[...]).astype(BF16)
    q = _dot(qn, wuq_ref[...])
    kn = _rms(kv_lat, gkv_ref[...]).astype(BF16)
    kf = _dot(kn, wuk_ref[...])
    v_ref[0] = (_dot(kn, wuv_ref[...]) + vone_ref[...]).astype(BF16)
    krr = _rope(kr, cos, s1, s2)
    for hd in range(N_HEADS):
        sl = slice(hd * HEAD_PAD, (hd + 1) * HEAD_PAD)
        q_ref[0, :, sl] = (_rope(q[:, sl], cos, s1, s2) * EXP2_SCALE).astype(BF16)
        k_ref[0, sl, :] = jnp.transpose(kf[:, sl] + krr).astype(BF16)

    u = _gelu_tanh(g_u)
    vn = _rms(_gelu_tanh(g_v), ggv_ref[...]).astype(BF16)
    lane = lax.broadcasted_iota(jnp.int32, (CHUNK, LANES), 1)
    left = lane < (LANES // 2)
    zero = jnp.zeros((CHUNK, LANES), BF16)
    for n in range(ts // CHUNK):
        rs = slice(n * CHUNK, (n + 1) * CHUNK)
        for p in range(GMLP_W // LANES):
            cs = slice(p * LANES, (p + 1) * LANES)
            vp = vn[rs, cs]
            rhs = jnp.concatenate([jnp.where(left, vp, zero), jnp.where(left, zero, vp)], axis=0)
            mix_ref[rs, cs] = _dot(ws_ref[p], rhs) + bs_ref[:, cs]
    gm = u * mix_ref[...]
    gm_ref[0] = _rms(gm, ggo_ref[...]).astype(BF16)


def _pre_call(x, mod, tabs, wts, ts):
    b, s, d = x.shape
    cos, s1, s2 = tabs
    (g1, win, gq, wuq, gkv, wuk, wuv, vone, ggv, wsp, bsp, ggo) = wts
    full = lambda a: pl.BlockSpec(a.shape, lambda bi, i: (0,) * a.ndim)
    tab = pl.BlockSpec((ts, LANES), lambda bi, i: (i, 0))
    out_shapes = (
        jax.ShapeDtypeStruct((b, s, N_HEADS * HEAD_PAD), BF16),
        jax.ShapeDtypeStruct((b, N_HEADS * HEAD_PAD, s), BF16),
        jax.ShapeDtypeStruct((b, s, N_HEADS * HEAD_PAD), BF16),
        jax.ShapeDtypeStruct((b, s, GMLP_W), BF16),
    )
    tok = lambda w: pl.BlockSpec((1, ts, w), lambda bi, i: (bi, i, 0))
    tok_t = pl.BlockSpec((1, N_HEADS * HEAD_PAD, ts), lambda bi, i: (bi, 0, i))
    return pl.pallas_call(
        _pre_kernel,
        name="pre",
        grid=(b, s // ts),
        in_specs=[tok(d), pl.BlockSpec((1, N_MOD, d), lambda bi, i: (bi, 0, 0)), tab, tab, tab,
                  full(g1), full(win), full(gq), full(wuq), full(gkv), full(wuk), full(wuv), full(vone),
                  full(ggv), full(wsp), full(bsp), full(ggo)],
        out_specs=(tok(N_HEADS * HEAD_PAD), tok_t, tok(N_HEADS * HEAD_PAD), tok(GMLP_W)),
        out_shape=out_shapes,
        scratch_shapes=[pltpu.VMEM((ts, GMLP_W), F32)],
        compiler_params=pltpu.CompilerParams(dimension_semantics=("arbitrary", "arbitrary"),
                                             vmem_limit_bytes=VMEM_LIMIT),
    )(x, mod, cos, s1, s2, g1, win, gq, wuq, gkv, wuk, wuv, vone, ggv, wsp, bsp, ggo)


def _attn_kernel(q_ref, k_ref, v_ref, o_ref, *, tkc, cpb):
    tq = q_ref.shape[1]
    n_chunks = v_ref.shape[1] // tkc
    hps = q_ref.shape[2] // HEAD_PAD
    heads = [slice(hh * HEAD_PAD, (hh + 1) * HEAD_PAD) for hh in range(hps)]
    qs = [q_ref[0, :, hs] for hs in heads]

    def step(i, carry):
        carry = list(carry)
        for cc in range(cpb):
            off = pl.multiple_of((i * cpb + cc) * tkc, tkc)
            for hh, hs in enumerate(heads):
                m, acc = carry[2 * hh], carry[2 * hh + 1]
                kc = k_ref[0, hs, pl.ds(off, tkc)]
                vc = v_ref[0, pl.ds(off, tkc), hs]
                sc = _dot(qs[hh], kc)
                m_new = jnp.maximum(m, jnp.max(sc, axis=-1, keepdims=True))
                p = jnp.exp2(sc - m_new).astype(BF16)
                carry[2 * hh + 1] = jnp.exp2(m - m_new) * acc + _dot(p, vc)
                carry[2 * hh] = m_new
        return tuple(carry)

    n_steps = n_chunks // cpb
    init = (jnp.full((tq, 1), -jnp.inf, F32), jnp.zeros((tq, HEAD_PAD), F32)) * hps
    res = step(0, init) if n_steps == 1 else lax.fori_loop(0, n_steps, step, init)
    lane = lax.broadcasted_iota(jnp.int32, (tq, HEAD_PAD), 1)
    for pair in range(hps // 2):
        a0, a1 = res[4 * pair + 1], res[4 * pair + 3]
        o0 = a0 / a0[:, V_DIM:V_DIM + 1]
        o1 = a1 / a1[:, V_DIM:V_DIM + 1]
        o_ref[0, :, pair * HEAD_PAD:(pair + 1) * HEAD_PAD] = jnp.where(
            lane < V_DIM, o0, pltpu.roll(o1, V_DIM, axis=1)).astype(BF16)


def _attn_call(q, k, v, tq, tkc, hps):
    b, s, _ = q.shape
    kv_bytes = 2 * s * hps * HEAD_PAD * 2
    mode = pl.Buffered(2 if 2 * kv_bytes <= VMEM_LIMIT // 3 else 1)
    k_res = pl.BlockSpec((1, hps * HEAD_PAD, s), lambda bi, h, i: (bi, h, 0), pipeline_mode=mode)
    v_res = pl.BlockSpec((1, s, hps * HEAD_PAD), lambda bi, h, i: (bi, 0, h), pipeline_mode=mode)
    return pl.pallas_call(
        functools.partial(_attn_kernel, tkc=tkc, cpb=math.gcd(s // tkc, ATTN_CHUNKS_PER_STEP)),
        name="attn",
        grid=(b, N_HEADS // hps, s // tq),
        in_specs=[pl.BlockSpec((1, tq, hps * HEAD_PAD), lambda bi, h, i: (bi, i, h)), k_res, v_res],
        out_specs=pl.BlockSpec((1, tq, hps * V_DIM), lambda bi, h, i: (bi, i, h)),
        out_shape=jax.ShapeDtypeStruct((b, s, N_HEADS * V_DIM), BF16),
        compiler_params=pltpu.CompilerParams(
            dimension_semantics=("arbitrary", "arbitrary", "arbitrary"),
            vmem_limit_bytes=VMEM_LIMIT),
    )(q, k, v)


def _post_kernel(attn_ref, gm_ref, x_ref, mod_ref, goa_ref, woa_ref, wog_ref, g2_ref, wr_ref,
                 br_ref, wgus_ref, wds_ref,
                 ybase_ref, h2p_ref, e_ref, rank_ref, w_ref, cnt_ref):
    ts = x_ref.shape[1]
    mod = mod_ref[0]
    ga1, sh2, sc2, ga2 = mod[2:3], mod[3:4], mod[4:5], mod[5:6]
    an = _rms(attn_ref[0].astype(F32), goa_ref[...]).astype(BF16)
    y = _dot(an, woa_ref[...]) + _dot(gm_ref[0], wog_ref[...])
    x1 = x_ref[0] + ga1 * y
    h2 = _rms(x1, g2_ref[...]) * (1.0 + sc2) + sh2
    h2b = h2.astype(BF16)
    h2p_ref[...] = _pack_bf16_pairs(h2)

    gu = _dot(h2b, wgus_ref[...])
    g, u = gu[:, :SHARED_DIM], gu[:, SHARED_DIM:]
    a = (g * _sigmoid(g) * u).astype(BF16)
    ybase_ref[0] = x1 + ga2 * _dot(a, wds_ref[...])

    logits = _dot_nt(wr_ref[...], h2b)
    scores = _sigmoid(logits)
    biased = scores + br_ref[...]
    ninf = jnp.float32(-jnp.inf)
    bj = [biased[j * N_GROUPS:(j + 1) * N_GROUPS] for j in range(GROUP_SIZE)]
    sj = [scores[j * N_GROUPS:(j + 1) * N_GROUPS] for j in range(GROUP_SIZE)]
    m1 = bj[0]
    for j in range(1, GROUP_SIZE):
        m1 = jnp.maximum(m1, bj[j])
    found = jnp.zeros_like(m1)
    m2 = jnp.full_like(m1, ninf)
    for j in range(GROUP_SIZE):
        eq = jnp.where(bj[j] == m1, 1.0, 0.0)
        is_first = eq * (1.0 - found)
        found = jnp.maximum(found, eq)
        m2 = jnp.maximum(m2, jnp.where(is_first > 0.0, ninf, bj[j]))
    gs = m1 + m2
    gidx = lax.broadcasted_iota(jnp.int32, gs.shape, 0)
    grank = jnp.zeros_like(gs)
    for kk in range(1, N_GROUPS):
        r = pltpu.roll(gs, kk, axis=0)
        grank = grank + jnp.where(gidx >= kk, jnp.where(r >= gs, 1.0, 0.0), jnp.where(r > gs, 1.0, 0.0))
    gsel = grank < float(TOPK_GROUPS)
    masked = [jnp.where(gsel, bj[j], ninf) for j in range(GROUP_SIZE)]
    eidx = [gidx * GROUP_SIZE + j for j in range(GROUP_SIZE)]

    selm = [jnp.zeros_like(gs) for _ in range(GROUP_SIZE)]
    e_sel = []
    for _k in range(TOP_K):
        m = masked[0]
        for j in range(1, GROUP_SIZE):
            m = jnp.maximum(m, masked[j])
        m = jnp.max(m, axis=0, keepdims=True)
        cand = jnp.where(masked[0] == m, eidx[0], N_EXPERTS)
        for j in range(1, GROUP_SIZE):
            cand = jnp.minimum(cand, jnp.where(masked[j] == m, eidx[j], N_EXPERTS))
        emin = jnp.min(cand, axis=0, keepdims=True)
        e_sel.append(emin)
        for j in range(GROUP_SIZE):
            hit = eidx[j] == emin
            selm[j] = jnp.where(hit, 1.0, selm[j])
            masked[j] = jnp.where(hit, ninf, masked[j])

    wsel = [selm[j] * sj[j] for j in range(GROUP_SIZE)]
    tot = wsel[0]
    for j in range(1, GROUP_SIZE):
        tot = tot + wsel[j]
    tot = jnp.sum(tot, axis=0, keepdims=True)
    wn = [wsel[j] / tot * ROUTED_SCALE for j in range(GROUP_SIZE)]

    tm = e_ref.shape[2]
    sel = jnp.concatenate(selm, axis=0)
    tr = lax.broadcasted_iota(jnp.int32, (tm, tm), 0)
    tc = lax.broadcasted_iota(jnp.int32, (tm, tm), 1)
    upper = jnp.where(tr < tc, 1.0, 0.0).astype(BF16)
    subs = [slice(i * tm, (i + 1) * tm) for i in range(ts // tm)]
    selb = sel.astype(BF16)
    rank_full = jnp.concatenate([_dot(selb[:, sub], upper) for sub in subs], axis=1)
    rj = [rank_full[j * N_GROUPS:(j + 1) * N_GROUPS] for j in range(GROUP_SIZE)]
    for i, sub in enumerate(subs):
        cnt_ref[i] = jnp.broadcast_to(jnp.sum(sel[:, sub], axis=1, keepdims=True), (N_EXPERTS, LANES))

    for k in range(TOP_K):
        rk = jnp.zeros_like(gs)
        wk = jnp.zeros_like(gs)
        for j in range(GROUP_SIZE):
            hit = eidx[j] == e_sel[k]
            rk = rk + jnp.where(hit, rj[j], 0.0)
            wk = wk + jnp.where(hit, wn[j], 0.0)
        rk = jnp.sum(rk, axis=0, keepdims=True).astype(jnp.int32)
        wk = jnp.sum(wk, axis=0, keepdims=True)
        for i, sub in enumerate(subs):
            e_ref[i, k:k + 1, :] = e_sel[k][:, sub]
            rank_ref[i, k:k + 1, :] = rk[:, sub]
            w_ref[i, k:k + 1, :] = wk[:, sub]


def _post_call(attn, gm, x, mod, wts, ts, tm):
    b, s, d = x.shape
    nt = s // ts
    nsub = ts // tm
    ntm = b * nt * nsub
    (goa, woa, wog, g2, wr, br, wgus, wds) = wts
    full = lambda a: pl.BlockSpec(a.shape, lambda bi, i: (0,) * a.ndim)
    tok = lambda w: pl.BlockSpec((1, ts, w), lambda bi, i: (bi, i, 0))
    rout = pl.BlockSpec((nsub, TOP_K, tm), lambda bi, i: (bi * nt + i, 0, 0))
    out_shapes = (
        jax.ShapeDtypeStruct((b, s, d), F32),
        jax.ShapeDtypeStruct((b * s, d // 2), jnp.uint32),
        jax.ShapeDtypeStruct((ntm, TOP_K, tm), jnp.int32),
        jax.ShapeDtypeStruct((ntm, TOP_K, tm), jnp.int32),
        jax.ShapeDtypeStruct((ntm, TOP_K, tm), F32),
        jax.ShapeDtypeStruct((ntm, N_EXPERTS, LANES), F32),
    )
    return pl.pallas_call(
        _post_kernel,
        name="post",
        grid=(b, nt),
        in_specs=[tok(N_HEADS * V_DIM), tok(GMLP_W), tok(d),
                  pl.BlockSpec((1, N_MOD, d), lambda bi, i: (bi, 0, 0)),
                  full(goa), full(woa), full(wog), full(g2), full(wr), full(br), full(wgus), full(wds)],
        out_specs=(tok(d), pl.BlockSpec((ts, d // 2), lambda bi, i: (bi * nt + i, 0)),
                   rout, rout, rout,
                   pl.BlockSpec((nsub, N_EXPERTS, LANES), lambda bi, i: (bi * nt + i, 0, 0))),
        out_shape=out_shapes,
        compiler_params=pltpu.CompilerParams(dimension_semantics=("arbitrary", "arbitrary"),
                                             vmem_limit_bytes=VMEM_LIMIT),
    )(attn, gm, x, mod, goa, woa, wog, g2, wr, br, wgus, wds)


def _disp_kernel(pends_ref, zfrom_ref, info_ref, lpos_ref, h_ref, xs_ref, sbuf, zbuf, nprev, sems, zsem):
    td = h_ref.shape[0]
    blk = zbuf.shape[0]
    step = pl.program_id(0)
    slot = lax.rem(step, 2)

    @pl.when(step == 0)
    def _():
        nprev[0] = 0
        zbuf[...] = jnp.zeros_like(zbuf)

        def tail_copy(e, j):
            start = pl.multiple_of(zfrom_ref[e] + j * blk, blk)
            return pltpu.make_async_copy(zbuf, xs_ref.at[pl.ds(start, blk), :], zsem)

        def n_tail(e):
            return (pends_ref[e] - zfrom_ref[e]) // blk

        def zstart(e, c):
            def one(j, c2):
                tail_copy(e, j).start()
                return c2
            return lax.fori_loop(0, n_tail(e), one, c)

        def zwait(e, c):
            def one(j, c2):
                tail_copy(e, j).wait()
                return c2
            return lax.fori_loop(0, n_tail(e), one, c)

        lax.fori_loop(0, N_EXPERTS, zstart, 0)
        lax.fori_loop(0, N_EXPERTS, zwait, 0)

    lo, hi = _unpack_bf16_pairs(h_ref[...])
    lo = lo.astype(BF16)
    hi = hi.astype(BF16)
    lpos = lpos_ref[0]
    lpos_b = [jnp.broadcast_to(lpos[k:k + 1, :], (PERM_ROWS, td)).astype(jnp.int16) for k in range(TOP_K)]
    row0 = lax.broadcasted_iota(jnp.int32, (PERM_ROWS, td), 0)
    one = jnp.ones((PERM_ROWS, td), BF16)

    def build(rb, c):
        r0 = pl.multiple_of(rb * PERM_ROWS, PERM_ROWS)
        riota = (row0 + r0).astype(jnp.int16)
        pb = jnp.zeros((PERM_ROWS, td), BF16)
        for k in range(TOP_K):
            pb = jnp.where(lpos_b[k] == riota, one, pb)
        xlo = pltpu.bitcast(_dot(pb, lo), jnp.uint32)
        xhi = pltpu.bitcast(_dot(pb, hi), jnp.uint32)
        sbuf[slot, pl.ds(r0, PERM_ROWS), :] = (xhi & jnp.uint32(0xFFFF0000)) | (xlo >> 16)
        return c

    lax.fori_loop(0, info_ref[0, 2, 1], build, 0)

    def chunk_copy(src, dst, sl):
        return pltpu.make_async_copy(
            sbuf.at[sl, pl.ds(pl.multiple_of(src, ROW_ALIGN), DISP_CHUNK), :],
            xs_ref.at[pl.ds(pl.multiple_of(dst, ROW_ALIGN), DISP_CHUNK), :], sems.at[sl])

    def drain(n, sl):
        def one(c, carry):
            chunk_copy(0, 0, sl).wait()
            return carry
        lax.fori_loop(0, n, one, 0)

    def issue(c, carry):
        chunk_copy(info_ref[0, 0, c], info_ref[0, 1, c], slot).start()
        return carry

    drain(nprev[0], 1 - slot)
    n_chunks = info_ref[0, 2, 0]
    lax.fori_loop(0, n_chunks, issue, 0)
    nprev[0] = n_chunks

    @pl.when(step == pl.num_programs(0) - 1)
    def _():
        drain(n_chunks, slot)


def _sorted_rows(td):
    need = TOP_K * td + N_EXPERTS * (ROW_ALIGN - 1) + DISP_CHUNK
    return -(-need // PERM_ROWS) * PERM_ROWS


def _disp_call(pends, zfrom, info, lpos, h2p, n_rows, td, blk):
    t, w = h2p.shape
    nt = t // td
    return pl.pallas_call(
        _disp_kernel,
        name="disp",
        grid_spec=pltpu.PrefetchScalarGridSpec(
            num_scalar_prefetch=2,
            grid=(nt,),
            in_specs=[pl.BlockSpec((1,) + info.shape[1:], lambda i, pe, pa: (i, 0, 0), memory_space=pltpu.SMEM),
                      pl.BlockSpec((1, TOP_K, td), lambda i, pe, pa: (i, 0, 0)),
                      pl.BlockSpec((td, w), lambda i, pe, pa: (i, 0))],
            out_specs=pl.BlockSpec(memory_space=pl.ANY),
            scratch_shapes=[pltpu.VMEM((2, _sorted_rows(td), w), jnp.uint32),
                            pltpu.VMEM((blk, w), jnp.uint32),
                            pltpu.SMEM((1,), jnp.int32),
                            pltpu.SemaphoreType.DMA((2,)), pltpu.SemaphoreType.DMA(())],
        ),
        out_shape=jax.ShapeDtypeStruct((n_rows, w), jnp.uint32),
        compiler_params=pltpu.CompilerParams(dimension_semantics=("arbitrary",),
                                             vmem_limit_bytes=VMEM_LIMIT),
    )(pends, zfrom, info, lpos, h2p)


def _exp_kernel(bexp_ref, nused_ref, xs_ref, wg_ref, wu_ref, wd_ref, ys_ref):
    i = pl.program_id(0)

    @pl.when(i < nused_ref[0])
    def _():
        half = D_MODEL // 2
        lo, hi = _unpack_bf16_pairs(xs_ref[...])
        lo = lo.astype(BF16)
        hi = hi.astype(BF16)
        wg = wg_ref[0].astype(BF16)
        wu = wu_ref[0].astype(BF16)
        g = _dot(lo, wg[:half]) + _dot(hi, wg[half:])
        u = _dot(lo, wu[:half]) + _dot(hi, wu[half:])
        a = (g * _sigmoid(g) * u).astype(BF16)
        ys_ref[...] = _dot(a, wd_ref[0].astype(BF16)).astype(BF16)


def _exp_call(block_exp, n_used, xs, wg, wu, wd, blk):
    n_rows, w = xs.shape
    n_blocks = n_rows // blk

    def row_map(i, bexp, nused):
        return (jnp.minimum(i, nused[0] - 1), 0)

    def w_map(i, bexp, nused):
        return (bexp[jnp.minimum(i, nused[0] - 1)], 0, 0)

    return pl.pallas_call(
        _exp_kernel,
        name="exp",
        grid_spec=pltpu.PrefetchScalarGridSpec(
            num_scalar_prefetch=2,
            grid=(n_blocks,),
            in_specs=[pl.BlockSpec((blk, w), row_map),
                      pl.BlockSpec((1, D_MODEL, EXPERT_DIM), w_map),
                      pl.BlockSpec((1, D_MODEL, EXPERT_DIM), w_map),
                      pl.BlockSpec((1, EXPERT_DIM, D_MODEL), w_map)],
            out_specs=pl.BlockSpec((blk, D_MODEL), row_map),
        ),
        out_shape=jax.ShapeDtypeStruct((n_rows, D_MODEL), BF16),
        compiler_params=pltpu.CompilerParams(dimension_semantics=("arbitrary",),
                                             vmem_limit_bytes=VMEM_LIMIT),
    )(block_exp, n_used, xs, wg, wu, wd)


def _comb_kernel(info_ref, next_info_ref, ybase_ref, mod_ref, cpos_ref, w_ref, gf_ref, ys_ref, o_ref,
                 ybuf, acc, sems):
    tc = ybase_ref.shape[1]
    step = pl.program_id(0) * pl.num_programs(1) + pl.program_id(1)
    n_steps = pl.num_programs(0) * pl.num_programs(1)
    slot = lax.rem(step, 2)

    def chunk_copy(src, dst, sl, n_tiles):
        return pltpu.make_async_copy(ys_ref.at[pl.ds(src, n_tiles)], ybuf.at[sl, pl.ds(dst, n_tiles)],
                                     sems.at[sl])

    def fetch(iref, sl):
        for i, n_tiles in enumerate(COMB_CHUNK_TILES):
            def issue(c, carry):
                chunk_copy(iref[0, 2 * i, c], iref[0, 2 * i + 1, c], sl, n_tiles).start()
                return carry
            lax.fori_loop(0, iref[0, 2 * len(COMB_CHUNK_TILES), i], issue, 0)

    @pl.when(step == 0)
    def _():
        ybuf[...] = jnp.zeros_like(ybuf)
        fetch(info_ref, 0)

    @pl.when(step + 1 < n_steps)
    def _():
        fetch(next_info_ref, 1 - slot)

    for i, n_tiles in enumerate(COMB_CHUNK_TILES):
        def drain(c, carry):
            chunk_copy(0, 0, slot, n_tiles).wait()
            return carry
        lax.fori_loop(0, info_ref[0, 2 * len(COMB_CHUNK_TILES), i], drain, 0)

    cpos = cpos_ref[...]
    w = w_ref[...]
    cpos_b = [jnp.broadcast_to(cpos[:, k:k + 1], (tc, LANES)).astype(jnp.int16) for k in range(TOP_K)]
    w_b = [jnp.broadcast_to(w[:, k:k + 1], (tc, LANES)).astype(BF16) for k in range(TOP_K)]
    lane = lax.broadcasted_iota(jnp.int32, (tc, LANES), 1)
    acc[...] = jnp.zeros_like(acc)

    def slab(kt, c):
        k0 = pl.multiple_of(kt * COMB_KTILE, COMB_KTILE)
        cols = []
        for j in range(COMB_KTILE // LANES):
            col = (lane + (k0 + j * LANES)).astype(jnp.int16)
            wm = jnp.zeros((tc, LANES), BF16)
            for k in range(TOP_K):
                wm = jnp.where(cpos_b[k] == col, w_b[k], wm)
            cols.append(wm)
        rows = ybuf[slot, pl.ds(kt * (COMB_KTILE // COMB_CHUNK), COMB_KTILE // COMB_CHUNK)]
        acc[...] += _dot(jnp.concatenate(cols, axis=1), rows.reshape(COMB_KTILE, rows.shape[-1]))
        return c

    lax.fori_loop(0, info_ref[0, 2 * len(COMB_CHUNK_TILES), len(COMB_CHUNK_TILES)], slab, 0)
    ga2 = mod_ref[0][5:6]
    o_ref[0] = _rms(ybase_ref[0] + ga2 * acc[...], gf_ref[...])


def _gather_rows(tc):
    need = TOP_K * tc + N_EXPERTS * 2 * (COMB_CHUNK - 1)
    return -(-need // COMB_KTILE) * COMB_KTILE


def _comb_call(info, ybase, mod, cpos_tok, w_tok, g_final, ys, tc):
    b, s, d = ybase.shape
    nt = s // tc
    return pl.pallas_call(
        _comb_kernel,
        name="comb",
        grid=(b, nt),
        in_specs=[pl.BlockSpec((1,) + info.shape[1:], lambda bi, i: (bi * nt + i, 0, 0), memory_space=pltpu.SMEM),
                  pl.BlockSpec((1,) + info.shape[1:], lambda bi, i: (jnp.minimum(bi * nt + i + 1, b * nt - 1), 0, 0),
                               memory_space=pltpu.SMEM),
                  pl.BlockSpec((1, tc, d), lambda bi, i: (bi, i, 0)),
                  pl.BlockSpec((1, N_MOD, d), lambda bi, i: (bi, 0, 0)),
                  pl.BlockSpec((tc, TOP_K), lambda bi, i: (bi * nt + i, 0)),
                  pl.BlockSpec((tc, TOP_K), lambda bi, i: (bi * nt + i, 0)),
                  pl.BlockSpec((1, d), lambda bi, i: (0, 0)),
                  pl.BlockSpec(memory_space=pl.ANY)],
        out_specs=pl.BlockSpec((1, tc, d), lambda bi, i: (bi, i, 0)),
        out_shape=jax.ShapeDtypeStruct((b, s, d), F32),
        scratch_shapes=[pltpu.VMEM((2, _gather_rows(tc) // COMB_CHUNK, COMB_CHUNK, d), BF16),
                        pltpu.VMEM((tc, d), F32), pltpu.SemaphoreType.DMA((2,))],
        compiler_params=pltpu.CompilerParams(dimension_semantics=("arbitrary", "arbitrary"),
                                             vmem_limit_bytes=VMEM_LIMIT),
    )(info, info, ybase, mod, cpos_tok, w_tok, g_final, ys)


def _prep_weights(w_ada, b_ada, g_norm1, w_in, g_q_lat, w_uq, g_kv_lat, w_ukv, g_gmlp_v, w_spatial,
                  b_spatial, g_out_attn, g_out_gmlp, w_out, g_norm2, w_router, b_router, w_gate_e,
                  w_up_e, w_down_e, w_gate_s, w_up_s, w_down_s, g_final):
    row = lambda g: g.reshape(1, -1).astype(F32)
    o1 = Q_LORA
    o2 = o1 + KV_LORA
    o3 = o2 + QK_ROPE
    o4 = o3 + GMLP_W
    kr_cols = jnp.pad(w_in[:, o2:o3], ((0, 0), (QK_NOPE, LANES - QK_NOPE - QK_ROPE)))
    win = jnp.concatenate([w_in[:, :o2], w_in[:, o3:o4], w_in[:, o4:], kr_cols], axis=1).astype(BF16)
    qd = QK_NOPE + QK_ROPE
    wuq = jnp.pad(w_uq.reshape(Q_LORA, N_HEADS, qd), ((0, 0), (0, 0), (0, HEAD_PAD - qd)))
    wuq = wuq.reshape(Q_LORA, N_HEADS * HEAD_PAD).astype(BF16)
    wkv = w_ukv.reshape(KV_LORA, N_HEADS, QK_NOPE + V_DIM)
    wuk = jnp.pad(wkv[:, :, :QK_NOPE], ((0, 0), (0, 0), (0, HEAD_PAD - QK_NOPE)))
    wuk = wuk.reshape(KV_LORA, N_HEADS * HEAD_PAD).astype(BF16)
    wuv = jnp.pad(wkv[:, :, QK_NOPE:], ((0, 0), (0, 0), (0, HEAD_PAD - V_DIM)))
    wuv = wuv.reshape(KV_LORA, N_HEADS * HEAD_PAD).astype(BF16)
    vone = jnp.tile((jnp.arange(HEAD_PAD) == V_DIM).astype(F32), N_HEADS).reshape(1, -1)
    wsp = w_spatial.reshape(N_HEADS // 2, 2, CHUNK, CHUNK).transpose(0, 2, 1, 3)
    wsp = wsp.reshape(N_HEADS // 2, CHUNK, 2 * CHUNK).astype(BF16)
    bsp = jnp.repeat(jnp.transpose(b_spatial), GMLP_W // N_HEADS, axis=1).astype(F32)
    pre = (row(g_norm1), win, row(g_q_lat), wuq, row(g_kv_lat), wuk, wuv, vone, row(g_gmlp_v), wsp,
           bsp, row(g_out_gmlp))
    perm = (jnp.arange(N_GROUPS)[None, :] * GROUP_SIZE + jnp.arange(GROUP_SIZE)[:, None]).reshape(-1)
    wr = jnp.transpose(w_router)[perm].astype(BF16)
    br = b_router.astype(F32)[perm].reshape(N_EXPERTS, 1)
    mla_w = N_HEADS * V_DIM
    wgus = jnp.concatenate([w_gate_s, w_up_s], axis=1).astype(BF16)
    post = (row(g_out_attn), w_out[:mla_w].astype(BF16), w_out[mla_w:].astype(BF16), row(g_norm2),
            wr, br, wgus, w_down_s.astype(BF16))
    return w_ada.astype(BF16), b_ada, pre, post, (w_gate_e, w_up_e, w_down_e), row(g_final)


def _rope_tables(s):
    inv = 1.0 / (ROPE_THETA ** (jnp.arange(0, QK_ROPE, 2, dtype=F32) / QK_ROPE))
    ang_a = (jnp.arange(s // CHUNK, dtype=F32) * CHUNK)[:, None] * inv[None, :]
    ang_b = jnp.arange(CHUNK, dtype=F32)[:, None] * inv[None, :]
    ca, sa = jnp.cos(ang_a)[:, None, :], jnp.sin(ang_a)[:, None, :]
    cb, sb = jnp.cos(ang_b)[None], jnp.sin(ang_b)[None]
    cos = (ca * cb - sa * sb).reshape(s, HALF_ROPE)
    sin = (sa * cb + ca * sb).reshape(s, HALF_ROPE)
    z = lambda n: jnp.zeros((s, n), F32)
    tail = LANES - QK_NOPE - QK_ROPE
    c = jnp.concatenate([jnp.ones((s, QK_NOPE), F32), cos, cos, z(tail)], axis=1)
    s1 = jnp.concatenate([z(QK_NOPE), -sin, z(HALF_ROPE), z(tail)], axis=1)
    s2 = jnp.concatenate([z(QK_NOPE), z(HALF_ROPE), sin, z(tail)], axis=1)
    return c, s1, s2


def _tiles(s):
    ts = min(1024, s)
    tq = min(512, s)
    tkc = min(1024, s)
    blk = 1024
    tm = min(256, s)
    hps = 8 if s * 8 * HEAD_PAD * 2 * 2 <= VMEM_LIMIT // 4 else 2
    return ts, tq, tkc, blk, tm, hps


def _ceil_to(x, m):
    return (x + m - 1) // m * m


def _trunk(x, c, prep, tiles=None, rope=None):
    w_ada, b_ada, pre_w, post_w, exp_w, g_final = prep
    b, s, d = x.shape
    ts, tq, tkc, blk, tm, hps = tiles or _tiles(s)
    t = b * s
    nt = t // tm
    mod = _modulation(c, w_ada, b_ada)
    q, k, v, gm = _pre_call(x, mod, rope or _rope_tables(s), pre_w, ts)
    attn = _attn_call(q, k, v, tq, tkc, hps)
    ybase, h2p, e_arr, lrank, w_arr, cnt = _post_call(attn, gm, x, mod, post_w, min(ts, 1024), tm)

    i32 = jnp.int32
    cnt = cnt[:, :, 0].astype(i32).reshape(nt, GROUP_SIZE, N_GROUPS).transpose(0, 2, 1).reshape(nt, N_EXPERTS)
    cnt8 = _ceil_to(cnt, ROW_ALIGN)
    base8 = jnp.cumsum(cnt8, axis=0) - cnt8
    total8 = jnp.sum(cnt8, axis=0)
    padded = _ceil_to(total8 + EXPERT_SLACK, blk)
    pends = jnp.cumsum(padded).astype(i32)
    zfrom = ((pends - padded + total8) // blk * blk).astype(i32)
    dstbase = (pends - padded)[None, :] + base8
    toff8 = jnp.cumsum(cnt8, axis=1) - cnt8
    eids = jnp.arange(N_EXPERTS, dtype=i32)

    def chunk_table(nch, chunk, n_max):
        cend = jnp.cumsum(nch, axis=1)
        cidx = jnp.arange(n_max, dtype=i32)
        e_of_c = jnp.minimum(jnp.sum((cend[:, None, :] <= cidx[None, :, None]).astype(i32), axis=-1),
                             N_EXPERTS - 1)
        pick = lambda tbl: jnp.sum(jnp.where(e_of_c[..., None] == eids, tbl[:, None, :], 0), axis=-1)
        rel = lambda first_row: pick(first_row - (cend - nch) * chunk) + cidx[None, :] * chunk
        return rel, cend[:, -1]

    n_dmax = N_EXPERTS + TOP_K * tm // DISP_CHUNK
    nch_d = (cnt + DISP_CHUNK - 1) // DISP_CHUNK
    rel_d, n_dch = chunk_table(nch_d, DISP_CHUNK, n_dmax)
    n_rb = (jnp.sum(cnt8, axis=1) + DISP_CHUNK + PERM_ROWS - 1) // PERM_ROWS
    tail = lambda a, b2, n: jnp.concatenate([a[:, None], b2[:, None], jnp.zeros((nt, n - 2), i32)], axis=1)
    dinfo = jnp.stack([rel_d(toff8), rel_d(dstbase), tail(n_dch, n_rb, n_dmax)], axis=1).astype(i32)

    shift = dstbase % COMB_CHUNK
    nch_c = jnp.where(cnt > 0, (cnt + shift + COMB_CHUNK - 1) // COMB_CHUNK, 0)
    boff = (jnp.cumsum(nch_c, axis=1) - nch_c) * COMB_CHUNK
    big = COMB_CHUNK_TILES[0]
    n_cmax = max(_gather_rows(tm) // COMB_CHUNK // big, N_EXPERTS)
    src_t = (dstbase - shift) // COMB_CHUNK
    dst_t = boff // COMB_CHUNK
    done = nch_c // big * big
    rows, totals = [], []
    for size in COMB_CHUNK_TILES:
        n_size = nch_c // big if size == big else (nch_c - done == size).astype(i32)
        first = 0 if size == big else done
        rel, total = chunk_table(n_size, size, n_cmax)
        rows += [rel(src_t + first), rel(dst_t + first)]
        totals.append(total[:, None])
    n_kt = (jnp.sum(nch_c, axis=1) * COMB_CHUNK + COMB_KTILE - 1) // COMB_KTILE
    counts = jnp.concatenate(totals + [n_kt[:, None], jnp.zeros((nt, n_cmax - len(totals) - 1), i32)], axis=1)
    cinfo = jnp.stack(rows + [counts], axis=1).astype(i32)

    onehot = e_arr[..., None] == eids
    lookup = lambda tbl: jnp.sum(jnp.where(onehot, tbl[:, None, None, :], 0), axis=-1)
    lpos = lookup(toff8) + lrank
    cpos = lookup(boff + shift) + lrank
    tok_major = lambda a: jnp.transpose(a, (0, 2, 1)).reshape(t, TOP_K)

    n_rows = _ceil_to(t * TOP_K + nt * N_EXPERTS * (ROW_ALIGN - 1) + N_EXPERTS * (EXPERT_SLACK + blk - 1), blk)
    n_blocks = n_rows // blk
    n_used = (pends[-1] // blk).astype(i32).reshape(1)
    block_start = jnp.arange(n_blocks, dtype=i32) * blk
    block_exp = jnp.minimum(jnp.sum((pends[None, :] <= block_start[:, None]).astype(i32), axis=1),
                            N_EXPERTS - 1)

    xs = _disp_call(pends, zfrom, dinfo, lpos.astype(i32), h2p, n_rows, tm, blk)
    ys = _exp_call(block_exp, n_used, xs, *exp_w, blk)
    ys3 = ys.reshape(n_rows // COMB_CHUNK, COMB_CHUNK, d)
    return _comb_call(cinfo, ybase, mod, tok_major(cpos).astype(i32), tok_major(w_arr), g_final, ys3, tm)


def kernel(x_prompt, x_sample, c_prompt, c_sample, w_ada, b_ada, g_norm1, w_in, g_q_lat, w_uq, g_kv_lat, w_ukv, g_gmlp_v, w_spatial, b_spatial, g_out_attn, g_out_gmlp, w_out, g_norm2, w_router, b_router, w_gate_e, w_up_e, w_down_e, w_gate_s, w_up_s, w_down_s, g_final):
    prep = _prep_weights(w_ada[0], b_ada[0], g_norm1[0], w_in[0], g_q_lat[0], w_uq[0], g_kv_lat[0],
                         w_ukv[0], g_gmlp_v[0], w_spatial[0], b_spatial[0], g_out_attn[0],
                         g_out_gmlp[0], w_out[0], g_norm2[0], w_router[0], b_router[0], w_gate_e[0],
                         w_up_e[0], w_down_e[0], w_gate_s[0], w_up_s[0], w_down_s[0], g_final)
    rope = _rope_tables(max(x_prompt.shape[1], x_sample.shape[1]))
    return (_trunk(x_prompt, c_prompt, prep, rope=rope), _trunk(x_sample, c_sample, prep, rope=rope))
```
